```python
import math
import jax
import jax.numpy as jnp
from jax import lax
import numpy as np

D_MODEL = 2048
BATCH = 1
SEQ = 16384
DEPTH = 4

GRID_W = 64
CTX_LEN = 256
EPS = 1e-6
NEG_INF = -1e30
ROPE_BASE = 10000.0
ROPE_AXIS_FREQS = 32

DIFF_HEADS = 4
DIFF_DH = 128
SWA_HEADS = 8
SWA_KV_HEADS = 2
SWA_GROUP = SWA_HEADS // SWA_KV_HEADS
SWA_DH = 128
WINDOW = 128
Q_BLOCK = 128
DN_QK_HEADS = 16
DN_V_HEADS = 32
DN_DK = 128
DN_DV = 128
DN_CHUNK = 64
SHORT_CONV_W = 3
FFN_HIDDEN = 5632
FFN_CONV_W = 3

N_EVEN = (DEPTH + 1) // 2
N_ODD = DEPTH // 2

DIFF_OUT = DIFF_HEADS * 2 * DIFF_DH
SWA_OUT = SWA_HEADS * SWA_DH
ATTN_OUT = DIFF_OUT + SWA_OUT
ATTN_IN_SIZES = (DIFF_HEADS * 2 * DIFF_DH, DIFF_HEADS * 2 * DIFF_DH, DIFF_HEADS * 2 * DIFF_DH,
                 SWA_HEADS * SWA_DH, SWA_KV_HEADS * SWA_DH, SWA_KV_HEADS * SWA_DH)
ATTN_IN = sum(ATTN_IN_SIZES)
DN_CONV_CH = 2 * DN_QK_HEADS * DN_DK + DN_V_HEADS * DN_DV
DN_OUT = DN_V_HEADS * DN_DV
DN_GATES = 4 * DN_V_HEADS
DN_IN_SIZES = (DN_CONV_CH, DN_OUT, DN_GATES)
DN_IN = sum(DN_IN_SIZES)

kernel_name = 'hybrid_diffattn_swa_deltanet_convffn_prefix_dit'


def _split(x, sizes):
    return jnp.split(x, np.cumsum(sizes)[:-1].tolist(), axis=-1)


def rmsnorm(x, gain):
    xf = x.astype(jnp.float32)
    y = xf * lax.rsqrt(jnp.mean(xf * xf, axis=-1, keepdims=True) + EPS)
    return (y * gain.astype(jnp.float32)).astype(x.dtype)


def l2norm(x):
    xf = x.astype(jnp.float32)
    return (xf * lax.rsqrt(jnp.sum(xf * xf, axis=-1, keepdims=True) + EPS)).astype(x.dtype)


def modulate(x, shift, scale):
    return x * (1.0 + scale[:, None, :]) + shift[:, None, :]


def dwconv_centred(x, w):
    k = w.shape[0]
    r = k // 2
    n = x.shape[1]
    xp = jnp.pad(x, ((0, 0), (r, r), (0, 0)))
    return sum(xp[:, i:i + n] * w[i] for i in range(k))


def axial_rope_tables(n):
    rows = n // GRID_W
    row = jnp.broadcast_to(jnp.arange(rows)[:, None], (rows, GRID_W)).reshape(-1).astype(jnp.float32)
    col = jnp.broadcast_to(jnp.arange(GRID_W)[None, :], (rows, GRID_W)).reshape(-1).astype(jnp.float32)
    inv = ROPE_BASE ** (-jnp.arange(ROPE_AXIS_FREQS, dtype=jnp.float32) / ROPE_AXIS_FREQS)
    ang = jnp.concatenate([row[:, None] * inv, col[:, None] * inv], axis=-1)
    return jnp.cos(ang), jnp.sin(ang)


def apply_axial_rope(x, cos, sin):
    shp = (1, x.shape[1]) + (1,) * (x.ndim - 3) + (2, ROPE_AXIS_FREQS)
    c = cos.reshape(shp).astype(x.dtype)
    s = sin.reshape(shp).astype(x.dtype)
    xs = x.reshape(x.shape[:-1] + (2, 2, ROPE_AXIS_FREQS))
    x1, x2 = xs[..., 0, :], xs[..., 1, :]
    out = jnp.stack([x1 * c - x2 * s, x2 * c + x1 * s], axis=-2)
    return out.reshape(x.shape)


def diff_attend(q, k, v, lam):
    s = jnp.einsum('bqhmd,bkhmd->bhmqk', q, k).astype(jnp.float32) * (DIFF_DH ** -0.5)
    p = jax.nn.softmax(s, axis=-1)
    w = (p[:, :, 0] - lam * p[:, :, 1]).astype(v.dtype)
    return jnp.einsum('bhqk,bkhe->bqhe', w, v)


def sink_softmax(s, sink):
    m = jnp.maximum(jnp.max(s, axis=-1, keepdims=True), sink)
    e = jnp.exp(s - m)
    return e / (jnp.sum(e, axis=-1, keepdims=True) + jnp.exp(sink - m))


def swa_block(q_blk, b_idx, k_pad, v_pad, k_ctx, v_ctx, sink5, n):
    start = b_idx * Q_BLOCK
    span = Q_BLOCK + 2 * WINDOW
    kw = lax.dynamic_slice_in_dim(k_pad, start, span, axis=1)
    vw = lax.dynamic_slice_in_dim(v_pad, start, span, axis=1)
    scale = SWA_DH ** -0.5
    s_ctx = jnp.einsum('bqkgd,bjkd->bkgqj', q_blk, k_ctx).astype(jnp.float32) * scale
    s_win = jnp.einsum('bqkgd,bjkd->bkgqj', q_blk, kw).astype(jnp.float32) * scale
    qpos = start + jnp.arange(Q_BLOCK)
    kpos = start - WINDOW + jnp.arange(span)
    valid = (jnp.abs(qpos[:, None] - kpos[None, :]) <= WINDOW) & (kpos >= 0)[None, :] & (kpos < n)[None, :]
    s_win = jnp.where(valid, s_win, NEG_INF)
    p = sink_softmax(jnp.concatenate([s_ctx, s_win], axis=-1), sink5).astype(v_ctx.dtype)
    n_ctx = k_ctx.shape[1]
    return (jnp.einsum('bkgqj,bjkd->bqkgd', p[..., :n_ctx], v_ctx)
            + jnp.einsum('bkgqj,bjkd->bqkgd', p[..., n_ctx:], vw))


def swa_ctx(q, k, v, sink5):
    s = jnp.einsum('bqkgd,bjkd->bkgqj', q, k).astype(jnp.float32) * (SWA_DH ** -0.5)
    p = sink_softmax(s, sink5).astype(v.dtype)
    return jnp.einsum('bkgqj,bjkd->bqkgd', p, v)


def attention_mixer(h_lat, h_ctx, w_in, lam_vec, subln_g, sink, w_out, lam_init, cos, sin, need_ctx_out):
    bsz, n, _ = h_lat.shape
    n_ctx = h_ctx.shape[1]
    nb = n // Q_BLOCK

    def project(h):
        bb, nn, _ = h.shape
        qa, ka, va, qb, kb, vb = _split(h @ w_in, ATTN_IN_SIZES)
        return (qa.reshape(bb, nn, DIFF_HEADS, 2, DIFF_DH),
                ka.reshape(bb, nn, DIFF_HEADS, 2, DIFF_DH),
                va.reshape(bb, nn, DIFF_HEADS, 2 * DIFF_DH),
                qb.reshape(bb, nn, SWA_KV_HEADS, SWA_GROUP, SWA_DH),
                kb.reshape(bb, nn, SWA_KV_HEADS, SWA_DH),
                vb.reshape(bb, nn, SWA_KV_HEADS, SWA_DH))

    qa_l, ka_l, va_l, qb_l, kb_l, vb_l = project(h_lat)
    qa_l, ka_l, qb_l, kb_l = (apply_axial_rope(t, cos, sin) for t in (qa_l, ka_l, qb_l, kb_l))
    qa_c, ka_c, va_c, qb_c, kb_c, vb_c = project(h_ctx)

    lf = lam_vec.astype(jnp.float32)
    lam = jnp.exp(jnp.sum(lf[0] * lf[1])) - jnp.exp(jnp.sum(lf[2] * lf[3])) + lam_init

    k_all = jnp.concatenate([ka_c, ka_l], axis=1)
    v_all = jnp.concatenate([va_c, va_l], axis=1)
    qa_blocks = jnp.moveaxis(qa_l.reshape(bsz, nb, Q_BLOCK, DIFF_HEADS, 2, DIFF_DH), 1, 0)
    oa_l = lax.map(lambda qq: diff_attend(qq, k_all, v_all, lam), qa_blocks)
    oa_l = jnp.moveaxis(oa_l, 0, 1).reshape(bsz, n, DIFF_HEADS, 2 * DIFF_DH)

    sink5 = sink.astype(jnp.float32).reshape(1, SWA_KV_HEADS, SWA_GROUP, 1, 1)
    pad = ((0, 0), (WINDOW, WINDOW), (0, 0), (0, 0))
    kb_pad = jnp.pad(kb_l, pad)
    vb_pad = jnp.pad(vb_l, pad)
    qb_blocks = jnp.moveaxis(qb_l.reshape(bsz, nb, Q_BLOCK, SWA_KV_HEADS, SWA_GROUP, SWA_DH), 1, 0)
    ob_l = lax.map(lambda a: swa_block(a[0], a[1], kb_pad, vb_pad, kb_c, vb_c, sink5, n),
                   (qb_blocks, jnp.arange(nb)))
    ob_l = jnp.moveaxis(ob_l, 0, 1).reshape(bsz, n, SWA_OUT)

    def merge(oa, ob):
        bb, nn = oa.shape[:2]
        oa = (rmsnorm(oa, subln_g) * (1.0 - lam_init)).reshape(bb, nn, DIFF_OUT)
        return jnp.concatenate([oa, ob], axis=-1) @ w_out

    y_lat = merge(oa_l, ob_l)
    y_ctx = None
    if need_ctx_out:
        oa_c = diff_attend(qa_c, ka_c, va_c, lam)
        ob_c = swa_ctx(qb_c, kb_c, vb_c, sink5).reshape(bsz, n_ctx, SWA_OUT)
        y_ctx = merge(oa_c, ob_c)
    return y_lat, y_ctx


def gated_delta_rule(q, k, v, beta, g, state):
    f32 = jnp.float32
    b, n, h, dk = q.shape
    c = DN_CHUNK
    nc = n // c

    def chunks(t):
        t = t.astype(f32).reshape((b, nc, c) + t.shape[2:])
        return jnp.moveaxis(t, (1, 2), (0, 3))

    qc, kc, vc, bc, gc = (chunks(t) for t in (q, k, v, beta, g))
    gam = jnp.cumsum(gc, axis=-1)
    tril = jnp.tril(jnp.ones((c, c), dtype=bool))
    strict = jnp.tril(jnp.ones((c, c), dtype=bool), -1)
    diff = gam[..., :, None] - gam[..., None, :]
    decay = jnp.where(tril, jnp.exp(jnp.where(tril, diff, 0.0)), 0.0)
    kk = jnp.einsum('...rd,...sd->...rs', kc, kc)
    a = jnp.where(strict, bc[..., :, None] * kk * decay, 0.0)
    rhs = jnp.concatenate([kc * (bc * jnp.exp(gam))[..., None], vc * bc[..., None]], axis=-1)
    sol = lax.linalg.triangular_solve(jnp.eye(c, dtype=f32) + a, rhs, left_side=True, lower=True,
                                      unit_diagonal=True)
    w, u = sol[..., :dk], sol[..., dk:]
    aqk = jnp.where(tril, jnp.einsum('...rd,...sd->...rs', qc, kc) * decay, 0.0)
    qd = qc * jnp.exp(gam)[..., None]
    last = gam[..., -1:]
    kd = kc * jnp.exp(last - gam)[..., None]
    el = jnp.exp(last)[..., None]

    def step(s, xs):
        w_, u_, qd_, kd_, aqk_, el_ = xs
        vnew = u_ - w_ @ s
        o = qd_ @ s + aqk_ @ vnew
        s = el_ * s + jnp.swapaxes(kd_, -1, -2) @ vnew
        return s, o

    state, o = lax.scan(step, state.astype(f32), (w, u, qd, kd, aqk, el))
    o = jnp.moveaxis(o, (0, 3), (1, 2)).reshape(b, n, h, v.shape[-1])
    return o, state


def deltanet_mixer(h_lat, h_ctx, w_in, conv_w, a_log, dt_bias, onorm_g, w_out, need_ctx_out):
    rep = DN_V_HEADS // DN_QK_HEADS

    def project(h):
        bb, nn, _ = h.shape
        qkv, z, gates = _split(h @ w_in, DN_IN_SIZES)
        qkv = jax.nn.silu(dwconv_centred(qkv, conv_w))
        q, k, v = _split(qkv, (DN_QK_HEADS * DN_DK, DN_QK_HEADS * DN_DK, DN_OUT))
        q = jnp.repeat(l2norm(q.reshape(bb, nn, DN_QK_HEADS, DN_DK)), rep, axis=2) * (DN_DK ** -0.5)
        k = jnp.repeat(l2norm(k.reshape(bb, nn, DN_QK_HEADS, DN_DK)), rep, axis=2)
        v = v.reshape(bb, nn, DN_V_HEADS, DN_DV)
        gates = gates.reshape(bb, nn, 2, 2, DN_V_HEADS).astype(jnp.float32)
        beta = jax.nn.sigmoid(gates[:, :, :, 0])
        g = -jnp.exp(a_log.astype(jnp.float32)) * jax.nn.softplus(gates[:, :, :, 1] + dt_bias.astype(jnp.float32))
        return q, k, v, z, beta, g

    ql, kl, vl, zl, bl, gl = project(h_lat)
    qc, kc, vc, zc, bcx, gcx = project(h_ctx)
    bsz = h_lat.shape[0]
    zero = jnp.zeros((bsz, DN_V_HEADS, DN_DK, DN_DV), jnp.float32)
    o_lat = 0.0
    o_ctx = 0.0
    for d, rev in enumerate((False, True)):
        f = (lambda t: jnp.flip(t, axis=1)) if rev else (lambda t: t)
        oc, s_ctx = gated_delta_rule(f(qc), f(kc), f(vc), f(bcx[:, :, d]), f(gcx[:, :, d]), zero)
        ol, _ = gated_delta_rule(f(ql), f(kl), f(vl), f(bl[:, :, d]), f(gl[:, :, d]), s_ctx)
        o_lat = o_lat + f(ol)
        o_ctx = o_ctx + f(oc)

    def out(o, z):
        bb, nn = z.shape[:2]
        o = rmsnorm(o.astype(z.dtype), onorm_g) * jax.nn.silu(z.reshape(bb, nn, DN_V_HEADS, DN_DV))
        return o.reshape(bb, nn, DN_OUT) @ w_out

    y_lat = out(o_lat, zl)
    y_ctx = out(o_ctx, zc) if need_ctx_out else None
    return y_lat, y_ctx


def conv_ffn(h, w_up, conv_w, w_down):
    u = dwconv_centred(h @ w_up, conv_w)
    gate, up = jnp.split(u, 2, axis=-1)
    return (jax.nn.silu(gate) * up) @ w_down


def setup_inputs(seed: int = 0) -> dict:
    key = jax.random.key(seed)
    ks = iter(jax.random.split(key, 32))
    nrm = lambda shape, s: jax.random.normal(next(ks), shape, jnp.float32) * s
    gain = lambda shape: 1.0 + nrm(shape, 0.02)
    d = D_MODEL
    dt = jnp.exp(jax.random.uniform(next(ks), (N_ODD, 2, DN_V_HEADS), jnp.float32,
                                    minval=math.log(1e-3), maxval=math.log(1e-1)))
    return {
        'x': nrm((BATCH, SEQ, d), 1.0),
        'c': nrm((BATCH, d), 1.0),
        'ctx': nrm((BATCH, CTX_LEN, d), 1.0),
        'c_ctx': nrm((d,), 1.0),
        'w_mod': nrm((DEPTH, d, 6 * d), 0.5 * d ** -0.5),
        'b_mod': nrm((DEPTH, 6 * d), 0.01),
        'norm1_g': gain((DEPTH, d)),
        'norm2_g': gain((DEPTH, d)),
        'attn_w_in': nrm((N_EVEN, d, ATTN_IN), d ** -0.5),
        'diff_lambda': nrm((N_EVEN, 4, DIFF_DH), 0.1),
        'diff_subln_g': gain((N_EVEN, 2 * DIFF_DH)),
        'swa_sink': nrm((N_EVEN, SWA_HEADS), 0.5),
        'attn_w_out': nrm((N_EVEN, ATTN_OUT, d), ATTN_OUT ** -0.5),
        'dn_w_in': nrm((N_ODD, d, DN_IN), d ** -0.5),
        'dn_conv_w': nrm((N_ODD, SHORT_CONV_W, DN_CONV_CH), SHORT_CONV_W ** -0.5),
        'dn_a_log': jnp.log(jax.random.uniform(next(ks), (N_ODD, 2, DN_V_HEADS), jnp.float32, minval=1.0, maxval=16.0)),
        'dn_dt_bias': dt + jnp.log(-jnp.expm1(-dt)),
        'dn_norm_g': gain((N_ODD, DN_DV)),
        'dn_w_out': nrm((N_ODD, DN_OUT, d), DN_OUT ** -0.5),
        'ffn_w_up': nrm((DEPTH, d, 2 * FFN_HIDDEN), d ** -0.5),
        'ffn_conv_w': nrm((DEPTH, FFN_CONV_W, 2 * FFN_HIDDEN), FFN_CONV_W ** -0.5),
        'ffn_w_down': nrm((DEPTH, FFN_HIDDEN, d), FFN_HIDDEN ** -0.5),
        'final_norm_g': gain((d,)),
    }


def reference(x, c, ctx, c_ctx, w_mod, b_mod, norm1_g, norm2_g, attn_w_in, diff_lambda, diff_subln_g,
              swa_sink, attn_w_out, dn_w_in, dn_conv_w, dn_a_log, dn_dt_bias, dn_norm_g, dn_w_out,
              ffn_w_up, ffn_conv_w, ffn_w_down, final_norm_g):
    n = x.shape[1]
    cos, sin = axial_rope_tables(n)
    x_lat, x_ctx = x, ctx
    for layer in range(DEPTH):
        need_ctx = layer < DEPTH - 1
        mod_l = jax.nn.silu(c) @ w_mod[layer] + b_mod[layer]
        mod_c = jax.nn.silu(c_ctx)[None, :] @ w_mod[layer] + b_mod[layer]
        sh1, sc1, g1, sh2, sc2, g2 = jnp.split(mod_l, 6, axis=-1)
        csh1, csc1, cg1, csh2, csc2, cg2 = jnp.split(mod_c, 6, axis=-1)
        h_lat = modulate(rmsnorm(x_lat, norm1_g[layer]), sh1, sc1)
        h_ctx = modulate(rmsnorm(x_ctx, norm1_g[layer]), csh1, csc1)
        if layer % 2 == 0:
            i = layer // 2
            lam_init = 0.8 - 0.6 * math.exp(-0.3 * layer)
            y_lat, y_ctx = attention_mixer(h_lat, h_ctx, attn_w_in[i], diff_lambda[i], diff_subln_g[i],
                                           swa_sink[i], attn_w_out[i], lam_init, cos, sin, need_ctx)
        else:
            i = layer // 2
            y_lat, y_ctx = deltanet_mixer(h_lat, h_ctx, dn_w_in[i], dn_conv_w[i], dn_a_log[i], dn_dt_bias[i],
                                          dn_norm_g[i], dn_w_out[i], need_ctx)
        x_lat = x_lat + g1[:, None, :] * y_lat
        h_lat = modulate(rmsnorm(x_lat, norm2_g[layer]), sh2, sc2)
        x_lat = x_lat + g2[:, None, :] * conv_ffn(h_lat, ffn_w_up[layer], ffn_conv_w[layer], ffn_w_down[layer])
        if need_ctx:
            x_ctx = x_ctx + cg1[:, None, :] * y_ctx
            h_ctx = modulate(rmsnorm(x_ctx, norm2_g[layer]), csh2, csc2)
            x_ctx = x_ctx + cg2[:, None, :] * conv_ffn(h_ctx, ffn_w_up[layer], ffn_conv_w[layer], ffn_w_down[layer])
    return rmsnorm(x_lat, final_norm_g)
```

```python
import functools
import math

import jax
import jax.numpy as jnp
from jax import lax
from jax.experimental import pallas as pl
from jax.experimental.pallas import tpu as pltpu

F32 = jnp.float32
BF16 = jnp.bfloat16

EPS = 1e-6
NEG_INF = -1e30
GRID_W = 64
ROPE_BASE = 10000.0
ROPE_AXIS_FREQS = 32
DIFF_HEADS = 4
DIFF_DH = 128
SWA_HEADS = 8
SWA_KV_HEADS = 2
SWA_GROUP = SWA_HEADS // SWA_KV_HEADS
SWA_DH = 128
WINDOW = 128
DN_QK_HEADS = 16
DN_V_HEADS = 32
DN_DK = 128
DN_DV = 128
DN_CHUNK = 64
LANES = 128
HALO = 16
VMEM_LIMIT = 56 * 1024 * 1024


def _cp(sem):
    return pltpu.CompilerParams(dimension_semantics=sem, vmem_limit_bytes=VMEM_LIMIT)


def _pick(n, candidates):
    for c in candidates:
        if n % c == 0:
            return c
    raise ValueError(f"no tile for {n} among {candidates}")


def _silu(x):
    return x * jax.nn.sigmoid(x)


def _rows(i, tm):
    return i * tm + lax.broadcasted_iota(jnp.int32, (tm, 1), 0)


def _mod_kernel(a_ref, w_ref, b_ref, o_ref):
    a = _silu(a_ref[...]).astype(BF16)
    w = w_ref[0].astype(BF16)
    o_ref[0] = jnp.dot(a, w, preferred_element_type=F32) + b_ref[0]


def mod_vectors(cc, w_mod, b_mod):
    depth, d, n = w_mod.shape
    tn = _pick(n, (1024, 512, 256, 128))
    return pl.pallas_call(
        _mod_kernel,
        grid=(depth, n // tn),
        in_specs=[pl.BlockSpec((8, d), lambda l, j: (0, 0)),
                  pl.BlockSpec((1, d, tn), lambda l, j: (l, 0, j)),
                  pl.BlockSpec((1, 1, tn), lambda l, j: (l, 0, j))],
        out_specs=pl.BlockSpec((1, 8, tn), lambda l, j: (l, 0, j)),
        out_shape=jax.ShapeDtypeStruct((depth, 8, n), F32),
        compiler_params=_cp(("arbitrary", "arbitrary")),
        name="mod_vectors",
    )(cc, w_mod, b_mod.reshape(depth, 1, n))


def _nmm_kernel(x_ref, g_ref, sh_ref, sc_ref, w_ref, *rest, tm, tn, nc, rope):
    if rope:
        cos_ref, sin_ref, cmask_ref, cscale_ref, o_ref, h_ref = rest
    else:
        o_ref, h_ref = rest
    i = pl.program_id(0)
    j = pl.program_id(1)

    @pl.when(j == 0)
    def _():
        sub = math.gcd(tm, 256)
        for r0 in range(0, tm, sub):
            x = x_ref[r0:r0 + sub, :]
            ms = jnp.mean(x * x, axis=-1, keepdims=True)
            y = x * lax.rsqrt(ms + EPS) * g_ref[...]
            is_ctx = (i * tm + r0 + lax.broadcasted_iota(jnp.int32, (sub, 1), 0)) < nc
            sc = jnp.where(is_ctx, sc_ref[1:2, :], sc_ref[0:1, :])
            sh = jnp.where(is_ctx, sh_ref[1:2, :], sh_ref[0:1, :])
            h_ref[r0:r0 + sub, :] = (y * (1.0 + sc) + sh).astype(BF16)

    acc = jnp.dot(h_ref[...], w_ref[...], preferred_element_type=F32)
    if not rope:
        o_ref[...] = acc.astype(o_ref.dtype)
        return
    cos = cos_ref[...]
    sin = sin_ref[...]
    lane = lax.broadcasted_iota(jnp.int32, (tm, LANES), 1)
    first = (lane % (2 * ROPE_AXIS_FREQS)) < ROPE_AXIS_FREQS
    for c0 in range(0, tn, LANES):
        a = acc[:, c0:c0 + LANES]
        partner = jnp.where(first, pltpu.roll(a, LANES - ROPE_AXIS_FREQS, 1), pltpu.roll(a, ROPE_AXIS_FREQS, 1))
        roped = a * cos + partner * sin
        a = jnp.where(cmask_ref[:, c0:c0 + LANES] > 0.5, roped, a) * cscale_ref[:, c0:c0 + LANES]
        o_ref[:, c0:c0 + LANES] = a.astype(o_ref.dtype)


def norm_mod_matmul(x, g, sh, sc, w, nc, rope_args=None, tn_candidates=(1024, 512, 256, 128), out_dtype=BF16):
    t, d = x.shape
    n = w.shape[1]
    tm = _pick(t, (640, 512, 320, 256, 128))
    tn = _pick(n, tn_candidates)
    rope = rope_args is not None
    in_specs = [pl.BlockSpec((tm, d), lambda i, j: (i, 0)),
                pl.BlockSpec((1, d), lambda i, j: (0, 0)),
                pl.BlockSpec((8, d), lambda i, j: (0, 0)),
                pl.BlockSpec((8, d), lambda i, j: (0, 0)),
                pl.BlockSpec((d, tn), lambda i, j: (0, j))]
    args = [x, g.reshape(1, d), sh, sc, w]
    if rope:
        cos, sin, cmask, cscale = rope_args
        in_specs += [pl.BlockSpec((tm, LANES), lambda i, j: (i, 0)),
                     pl.BlockSpec((tm, LANES), lambda i, j: (i, 0)),
                     pl.BlockSpec((1, tn), lambda i, j: (0, j)),
                     pl.BlockSpec((1, tn), lambda i, j: (0, j))]
        args += [cos, sin, cmask, cscale]
    return pl.pallas_call(
        functools.partial(_nmm_kernel, tm=tm, tn=tn, nc=nc, rope=rope),
        grid=(t // tm, n // tn),
        in_specs=in_specs,
        out_specs=pl.BlockSpec((tm, tn), lambda i, j: (i, j)),
        out_shape=jax.ShapeDtypeStruct((t, n), out_dtype),
        scratch_shapes=[pltpu.VMEM((tm, d), BF16)],
        compiler_params=_cp(("arbitrary", "arbitrary")),
        name="norm_mod_matmul",
    )(*args)


def _proj_res_kernel(*refs, prologue, n_lhs, nk, tm, nc):
    lhs = refs[:n_lhs]
    w_ref, x_ref, gate_ref, o_ref, acc_ref = refs[n_lhs:]
    i = pl.program_id(0)
    k = pl.program_id(1)
    a = prologue(i, k, *lhs)
    p = jnp.dot(a, w_ref[...], preferred_element_type=F32)

    @pl.when(k == 0)
    def _():
        acc_ref[...] = p

    @pl.when(k > 0)
    def _():
        acc_ref[...] += p

    @pl.when(k == nk - 1)
    def _():
        is_ctx = _rows(i, tm) < nc
        gate = jnp.where(is_ctx, gate_ref[1:2, :], gate_ref[0:1, :])
        o_ref[...] = x_ref[...] + gate * acc_ref[...]


def proj_residual(prologue, lhs_args, lhs_specs, w, x, gate, nc, tm, tk):
    t, d = x.shape
    kdim = w.shape[0]
    nk = kdim // tk
    in_specs = list(lhs_specs) + [pl.BlockSpec((tk, d), lambda i, k: (k, 0)),
                                  pl.BlockSpec((tm, d), lambda i, k: (i, 0)),
                                  pl.BlockSpec((8, d), lambda i, k: (0, 0))]
    return pl.pallas_call(
        functools.partial(_proj_res_kernel, prologue=prologue, n_lhs=len(lhs_args), nk=nk, tm=tm, nc=nc),
        grid=(t // tm, nk),
        in_specs=in_specs,
        out_specs=pl.BlockSpec((tm, d), lambda i, k: (i, 0)),
        out_shape=jax.ShapeDtypeStruct((t, d), F32),
        scratch_shapes=[pltpu.VMEM((tm, d), F32)],
        compiler_params=_cp(("arbitrary", "arbitrary")),
        name="proj_residual",
    )(*lhs_args, w, x, gate)


def _halo_specs(tm, tk, t, col_off):
    nh = t // HALO
    per = tm // HALO
    return [pl.BlockSpec((tm, tk), lambda i, k: (i, col_off + k)),
            pl.BlockSpec((HALO, tk), lambda i, k: (jnp.maximum(i * per - 1, 0), col_off + k)),
            pl.BlockSpec((HALO, tk), lambda i, k: (jnp.minimum((i + 1) * per, nh - 1), col_off + k))]


def _conv3(u_ref, prev_ref, next_ref, cw_ref, i, tm, nc, t):
    u = u_ref[...].astype(F32)
    prev_row = prev_ref[...].astype(F32)[HALO - 1:HALO, :]
    next_row = next_ref[...].astype(F32)[0:1, :]
    local = lax.broadcasted_iota(jnp.int32, (tm, 1), 0)
    grow = i * tm + local
    up = jnp.where(local == 0, prev_row, pltpu.roll(u, 1, 0))
    un = jnp.where(local == tm - 1, next_row, pltpu.roll(u, tm - 1, 0))
    up = jnp.where((grow == 0) | (grow == nc), 0.0, up)
    un = jnp.where((grow == nc - 1) | (grow == t - 1), 0.0, un)
    w = cw_ref[...]
    return up * w[0:1, :] + u * w[1:2, :] + un * w[2:3, :]


def _diff_attn_kernel(lam_ref, g_ref, q_ref, k_ref, v_ref, o_ref, m_ref, l_ref, acc_ref,
                      *, tq, tk, nc, nk, lam_init):
    i = pl.program_id(1)
    kk = pl.program_id(2)

    @pl.when(kk == 0)
    def _():
        m_ref[...] = jnp.full(m_ref.shape, NEG_INF, F32)
        l_ref[...] = jnp.zeros(l_ref.shape, F32)
        acc_ref[...] = jnp.zeros(acc_ref.shape, F32)

    def step(masked):
        v = v_ref[...]
        for c in range(2):
            q = q_ref[:, c * DIFF_DH:(c + 1) * DIFF_DH]
            k = k_ref[:, c * DIFF_DH:(c + 1) * DIFF_DH]
            s = lax.dot_general(q, k, (((1,), (1,)), ((), ())), preferred_element_type=F32)
            if masked:
                qrow = i * tq + lax.broadcasted_iota(jnp.int32, (tq, 1), 0)
                kcol = kk * tk + lax.broadcasted_iota(jnp.int32, (1, tk), 1)
                s = jnp.where((qrow >= nc) | (kcol < nc), s, NEG_INF)
            m_prev = m_ref[c]
            m_new = jnp.maximum(m_prev, jnp.max(s, axis=-1, keepdims=True))
            alpha = jnp.exp(m_prev - m_new)
            p = jnp.exp(s - m_new)
            l_ref[c] = alpha * l_ref[c] + jnp.sum(p, axis=-1, keepdims=True)
            acc_ref[c] = alpha * acc_ref[c] + jnp.dot(p.astype(BF16), v, preferred_element_type=F32)
            m_ref[c] = m_new

    has_ctx_rows = i * tq < nc

    @pl.when(has_ctx_rows)
    def _():
        step(True)

    @pl.when(jnp.logical_not(has_ctx_rows))
    def _():
        step(False)

    @pl.when(kk == nk - 1)
    def _():
        lf = lam_ref[...]
        lam = (jnp.exp(jnp.sum(lf[0:1] * lf[1:2], axis=-1, keepdims=True))
               - jnp.exp(jnp.sum(lf[2:3] * lf[3:4], axis=-1, keepdims=True)) + lam_init)
        o = acc_ref[0] / l_ref[0] - lam * (acc_ref[1] / l_ref[1])
        ms = jnp.mean(o * o, axis=-1, keepdims=True)
        o_ref[...] = (o * lax.rsqrt(ms + EPS) * g_ref[...] * (1.0 - lam_init)).astype(o_ref.dtype)


def diff_attention(qkv, lam_vec, subln_g, nc, lam_init, col_q, col_k, col_v):
    t = qkv.shape[0]
    tq = _pick(t, (1280, 640, 256, 128))
    tk = _pick(t, (640, 256, 128))
    nk = t // tk
    hw = 2 * DIFF_DH
    return pl.pallas_call(
        functools.partial(_diff_attn_kernel, tq=tq, tk=tk, nc=nc, nk=nk, lam_init=lam_init),
        grid=(DIFF_HEADS, t // tq, nk),
        in_specs=[pl.BlockSpec((4, DIFF_DH), lambda h, i, k: (0, 0)),
                  pl.BlockSpec((1, hw), lambda h, i, k: (0, 0)),
                  pl.BlockSpec((tq, hw), lambda h, i, k: (i, col_q // hw + h)),
                  pl.BlockSpec((tk, hw), lambda h, i, k: (k, col_k // hw + h)),
                  pl.BlockSpec((tk, hw), lambda h, i, k: (k, col_v // hw + h))],
        out_specs=pl.BlockSpec((tq, hw), lambda h, i, k: (i, h)),
        out_shape=jax.ShapeDtypeStruct((t, DIFF_HEADS * hw), BF16),
        scratch_shapes=[pltpu.VMEM((2, tq, 1), F32), pltpu.VMEM((2, tq, 1), F32), pltpu.VMEM((2, tq, hw), F32)],
        compiler_params=_cp(("arbitrary", "arbitrary", "arbitrary")),
        name="diff_attention",
    )(lam_vec, subln_g.reshape(1, hw), qkv, qkv, qkv)


def _swa_kernel(sink_ref, q_ref, kc_ref, k0_ref, k1_ref, k2_ref, vc_ref, v0_ref, v1_ref, v2_ref, o_ref,
                *, nc, t):
    r = pl.program_id(1)
    qb = WINDOW
    q = jnp.concatenate([q_ref[:, g * SWA_DH:(g + 1) * SWA_DH] for g in range(SWA_GROUP)], axis=0)
    sk = sink_ref[0]
    sink = jnp.concatenate([jnp.broadcast_to(sk[g:g + 1, 0:1], (qb, 1)) for g in range(SWA_GROUP)], axis=0)
    nt = (((1,), (1,)), ((), ()))
    local_q = lax.broadcasted_iota(jnp.int32, (qb, 1), 0)
    qrow = jnp.concatenate([r * qb + local_q] * SWA_GROUP, axis=0)
    s_list = [lax.dot_general(q, kc_ref[...], nt, preferred_element_type=F32)]
    for w, k_ref in zip((-1, 0, 1), (k0_ref, k1_ref, k2_ref)):
        s = lax.dot_general(q, k_ref[...], nt, preferred_element_type=F32)
        krow = (r + w) * qb + lax.broadcasted_iota(jnp.int32, (1, qb), 1)
        valid = (jnp.abs(qrow - krow) <= WINDOW) & (krow >= nc) & (krow < t) & (qrow >= nc)
        s_list.append(jnp.where(valid, s, NEG_INF))
    m = sink
    for s in s_list:
        m = jnp.maximum(m, jnp.max(s, axis=-1, keepdims=True))
    denom = jnp.exp(sink - m)
    out = jnp.zeros((SWA_GROUP * qb, SWA_DH), F32)
    for s, v_ref in zip(s_list, (vc_ref, v0_ref, v1_ref, v2_ref)):
        e = jnp.exp(s - m)
        denom = denom + jnp.sum(e, axis=-1, keepdims=True)
        out = out + jnp.dot(e.astype(BF16), v_ref[...], preferred_element_type=F32)
    out = out / denom
    for g in range(SWA_GROUP):
        o_ref[:, g * SWA_DH:(g + 1) * SWA_DH] = out[g * qb:(g + 1) * qb, :].astype(o_ref.dtype)


def swa_attention(qkv, sink, nc, col_q, col_k, col_v):
    t = qkv.shape[0]
    qb = WINDOW
    nb = t // qb
    gw = SWA_GROUP * SWA_DH
    sink_b = jnp.broadcast_to(sink.astype(F32).reshape(SWA_KV_HEADS, SWA_GROUP, 1), (SWA_KV_HEADS, SWA_GROUP, LANES))
    kcb, vcb = col_k // SWA_DH, col_v // SWA_DH

    def win(cb, w):
        return pl.BlockSpec((qb, SWA_DH), lambda kv, r: (jnp.clip(r + w, 0, nb - 1), cb + kv))

    return pl.pallas_call(
        functools.partial(_swa_kernel, nc=nc, t=t),
        grid=(SWA_KV_HEADS, nb),
        in_specs=[pl.BlockSpec((1, SWA_GROUP, LANES), lambda kv, r: (kv, 0, 0)),
                  pl.BlockSpec((qb, gw), lambda kv, r: (r, col_q // gw + kv)),
                  pl.BlockSpec((nc, SWA_DH), lambda kv, r: (0, kcb + kv)),
                  win(kcb, -1), win(kcb, 0), win(kcb, 1),
                  pl.BlockSpec((nc, SWA_DH), lambda kv, r: (0, vcb + kv)),
                  win(vcb, -1), win(vcb, 0), win(vcb, 1)],
        out_specs=pl.BlockSpec((qb, gw), lambda kv, r: (r, kv)),
        out_shape=jax.ShapeDtypeStruct((t, SWA_HEADS * SWA_DH), BF16),
        compiler_params=_cp(("arbitrary", "arbitrary")),
        name="swa_attention",
    )(sink_b, qkv, qkv, qkv, qkv, qkv, qkv, qkv, qkv, qkv)


def _dn_conv_kernel(u_ref, prev_ref, next_ref, cw_ref, o_ref, *, tm, tc, nc, t, n_q, n_qk):
    i = pl.program_id(0)
    j = pl.program_id(1)
    y = _silu(_conv3(u_ref, prev_ref, next_ref, cw_ref, i, tm, nc, t))

    @pl.when(j < n_qk)
    def _():
        scale = jnp.where(j < n_q, DN_DK ** -0.5, 1.0).astype(F32)
        for c0 in range(0, tc, DN_DK):
            a = y[:, c0:c0 + DN_DK]
            ss = jnp.sum(a * a, axis=-1, keepdims=True)
            o_ref[:, c0:c0 + DN_DK] = (a * lax.rsqrt(ss + EPS) * scale).astype(o_ref.dtype)

    @pl.when(j >= n_qk)
    def _():
        o_ref[...] = y.astype(o_ref.dtype)


def dn_conv(u, conv_w, nc, n_cols):
    t = u.shape[0]
    tm = _pick(t, (640, 320, 256, 128))
    tc = 512
    qk_cols = DN_QK_HEADS * DN_DK
    return pl.pallas_call(
        functools.partial(_dn_conv_kernel, tm=tm, tc=tc, nc=nc, t=t, n_q=qk_cols // tc, n_qk=2 * qk_cols // tc),
        grid=(t // tm, n_cols // tc),
        in_specs=_halo_specs(tm, tc, t, 0) + [pl.BlockSpec((3, tc), lambda i, k: (0, k))],
        out_specs=pl.BlockSpec((tm, tc), lambda i, k: (i, k)),
        out_shape=jax.ShapeDtypeStruct((t, n_cols), BF16),
        compiler_params=_cp(("arbitrary", "arbitrary")),
        name="dn_conv",
    )(u, u, u, conv_w)


DN_GQ = 4
INV_DT = BF16
SOLVE_DT = BF16
STATE_DT = BF16
KK_DT = BF16
NEWTON_STEPS = 1


def _split3(x):
    hi = x.astype(BF16)
    r1 = x - hi.astype(F32)
    mid = r1.astype(BF16)
    lo = (r1 - mid.astype(F32)).astype(BF16)
    return hi, mid, lo


def _delta_kernel(gates_ref, nega_ref, bias_ref, q_ref, k_ref, v_ref, o_ref, s_ref):
    d = pl.program_id(0)
    hg = pl.program_id(1)
    s_idx = pl.program_id(2)
    c = DN_CHUNK
    nt = (((1,), (1,)), ((), ()))
    tn = (((0,), (0,)), ((), ()))

    @pl.when(s_idx == 0)
    def _():
        s_ref[...] = jnp.zeros(s_ref.shape, F32)

    gt = gates_ref[...]
    shift = (2 * LANES - d * (2 * DN_V_HEADS) - hg * (2 * DN_GQ)) % LANES
    gt = pltpu.roll(gt, shift, 1)
    nega = pltpu.roll(nega_ref[...], shift, 1)
    bias = pltpu.roll(bias_ref[...], shift, 1)
    beta_all = jax.nn.sigmoid(gt)
    z = gt + bias[0:1, :]
    softplus = jnp.maximum(z, 0.0) + jnp.log(1.0 + jnp.exp(-jnp.abs(z)))
    g_all = nega[0:1, :] * softplus

    ri = lax.broadcasted_iota(jnp.int32, (c, c), 0)
    ci = lax.broadcasted_iota(jnp.int32, (c, c), 1)
    order = (ri - ci) * jnp.where(d == 0, 1, -1)
    incl = order >= 0
    strict = order > 0
    eye = ri == ci
    eye_f = eye.astype(F32)
    csum = incl.astype(BF16)
    hi, mid, lo = _split3(g_all)
    gam_all = (jnp.dot(csum, hi, preferred_element_type=F32) + jnp.dot(csum, mid, preferred_element_type=F32)
               + jnp.dot(csum, lo, preferred_element_type=F32))
    tot_all = jnp.sum(g_all, axis=0, keepdims=True)

    for hq in range(DN_GQ):
        kb = k_ref[:, hq * DN_DK:(hq + 1) * DN_DK]
        qb = q_ref[:, hq * DN_DK:(hq + 1) * DN_DK]
        kf = kb.astype(F32)
        qf = qb.astype(F32)
        kk = lax.dot_general(kb.astype(KK_DT), kb.astype(KK_DT), nt, preferred_element_type=F32)
        qk = lax.dot_general(qb.astype(KK_DT), kb.astype(KK_DT), nt, preferred_element_type=F32)
        for e in range(2):
            hv = 2 * hq + e
            beta = beta_all[:, hv:hv + 1]
            gam = gam_all[:, DN_V_HEADS + hv:DN_V_HEADS + hv + 1]
            tot = tot_all[:, DN_V_HEADS + hv:DN_V_HEADS + hv + 1]
            gam_row = jnp.sum(jnp.where(eye, gam, 0.0), axis=0, keepdims=True)
            dec = jnp.where(incl, jnp.exp(jnp.where(incl, gam - gam_row, 0.0)), 0.0)
            a = jnp.where(strict, beta * kk * dec, 0.0)
            pw = -a
            tinv = eye_f + pw
            for _ in range(int(math.log2(c)) - 1):
                pwb = pw.astype(INV_DT)
                pw = jnp.dot(pwb, pwb, preferred_element_type=F32)
                tinv = tinv + jnp.dot(tinv.astype(INV_DT), pw.astype(INV_DT), preferred_element_type=F32)
            for _ in range(NEWTON_STEPS):
                t_b = tinv.astype(BF16)
                resid = eye_f - tinv - jnp.dot(a.astype(BF16), t_b, preferred_element_type=F32)
                tinv = tinv + jnp.dot(t_b, resid.astype(BF16), preferred_element_type=F32)
            tb = tinv.astype(SOLVE_DT)
            eg = jnp.exp(gam)
            vf = v_ref[:, hv * DN_DV:(hv + 1) * DN_DV].astype(F32)
            w = jnp.dot(tb, (kf * (beta * eg)).astype(SOLVE_DT), preferred_element_type=F32)
            u = jnp.dot(tb, (vf * beta).astype(SOLVE_DT), preferred_element_type=F32)
            aqk = jnp.where(incl, qk * dec, 0.0)
            qd = qf * eg
            kd = kf * jnp.exp(tot - gam)
            state = s_ref[hv]
            sb = state.astype(STATE_DT)
            vnew = u - jnp.dot(w.astype(STATE_DT), sb, preferred_element_type=F32)
            vnb = vnew.astype(STATE_DT)
            o = (jnp.dot(qd.astype(STATE_DT), sb, preferred_element_type=F32)
                 + jnp.dot(aqk.astype(STATE_DT), vnb, preferred_element_type=F32))
            s_ref[hv] = jnp.exp(tot) * state + lax.dot_general(kd.astype(STATE_DT), vnb, tn, preferred_element_type=F32)
            o_ref[0, :, hv * DN_DV:(hv + 1) * DN_DV] = o.astype(o_ref.dtype)


def gated_delta(qkvc, gates, nega, bias, nc):
    t = qkvc.shape[0]
    c = DN_CHUNK
    nch = t // c
    ncc = nc // c
    qw = DN_GQ * DN_DK
    vw = 2 * DN_GQ * DN_DV
    qk_cols = DN_QK_HEADS * DN_DK

    def chunk(d, s):
        rev = jnp.where(s < ncc, ncc - 1 - s, nch - 1 - (s - ncc))
        return jnp.where(d == 0, s, rev)

    return pl.pallas_call(
        _delta_kernel,
        grid=(2, DN_QK_HEADS // DN_GQ, nch),
        in_specs=[pl.BlockSpec((c, LANES), lambda d, h, s: (chunk(d, s), 0)),
                  pl.BlockSpec((8, LANES), lambda d, h, s: (0, 0)),
                  pl.BlockSpec((8, LANES), lambda d, h, s: (0, 0)),
                  pl.BlockSpec((c, qw), lambda d, h, s: (chunk(d, s), h)),
                  pl.BlockSpec((c, qw), lambda d, h, s: (chunk(d, s), qk_cols // qw + h)),
                  pl.BlockSpec((c, vw), lambda d, h, s: (chunk(d, s), 2 * qk_cols // vw + h))],
        out_specs=pl.BlockSpec((1, c, vw), lambda d, h, s: (d, chunk(d, s), h)),
        out_shape=jax.ShapeDtypeStruct((2, t, DN_V_HEADS * DN_DV), BF16),
        scratch_shapes=[pltpu.VMEM((2 * DN_GQ, DN_DK, DN_DV), F32)],
        compiler_params=_cp(("arbitrary", "arbitrary", "arbitrary")),
        name="gated_delta",
    )(gates, nega, bias, qkvc, qkvc, qkvc)


def _attn_out_prologue(i, k, oa_ref, ob_ref, *, n_a):
    return jnp.where(k < n_a, oa_ref[...], ob_ref[...])


def _dn_out_prologue(i, k, of_ref, or_ref, z_ref, g_ref, *, tk):
    o = of_ref[0].astype(F32) + or_ref[0].astype(F32)
    z = z_ref[...].astype(F32)
    parts = []
    for c0 in range(0, tk, DN_DV):
        a = o[:, c0:c0 + DN_DV]
        ms = jnp.mean(a * a, axis=-1, keepdims=True)
        parts.append(a * lax.rsqrt(ms + EPS) * g_ref[...])
    return (jnp.concatenate(parts, axis=-1) * _silu(z)).astype(BF16)


def _ffn_prologue(i, k, ug, pg, ng, uu, pu, nu, cwg, cwu, *, tm, nc, t):
    gate = _conv3(ug, pg, ng, cwg, i, tm, nc, t)
    up = _conv3(uu, pu, nu, cwu, i, tm, nc, t)
    return (_silu(gate) * up).astype(BF16)


def _final_norm_kernel(x_ref, g_ref, o_ref):
    x = x_ref[...]
    ms = jnp.mean(x * x, axis=-1, keepdims=True)
    o_ref[...] = x * lax.rsqrt(ms + EPS) * g_ref[...]


def final_norm(x, g, nc):
    t, d = x.shape
    tm = math.gcd(nc, 256)
    off = nc // tm
    return pl.pallas_call(
        _final_norm_kernel,
        grid=((t - nc) // tm,),
        in_specs=[pl.BlockSpec((tm, d), lambda i: (i + off, 0)), pl.BlockSpec((1, d), lambda i: (0, 0))],
        out_specs=pl.BlockSpec((tm, d), lambda i: (i, 0)),
        out_shape=jax.ShapeDtypeStruct((t - nc, d), F32),
        compiler_params=_cp(("arbitrary",)),
        name="final_norm",
    )(x, g.reshape(1, d))


def _rope_tables(n, nc):
    rows = n // GRID_W
    row = jnp.broadcast_to(jnp.arange(rows)[:, None], (rows, GRID_W)).reshape(-1).astype(F32)
    col = jnp.broadcast_to(jnp.arange(GRID_W)[None, :], (rows, GRID_W)).reshape(-1).astype(F32)
    inv = ROPE_BASE ** (-jnp.arange(ROPE_AXIS_FREQS, dtype=F32) / ROPE_AXIS_FREQS)
    ar, ac = row[:, None] * inv, col[:, None] * inv
    cos = jnp.concatenate([jnp.cos(ar), jnp.cos(ar), jnp.cos(ac), jnp.cos(ac)], axis=-1)
    sin = jnp.concatenate([-jnp.sin(ar), jnp.sin(ar), -jnp.sin(ac), jnp.sin(ac)], axis=-1)
    cos = jnp.concatenate([jnp.ones((nc, LANES), F32), cos], axis=0)
    sin = jnp.concatenate([jnp.zeros((nc, LANES), F32), sin], axis=0)
    return cos, sin


def kernel(x, c, ctx, c_ctx, w_mod, b_mod, norm1_g, norm2_g, attn_w_in, diff_lambda, diff_subln_g, swa_sink,
           attn_w_out, dn_w_in, dn_conv_w, dn_a_log, dn_dt_bias, dn_norm_g, dn_w_out, ffn_w_up, ffn_conv_w,
           ffn_w_down, final_norm_g):
    assert x.shape[0] == 1, "single-sequence kernel"
    n, d = x.shape[1], x.shape[2]
    nc = ctx.shape[1]
    t = nc + n
    depth = w_mod.shape[0]
    xs = jnp.concatenate([ctx[0], x[0]], axis=0)
    cc = jnp.zeros((8, d), F32).at[0].set(c[0]).at[1].set(c_ctx)
    mods = mod_vectors(cc, w_mod, b_mod)
    cos, sin = _rope_tables(n, nc)

    da, sb_, skv = DIFF_HEADS * 2 * DIFF_DH, SWA_HEADS * SWA_DH, SWA_KV_HEADS * SWA_DH
    col_qa, col_ka, col_qb = 0, da, 2 * da
    col_kb = col_qb + sb_
    col_va = col_kb + skv
    col_vb = col_va + da
    n_attn = col_vb + skv
    cols = jnp.arange(n_attn)
    cmask = (cols < col_va).astype(F32).reshape(1, n_attn)
    is_q = (cols < col_ka) | ((cols >= col_qb) & (cols < col_kb))
    cscale = jnp.where(is_q, DIFF_DH ** -0.5, 1.0).astype(F32).reshape(1, n_attn)

    tm_p = _pick(t, (640, 320, 256, 128))
    hidden = ffn_w_down.shape[1]
    tk_f = _pick(hidden, (512, 256, 128))
    qkv_cols = 2 * DN_QK_HEADS * DN_DK + DN_V_HEADS * DN_DV
    dn_out = DN_V_HEADS * DN_DV

    for layer in range(depth):
        m = mods[layer]
        sh1, sc1, g1, sh2, sc2, g2 = (m[:, j * d:(j + 1) * d] for j in range(6))
        i = layer // 2
        if layer % 2 == 0:
            lam_init = 0.8 - 0.6 * math.exp(-0.3 * layer)
            w = attn_w_in[i]
            w = jnp.concatenate([w[:, 0:2 * da], w[:, 3 * da:3 * da + sb_ + skv], w[:, 2 * da:3 * da],
                                 w[:, 3 * da + sb_ + skv:]], axis=1).astype(BF16)
            qkv = norm_mod_matmul(xs, norm1_g[layer], sh1, sc1, w, nc, rope_args=(cos, sin, cmask, cscale),
                                  tn_candidates=(512, 256, 128))
            oa = diff_attention(qkv, diff_lambda[i], diff_subln_g[i], nc, lam_init, col_qa, col_ka, col_va)
            ob = swa_attention(qkv, swa_sink[i], nc, col_qb, col_kb, col_vb)
            tk = 512
            n_a = da // tk
            xs = proj_residual(
                functools.partial(_attn_out_prologue, n_a=n_a), [oa, ob],
                [pl.BlockSpec((tm_p, tk), lambda r, k: (r, jnp.minimum(k, n_a - 1))),
                 pl.BlockSpec((tm_p, tk), lambda r, k: (r, jnp.maximum(k - n_a, 0)))],
                attn_w_out[i].astype(BF16), xs, g1, nc, tm_p, tk)
        else:
            w = dn_w_in[i]
            u = norm_mod_matmul(xs, norm1_g[layer], sh1, sc1, w[:, :qkv_cols + dn_out].astype(BF16), nc)
            gates = norm_mod_matmul(xs, norm1_g[layer], sh1, sc1, w[:, qkv_cols + dn_out:].astype(BF16), nc, out_dtype=F32)
            qkvc = dn_conv(u, dn_conv_w[i], nc, qkv_cols)
            zero = jnp.zeros((2, DN_V_HEADS), F32)
            nega = jnp.stack([zero, -jnp.exp(dn_a_log[i].astype(F32))], axis=1).reshape(1, 4 * DN_V_HEADS)
            bias = jnp.stack([zero, dn_dt_bias[i].astype(F32)], axis=1).reshape(1, 4 * DN_V_HEADS)
            o2 = gated_delta(qkvc, gates, jnp.broadcast_to(nega, (8, LANES)),
                             jnp.broadcast_to(bias, (8, LANES)), nc)
            tk = 512
            zoff = qkv_cols // tk
            xs = proj_residual(
                functools.partial(_dn_out_prologue, tk=tk), [o2, o2, u, dn_norm_g[i].reshape(1, DN_DV)],
                [pl.BlockSpec((1, tm_p, tk), lambda r, k: (0, r, k)),
                 pl.BlockSpec((1, tm_p, tk), lambda r, k: (1, r, k)),
                 pl.BlockSpec((tm_p, tk), lambda r, k: (r, zoff + k)),
                 pl.BlockSpec((1, DN_DV), lambda r, k: (0, 0))],
                dn_w_out[i].astype(BF16), xs, g1, nc, tm_p, tk)
        uf = norm_mod_matmul(xs, norm2_g[layer], sh2, sc2, ffn_w_up[layer].astype(BF16), nc)
        cw = ffn_conv_w[layer]
        xs = proj_residual(
            functools.partial(_ffn_prologue, tm=tm_p, nc=nc, t=t), [uf, uf, uf, uf, uf, uf, cw, cw],
            _halo_specs(tm_p, tk_f, t, 0) + _halo_specs(tm_p, tk_f, t, hidden // tk_f)
            + [pl.BlockSpec((3, tk_f), lambda r, k: (0, k)),
               pl.BlockSpec((3, tk_f), lambda r, k: (0, hidden // tk_f + k))],
            ffn_w_down[layer].astype(BF16), xs, g2, nc, tm_p, tk_f)
    return final_norm(xs, final_norm_g, nc)[None]
```

```python
import functools
import math

import jax
import jax.numpy as jnp
from jax import lax
from jax.experimental import pallas as pl
from jax.experimental.pallas import tpu as pltpu

F32 = jnp.float32
BF16 = jnp.bfloat16

EPS = 1e-6
NEG_INF = -1e30
GRID_W = 64
ROPE_BASE = 10000.0
ROPE_AXIS_FREQS = 32
DIFF_HEADS = 4
DIFF_DH = 128
SWA_HEADS = 8
SWA_KV_HEADS = 2
SWA_GROUP = SWA_HEADS // SWA_KV_HEADS
SWA_DH = 128
WINDOW = 128
DN_QK_HEADS = 16
DN_V_HEADS = 32
DN_DK = 128
DN_DV = 128
DN_CHUNK = 64
LANES = 128
HALO = 16
VMEM_LIMIT = 56 * 1024 * 1024


def _cp(sem):
    return pltpu.CompilerParams(dimension_semantics=sem, vmem_limit_bytes=VMEM_LIMIT)


def _pick(n, candidates):
    for c in candidates:
        if n % c == 0:
            return c
    raise ValueError(f"no tile for {n} among {candidates}")


def _silu(x):
    return x * jax.nn.sigmoid(x)


def _rows(i, tm):
    return i * tm + lax.broadcasted_iota(jnp.int32, (tm, 1), 0)


def _mod_kernel(a_ref, w_ref, b_ref, o_ref):
    a = _silu(a_ref[...]).astype(BF16)
    w = w_ref[0].astype(BF16)
    o_ref[0] = jnp.dot(a, w, preferred_element_type=F32) + b_ref[0]


def mod_vectors(cc, w_mod, b_mod):
    depth, d, n = w_mod.shape
    tn = _pick(n, (1024, 512, 256, 128))
    return pl.pallas_call(
        _mod_kernel,
        grid=(depth, n // tn),
        in_specs=[pl.BlockSpec((8, d), lambda l, j: (0, 0)),
                  pl.BlockSpec((1, d, tn), lambda l, j: (l, 0, j)),
                  pl.BlockSpec((1, 1, tn), lambda l, j: (l, 0, j))],
        out_specs=pl.BlockSpec((1, 8, tn), lambda l, j: (l, 0, j)),
        out_shape=jax.ShapeDtypeStruct((depth, 8, n), F32),
        compiler_params=_cp(("arbitrary", "arbitrary")),
        name="mod_vectors",
    )(cc, w_mod, b_mod.reshape(depth, 1, n))


def _nmm_kernel(x_ref, g_ref, sh_ref, sc_ref, w_ref, *rest, tm, tn, nc, rope):
    if rope:
        cos_ref, sin_ref, cmask_ref, cscale_ref, o_ref, h_ref = rest
    else:
        o_ref, h_ref = rest
    i = pl.program_id(0)
    j = pl.program_id(1)

    @pl.when(j == 0)
    def _():
        sub = math.gcd(tm, 256)
        for r0 in range(0, tm, sub):
            x = x_ref[r0:r0 + sub, :]
            ms = jnp.mean(x * x, axis=-1, keepdims=True)
            y = x * lax.rsqrt(ms + EPS) * g_ref[...]
            is_ctx = (i * tm + r0 + lax.broadcasted_iota(jnp.int32, (sub, 1), 0)) < nc
            sc = jnp.where(is_ctx, sc_ref[1:2, :], sc_ref[0:1, :])
            sh = jnp.where(is_ctx, sh_ref[1:2, :], sh_ref[0:1, :])
            h_ref[r0:r0 + sub, :] = (y * (1.0 + sc) + sh).astype(BF16)

    acc = jnp.dot(h_ref[...], w_ref[...], preferred_element_type=F32)
    if not rope:
        o_ref[...] = acc.astype(o_ref.dtype)
        return
    cos = cos_ref[...]
    sin = sin_ref[...]
    lane = lax.broadcasted_iota(jnp.int32, (tm, LANES), 1)
    first = (lane % (2 * ROPE_AXIS_FREQS)) < ROPE_AXIS_FREQS
    for c0 in range(0, tn, LANES):
        a = acc[:, c0:c0 + LANES]
        partner = jnp.where(first, pltpu.roll(a, LANES - ROPE_AXIS_FREQS, 1), pltpu.roll(a, ROPE_AXIS_FREQS, 1))
        roped = a * cos + partner * sin
        a = jnp.where(cmask_ref[:, c0:c0 + LANES] > 0.5, roped, a) * cscale_ref[:, c0:c0 + LANES]
        o_ref[:, c0:c0 + LANES] = a.astype(o_ref.dtype)


def norm_mod_matmul(x, g, sh, sc, w, nc, rope_args=None, tn_candidates=(1024, 512, 256, 128), out_dtype=BF16):
    t, d = x.shape
    n = w.shape[1]
    tm = _pick(t, (640, 512, 320, 256, 128))
    tn = _pick(n, tn_candidates)
    rope = rope_args is not None
    in_specs = [pl.BlockSpec((tm, d), lambda i, j: (i, 0)),
                pl.BlockSpec((1, d), lambda i, j: (0, 0)),
                pl.BlockSpec((8, d), lambda i, j: (0, 0)),
                pl.BlockSpec((8, d), lambda i, j: (0, 0)),
                pl.BlockSpec((d, tn), lambda i, j: (0, j))]
    args = [x, g.reshape(1, d), sh, sc, w]
    if rope:
        cos, sin, cmask, cscale = rope_args
        in_specs += [pl.BlockSpec((tm, LANES), lambda i, j: (i, 0)),
                     pl.BlockSpec((tm, LANES), lambda i, j: (i, 0)),
                     pl.BlockSpec((1, tn), lambda i, j: (0, j)),
                     pl.BlockSpec((1, tn), lambda i, j: (0, j))]
        args += [cos, sin, cmask, cscale]
    return pl.pallas_call(
        functools.partial(_nmm_kernel, tm=tm, tn=tn, nc=nc, rope=rope),
        grid=(t // tm, n // tn),
        in_specs=in_specs,
        out_specs=pl.BlockSpec((tm, tn), lambda i, j: (i, j)),
        out_shape=jax.ShapeDtypeStruct((t, n), out_dtype),
        scratch_shapes=[pltpu.VMEM((tm, d), BF16)],
        compiler_params=_cp(("arbitrary", "arbitrary")),
        name="norm_mod_matmul",
    )(*args)


def _proj_res_kernel(*refs, prologue, n_lhs, nk, tm, nc):
    lhs = refs[:n_lhs]
    w_ref, x_ref, gate_ref, o_ref, acc_ref = refs[n_lhs:]
    i = pl.program_id(0)
    k = pl.program_id(1)
    a = prologue(i, k, *lhs)
    p = jnp.dot(a, w_ref[...], preferred_element_type=F32)

    @pl.when(k == 0)
    def _():
        acc_ref[...] = p

    @pl.when(k > 0)
    def _():
        acc_ref[...] += p

    @pl.when(k == nk - 1)
    def _():
        is_ctx = _rows(i, tm) < nc
        gate = jnp.where(is_ctx, gate_ref[1:2, :], gate_ref[0:1, :])
        o_ref[...] = x_ref[...] + gate * acc_ref[...]


def proj_residual(prologue, lhs_args, lhs_specs, w, x, gate, nc, tm, tk):
    t, d = x.shape
    kdim = w.shape[0]
    nk = kdim // tk
    in_specs = list(lhs_specs) + [pl.BlockSpec((tk, d), lambda i, k: (k, 0)),
                                  pl.BlockSpec((tm, d), lambda i, k: (i, 0)),
                                  pl.BlockSpec((8, d), lambda i, k: (0, 0))]
    return pl.pallas_call(
        functools.partial(_proj_res_kernel, prologue=prologue, n_lhs=len(lhs_args), nk=nk, tm=tm, nc=nc),
        grid=(t // tm, nk),
        in_specs=in_specs,
        out_specs=pl.BlockSpec((tm, d), lambda i, k: (i, 0)),
        out_shape=jax.ShapeDtypeStruct((t, d), F32),
        scratch_shapes=[pltpu.VMEM((tm, d), F32)],
        compiler_params=_cp(("arbitrary", "arbitrary")),
        name="proj_residual",
    )(*lhs_args, w, x, gate)


def _halo_specs(tm, tk, t, col_off):
    nh = t // HALO
    per = tm // HALO
    return [pl.BlockSpec((tm, tk), lambda i, k: (i, col_off + k)),
            pl.BlockSpec((HALO, tk), lambda i, k: (jnp.maximum(i * per - 1, 0), col_off + k)),
            pl.BlockSpec((HALO, tk), lambda i, k: (jnp.minimum((i + 1) * per, nh - 1), col_off + k))]


CONV_STRIP = 32
SUBLANES = 8


def _patch_rows(x, g0, patch):
    parts = [x[:g0], patch(x[g0:g0 + SUBLANES]), x[g0 + SUBLANES:]]
    return jnp.concatenate([p for p in parts if p.shape[0]], axis=0)


def _conv3_strips(u_ref, prev_ref, next_ref, cw_ref, i, tm, nc, t):
    s = CONV_STRIP
    sub = lax.broadcasted_iota(jnp.int32, (SUBLANES, 1), 0)
    w = cw_ref[...]
    w0, w1, w2 = w[0:1, :], w[1:2, :], w[2:3, :]
    halo_prev = jnp.where(i == 0, 0.0, prev_ref[...].astype(F32)[HALO - 1:HALO, :])
    halo_next = jnp.where(i == t // tm - 1, 0.0, next_ref[...].astype(F32)[0:1, :])
    n = tm // s
    strips = [u_ref[j * s:(j + 1) * s, :].astype(F32) for j in range(n)]
    first_blk, first_row = nc // tm, nc % tm
    last_blk, last_row = (nc - 1) // tm, (nc - 1) % tm
    for j in range(n):
        u = strips[j]
        before = halo_prev if j == 0 else strips[j - 1][s - 1:s, :]
        after = halo_next if j == n - 1 else strips[j + 1][0:1, :]
        up = _patch_rows(pltpu.roll(u, 1, 0), 0, lambda g: jnp.where(sub == 0, before, g))
        un = _patch_rows(pltpu.roll(u, s - 1, 0), s - SUBLANES, lambda g: jnp.where(sub == SUBLANES - 1, after, g))
        if 0 < nc < t and first_row // s == j:
            r = first_row % s
            up = _patch_rows(up, r // SUBLANES * SUBLANES,
                             lambda g: jnp.where((sub == r % SUBLANES) & (i == first_blk), 0.0, g))
        if 0 < nc < t and last_row // s == j:
            r = last_row % s
            un = _patch_rows(un, r // SUBLANES * SUBLANES,
                             lambda g: jnp.where((sub == r % SUBLANES) & (i == last_blk), 0.0, g))
        yield up * w0 + u * w1 + un * w2


ATTN_STRIP = 16
LOG2E = math.log2(math.e)


def _diff_attn_kernel(lam_ref, g_ref, q_ref, k_ref, v_ref, o_ref, m_ref, l_ref, acc_ref, p_ref, alpha_ref,
                      *, tq, tk, nc, nk, lam_init):
    i = pl.program_id(1)
    kk = pl.program_id(2)

    @pl.when(kk == 0)
    def _():
        m_ref[...] = jnp.full(m_ref.shape, NEG_INF, F32)
        l_ref[...] = jnp.zeros(l_ref.shape, F32)
        acc_ref[...] = jnp.zeros(acc_ref.shape, F32)

    def step(masked):
        v = v_ref[...]
        nt = (((1,), (1,)), ((), ()))
        for c in range(2):
            s_all = lax.dot_general(q_ref[:, c * DIFF_DH:(c + 1) * DIFF_DH], k_ref[:, c * DIFF_DH:(c + 1) * DIFF_DH], nt,
                                    preferred_element_type=F32)
            for r0 in range(0, tq, ATTN_STRIP):
                rows = slice(r0, r0 + ATTN_STRIP)
                s = s_all[rows, :]
                if masked:
                    qrow = i * tq + r0 + lax.broadcasted_iota(jnp.int32, (ATTN_STRIP, 1), 0)
                    kcol = kk * tk + lax.broadcasted_iota(jnp.int32, (1, tk), 1)
                    s = jnp.where((qrow >= nc) | (kcol < nc), s, NEG_INF)
                m_prev = m_ref[c, rows, :]
                m_new = jnp.maximum(m_prev, jnp.max(s, axis=-1, keepdims=True))
                alpha = jnp.exp2(m_prev - m_new)
                p = jnp.exp2(s - m_new)
                l_ref[c, rows, :] = alpha * l_ref[c, rows, :] + jnp.sum(p, axis=-1, keepdims=True)
                m_ref[c, rows, :] = m_new
                alpha_ref[c, rows, :] = alpha
                p_ref[c, rows, :] = p.astype(BF16)
            acc_ref[c] = alpha_ref[c] * acc_ref[c] + jnp.dot(p_ref[c], v, preferred_element_type=F32)

    has_ctx_rows = i * tq < nc

    @pl.when(has_ctx_rows)
    def _():
        step(True)

    @pl.when(jnp.logical_not(has_ctx_rows))
    def _():
        step(False)

    @pl.when(kk == nk - 1)
    def _():
        lf = lam_ref[...]
        lam = (jnp.exp(jnp.sum(lf[0:1] * lf[1:2], axis=-1, keepdims=True))
               - jnp.exp(jnp.sum(lf[2:3] * lf[3:4], axis=-1, keepdims=True)) + lam_init)
        o = acc_ref[0] / l_ref[0] - lam * (acc_ref[1] / l_ref[1])
        ms = jnp.mean(o * o, axis=-1, keepdims=True)
        o_ref[...] = (o * lax.rsqrt(ms + EPS) * g_ref[...] * (1.0 - lam_init)).astype(o_ref.dtype)


def diff_attention(qkv, lam_vec, subln_g, nc, lam_init, col_q, col_k, col_v):
    t = qkv.shape[0]
    tq = _pick(t, (640, 256, 128))
    tk = _pick(t, (1280, 640, 256, 128))
    nk = t // tk
    hw = 2 * DIFF_DH
    return pl.pallas_call(
        functools.partial(_diff_attn_kernel, tq=tq, tk=tk, nc=nc, nk=nk, lam_init=lam_init),
        grid=(DIFF_HEADS, t // tq, nk),
        in_specs=[pl.BlockSpec((4, DIFF_DH), lambda h, i, k: (0, 0)),
                  pl.BlockSpec((1, hw), lambda h, i, k: (0, 0)),
                  pl.BlockSpec((tq, hw), lambda h, i, k: (i, col_q // hw + h)),
                  pl.BlockSpec((tk, hw), lambda h, i, k: (k, col_k // hw + h)),
                  pl.BlockSpec((tk, hw), lambda h, i, k: (k, col_v // hw + h))],
        out_specs=pl.BlockSpec((tq, hw), lambda h, i, k: (i, h)),
        out_shape=jax.ShapeDtypeStruct((t, DIFF_HEADS * hw), BF16),
        scratch_shapes=[pltpu.VMEM((2, tq, 1), F32), pltpu.VMEM((2, tq, 1), F32), pltpu.VMEM((2, tq, hw), F32),
                        pltpu.VMEM((2, tq, tk), BF16), pltpu.VMEM((2, tq, 1), F32)],
        compiler_params=_cp(("arbitrary", "arbitrary", "arbitrary")),
        name="diff_attention",
    )(lam_vec, subln_g.reshape(1, hw), qkv, qkv, qkv)


def _swa_kernel(sink_ref, q_ref, kc_ref, k0_ref, k1_ref, k2_ref, vc_ref, v0_ref, v1_ref, v2_ref, o_ref,
                *, nc, t):
    r = pl.program_id(1)
    qb = WINDOW
    q = jnp.concatenate([q_ref[:, g * SWA_DH:(g + 1) * SWA_DH] for g in range(SWA_GROUP)], axis=0)
    sk = sink_ref[0]
    sink = jnp.concatenate([jnp.broadcast_to(sk[g:g + 1, 0:1], (qb, 1)) for g in range(SWA_GROUP)], axis=0)
    nt = (((1,), (1,)), ((), ()))
    local_q = lax.broadcasted_iota(jnp.int32, (qb, 1), 0)
    qrow = jnp.concatenate([r * qb + local_q] * SWA_GROUP, axis=0)
    s_list = [lax.dot_general(q, kc_ref[...], nt, preferred_element_type=F32)]
    for w, k_ref in zip((-1, 0, 1), (k0_ref, k1_ref, k2_ref)):
        s = lax.dot_general(q, k_ref[...], nt, preferred_element_type=F32)
        krow = (r + w) * qb + lax.broadcasted_iota(jnp.int32, (1, qb), 1)
        valid = (jnp.abs(qrow - krow) <= WINDOW) & (krow >= nc) & (krow < t) & (qrow >= nc)
        s_list.append(jnp.where(valid, s, NEG_INF))
    m = sink
    for s in s_list:
        m = jnp.maximum(m, jnp.max(s, axis=-1, keepdims=True))
    denom = jnp.exp(sink - m)
    out = jnp.zeros((SWA_GROUP * qb, SWA_DH), F32)
    for s, v_ref in zip(s_list, (vc_ref, v0_ref, v1_ref, v2_ref)):
        e = jnp.exp(s - m)
        denom = denom + jnp.sum(e, axis=-1, keepdims=True)
        out = out + jnp.dot(e.astype(BF16), v_ref[...], preferred_element_type=F32)
    out = out / denom
    for g in range(SWA_GROUP):
        o_ref[:, g * SWA_DH:(g + 1) * SWA_DH] = out[g * qb:(g + 1) * qb, :].astype(o_ref.dtype)


def swa_attention(qkv, sink, nc, col_q, col_k, col_v):
    t = qkv.shape[0]
    qb = WINDOW
    nb = t // qb
    gw = SWA_GROUP * SWA_DH
    sink_b = jnp.broadcast_to(sink.astype(F32).reshape(SWA_KV_HEADS, SWA_GROUP, 1), (SWA_KV_HEADS, SWA_GROUP, LANES))
    kcb, vcb = col_k // SWA_DH, col_v // SWA_DH

    def win(cb, w):
        return pl.BlockSpec((qb, SWA_DH), lambda kv, r: (jnp.clip(r + w, 0, nb - 1), cb + kv))

    return pl.pallas_call(
        functools.partial(_swa_kernel, nc=nc, t=t),
        grid=(SWA_KV_HEADS, nb),
        in_specs=[pl.BlockSpec((1, SWA_GROUP, LANES), lambda kv, r: (kv, 0, 0)),
                  pl.BlockSpec((qb, gw), lambda kv, r: (r, col_q // gw + kv)),
                  pl.BlockSpec((nc, SWA_DH), lambda kv, r: (0, kcb + kv)),
                  win(kcb, -1), win(kcb, 0), win(kcb, 1),
                  pl.BlockSpec((nc, SWA_DH), lambda kv, r: (0, vcb + kv)),
                  win(vcb, -1), win(vcb, 0), win(vcb, 1)],
        out_specs=pl.BlockSpec((qb, gw), lambda kv, r: (r, kv)),
        out_shape=jax.ShapeDtypeStruct((t, SWA_HEADS * SWA_DH), BF16),
        compiler_params=_cp(("arbitrary", "arbitrary")),
        name="swa_attention",
    )(sink_b, qkv, qkv, qkv, qkv, qkv, qkv, qkv, qkv, qkv)


def _dn_conv_kernel(u_ref, prev_ref, next_ref, cw_ref, o_ref, *, tm, tc, nc, t, n_q, n_qk):
    i = pl.program_id(0)
    j = pl.program_id(1)
    s = CONV_STRIP

    @pl.when(j < n_qk)
    def _():
        scale = jnp.where(j < n_q, DN_DK ** -0.5, 1.0).astype(F32)
        for n, conv in enumerate(_conv3_strips(u_ref, prev_ref, next_ref, cw_ref, i, tm, nc, t)):
            y = _silu(conv)
            for c0 in range(0, tc, DN_DK):
                a = y[:, c0:c0 + DN_DK]
                ss = jnp.sum(a * a, axis=-1, keepdims=True)
                o_ref[n * s:(n + 1) * s, c0:c0 + DN_DK] = (a * lax.rsqrt(ss + EPS) * scale).astype(o_ref.dtype)

    @pl.when(j >= n_qk)
    def _():
        for n, conv in enumerate(_conv3_strips(u_ref, prev_ref, next_ref, cw_ref, i, tm, nc, t)):
            o_ref[n * s:(n + 1) * s, :] = _silu(conv).astype(o_ref.dtype)


def dn_conv(u, conv_w, nc, n_cols):
    t = u.shape[0]
    tm = _pick(t, (640, 320, 256, 128))
    tc = 512
    qk_cols = DN_QK_HEADS * DN_DK
    return pl.pallas_call(
        functools.partial(_dn_conv_kernel, tm=tm, tc=tc, nc=nc, t=t, n_q=qk_cols // tc, n_qk=2 * qk_cols // tc),
        grid=(t // tm, n_cols // tc),
        in_specs=_halo_specs(tm, tc, t, 0) + [pl.BlockSpec((3, tc), lambda i, k: (0, k))],
        out_specs=pl.BlockSpec((tm, tc), lambda i, k: (i, k)),
        out_shape=jax.ShapeDtypeStruct((t, n_cols), BF16),
        compiler_params=_cp(("arbitrary", "arbitrary")),
        name="dn_conv",
    )(u, u, u, conv_w)


DN_GQ = 8


def _split3(x):
    hi = x.astype(BF16)
    r1 = x - hi.astype(F32)
    mid = r1.astype(BF16)
    lo = (r1 - mid.astype(F32)).astype(BF16)
    return hi, mid, lo


def _delta_kernel(gates_ref, nega_ref, bias_ref, q_ref, k_ref, v_ref, o_ref, s_ref):
    d = pl.program_id(0)
    hg = pl.program_id(1)
    s_idx = pl.program_id(2)
    c = DN_CHUNK
    nt = (((1,), (1,)), ((), ()))
    tn = (((0,), (0,)), ((), ()))

    @pl.when(s_idx == 0)
    def _():
        s_ref[...] = jnp.zeros(s_ref.shape, F32)

    gt = gates_ref[...]
    shift = (2 * LANES - d * (2 * DN_V_HEADS) - hg * (2 * DN_GQ)) % LANES
    gt = pltpu.roll(gt, shift, 1)
    nega = pltpu.roll(nega_ref[...], shift, 1)
    bias = pltpu.roll(bias_ref[...], shift, 1)
    beta_all = jax.nn.sigmoid(gt)
    z = gt + bias[0:1, :]
    softplus = jnp.maximum(z, 0.0) + jnp.log(1.0 + jnp.exp(-jnp.abs(z)))
    g_all = nega[0:1, :] * softplus

    ri = lax.broadcasted_iota(jnp.int32, (c, c), 0)
    ci = lax.broadcasted_iota(jnp.int32, (c, c), 1)
    order = (ri - ci) * jnp.where(d == 0, 1, -1)
    incl = order >= 0
    strict = order > 0
    eye = ri == ci
    eye_f = eye.astype(F32)
    csum = incl.astype(BF16)
    hi, mid, lo = _split3(g_all)
    gam_all = (jnp.dot(csum, hi, preferred_element_type=F32) + jnp.dot(csum, mid, preferred_element_type=F32)
               + jnp.dot(csum, lo, preferred_element_type=F32))
    tot_all = jnp.sum(g_all, axis=0, keepdims=True)

    def dot(x, y):
        return jnp.dot(x, y, preferred_element_type=F32)

    hqs = range(DN_GQ)
    hvs = range(2 * DN_GQ)
    kb = [k_ref[:, h * DN_DK:(h + 1) * DN_DK] for h in hqs]
    qb = [q_ref[:, h * DN_DK:(h + 1) * DN_DK] for h in hqs]
    kq = [lax.dot_general(jnp.concatenate([kb[h], qb[h]], axis=0), kb[h], nt, preferred_element_type=F32)
          for h in hqs]
    kf = [x.astype(F32) for x in kb]
    qf = [x.astype(F32) for x in qb]
    beta = [beta_all[:, hv:hv + 1] for hv in hvs]
    gam = [gam_all[:, DN_V_HEADS + hv:DN_V_HEADS + hv + 1] for hv in hvs]
    tot = [tot_all[:, DN_V_HEADS + hv:DN_V_HEADS + hv + 1] for hv in hvs]
    gam_row = [jnp.sum(jnp.where(eye, g, 0.0), axis=0, keepdims=True) for g in gam]
    dec = [jnp.where(incl, jnp.exp(jnp.where(incl, g - gr, 0.0)), 0.0) for g, gr in zip(gam, gam_row)]
    a = [jnp.where(strict, beta[hv] * kq[hv // 2][:c] * dec[hv], 0.0) for hv in hvs]
    pw = [-x for x in a]
    tinv = [eye_f + p for p in pw]
    for _ in range(int(math.log2(c)) - 2):
        pwb = [p.astype(BF16) for p in pw]
        pw = [dot(p, p) for p in pwb]
        tinv = [t + dot(t.astype(BF16), p.astype(BF16)) for t, p in zip(tinv, pw)]
    tb = [t.astype(BF16) for t in tinv]
    resid = [eye_f - t - dot(x.astype(BF16), t_b) for t, x, t_b in zip(tinv, a, tb)]
    tinv = [t + dot(t_b, r.astype(BF16)) for t, t_b, r in zip(tinv, tb, resid)]
    tb = [t.astype(BF16) for t in tinv]
    eg = [jnp.exp(g) for g in gam]
    vf = [v_ref[:, hv * DN_DV:(hv + 1) * DN_DV].astype(F32) for hv in hvs]
    rhs = [jnp.concatenate([kf[hv // 2] * (beta[hv] * eg[hv]), vf[hv] * beta[hv]], axis=1).astype(BF16) for hv in hvs]
    sol = [dot(t_b, r) for t_b, r in zip(tb, rhs)]
    aqk = [jnp.where(incl, kq[hv // 2][c:] * dec[hv], 0.0).astype(BF16) for hv in hvs]
    wq = [jnp.concatenate([sol[hv][:, :DN_DK], qf[hv // 2] * eg[hv]], axis=0).astype(BF16) for hv in hvs]
    kd = [(kf[hv // 2] * jnp.exp(tot[hv] - gam[hv])).astype(BF16) for hv in hvs]
    state = [s_ref[hv] for hv in hvs]
    ws = [dot(x, s.astype(BF16)) for x, s in zip(wq, state)]
    vnb = [(sol[hv][:, DN_DK:] - ws[hv][:c]).astype(BF16) for hv in hvs]
    o = [ws[hv][c:] + dot(aqk[hv], vnb[hv]) for hv in hvs]
    upd = [lax.dot_general(kd[hv], vnb[hv], tn, preferred_element_type=F32) for hv in hvs]
    for hv in hvs:
        s_ref[hv] = jnp.exp(tot[hv]) * state[hv] + upd[hv]
        o_ref[0, :, hv * DN_DV:(hv + 1) * DN_DV] = o[hv].astype(o_ref.dtype)


def gated_delta(qkvc, gates, nega, bias, nc):
    t = qkvc.shape[0]
    c = DN_CHUNK
    nch = t // c
    ncc = nc // c
    qw = DN_GQ * DN_DK
    vw = 2 * DN_GQ * DN_DV
    qk_cols = DN_QK_HEADS * DN_DK

    def chunk(d, s):
        rev = jnp.where(s < ncc, ncc - 1 - s, nch - 1 - (s - ncc))
        return jnp.where(d == 0, s, rev)

    return pl.pallas_call(
        _delta_kernel,
        grid=(2, DN_QK_HEADS // DN_GQ, nch),
        in_specs=[pl.BlockSpec((c, LANES), lambda d, h, s: (chunk(d, s), 0)),
                  pl.BlockSpec((8, LANES), lambda d, h, s: (0, 0)),
                  pl.BlockSpec((8, LANES), lambda d, h, s: (0, 0)),
                  pl.BlockSpec((c, qw), lambda d, h, s: (chunk(d, s), h)),
                  pl.BlockSpec((c, qw), lambda d, h, s: (chunk(d, s), qk_cols // qw + h)),
                  pl.BlockSpec((c, vw), lambda d, h, s: (chunk(d, s), 2 * qk_cols // vw + h))],
        out_specs=pl.BlockSpec((1, c, vw), lambda d, h, s: (d, chunk(d, s), h)),
        out_shape=jax.ShapeDtypeStruct((2, t, DN_V_HEADS * DN_DV), BF16),
        scratch_shapes=[pltpu.VMEM((2 * DN_GQ, DN_DK, DN_DV), F32)],
        compiler_params=_cp(("arbitrary", "arbitrary", "arbitrary")),
        name="gated_delta",
    )(gates, nega, bias, qkvc, qkvc, qkvc)


def _attn_out_prologue(i, k, oa_ref, ob_ref, *, n_a):
    return jnp.where(k < n_a, oa_ref[...], ob_ref[...])


def _dn_out_prologue(i, k, of_ref, or_ref, z_ref, g_ref, *, tk):
    o = of_ref[0].astype(F32) + or_ref[0].astype(F32)
    z = z_ref[...].astype(F32)
    parts = []
    for c0 in range(0, tk, DN_DV):
        a = o[:, c0:c0 + DN_DV]
        ms = jnp.mean(a * a, axis=-1, keepdims=True)
        parts.append(a * lax.rsqrt(ms + EPS) * g_ref[...])
    return (jnp.concatenate(parts, axis=-1) * _silu(z)).astype(BF16)


def _ffn_prologue(i, k, ug, pg, ng, uu, pu, nu, cwg, cwu, *, tm, nc, t):
    gates = _conv3_strips(ug, pg, ng, cwg, i, tm, nc, t)
    ups = _conv3_strips(uu, pu, nu, cwu, i, tm, nc, t)
    return jnp.concatenate([(_silu(g) * u).astype(BF16) for g, u in zip(gates, ups)], axis=0)


def _final_norm_kernel(x_ref, g_ref, o_ref):
    x = x_ref[...]
    ms = jnp.mean(x * x, axis=-1, keepdims=True)
    o_ref[...] = x * lax.rsqrt(ms + EPS) * g_ref[...]


def final_norm(x, g, nc):
    t, d = x.shape
    tm = math.gcd(nc, 256)
    off = nc // tm
    return pl.pallas_call(
        _final_norm_kernel,
        grid=((t - nc) // tm,),
        in_specs=[pl.BlockSpec((tm, d), lambda i: (i + off, 0)), pl.BlockSpec((1, d), lambda i: (0, 0))],
        out_specs=pl.BlockSpec((tm, d), lambda i: (i, 0)),
        out_shape=jax.ShapeDtypeStruct((t - nc, d), F32),
        compiler_params=_cp(("arbitrary",)),
        name="final_norm",
    )(x, g.reshape(1, d))


def _rope_tables(n, nc):
    rows = n // GRID_W
    row = jnp.broadcast_to(jnp.arange(rows)[:, None], (rows, GRID_W)).reshape(-1).astype(F32)
    col = jnp.broadcast_to(jnp.arange(GRID_W)[None, :], (rows, GRID_W)).reshape(-1).astype(F32)
    inv = ROPE_BASE ** (-jnp.arange(ROPE_AXIS_FREQS, dtype=F32) / ROPE_AXIS_FREQS)
    ar, ac = row[:, None] * inv, col[:, None] * inv
    cos = jnp.concatenate([jnp.cos(ar), jnp.cos(ar), jnp.cos(ac), jnp.cos(ac)], axis=-1)
    sin = jnp.concatenate([-jnp.sin(ar), jnp.sin(ar), -jnp.sin(ac), jnp.sin(ac)], axis=-1)
    cos = jnp.concatenate([jnp.ones((nc, LANES), F32), cos], axis=0)
    sin = jnp.concatenate([jnp.zeros((nc, LANES), F32), sin], axis=0)
    return cos, sin


def kernel(x, c, ctx, c_ctx, w_mod, b_mod, norm1_g, norm2_g, attn_w_in, diff_lambda, diff_subln_g, swa_sink,
           attn_w_out, dn_w_in, dn_conv_w, dn_a_log, dn_dt_bias, dn_norm_g, dn_w_out, ffn_w_up, ffn_conv_w,
           ffn_w_down, final_norm_g):
    assert x.shape[0] == 1, "single-sequence kernel"
    n, d = x.shape[1], x.shape[2]
    nc = ctx.shape[1]
    t = nc + n
    depth = w_mod.shape[0]
    xs = jnp.concatenate([ctx[0], x[0]], axis=0)
    cc = jnp.zeros((8, d), F32).at[0].set(c[0]).at[1].set(c_ctx)
    mods = mod_vectors(cc, w_mod, b_mod)
    cos, sin = _rope_tables(n, nc)

    da, sb_, skv = DIFF_HEADS * 2 * DIFF_DH, SWA_HEADS * SWA_DH, SWA_KV_HEADS * SWA_DH
    col_qa, col_ka, col_qb = 0, da, 2 * da
    col_kb = col_qb + sb_
    col_va = col_kb + skv
    col_vb = col_va + da
    n_attn = col_vb + skv
    cols = jnp.arange(n_attn)
    cmask = (cols < col_va).astype(F32).reshape(1, n_attn)
    cscale = jnp.where(cols < col_ka, DIFF_DH ** -0.5 * LOG2E,
                       jnp.where((cols >= col_qb) & (cols < col_kb), SWA_DH ** -0.5, 1.0))
    cscale = cscale.astype(F32).reshape(1, n_attn)

    tm_p = _pick(t, (640, 320, 256, 128))
    hidden = ffn_w_down.shape[1]
    tk_f = _pick(hidden, (512, 256, 128))
    qkv_cols = 2 * DN_QK_HEADS * DN_DK + DN_V_HEADS * DN_DV
    dn_out = DN_V_HEADS * DN_DV

    for layer in range(depth):
        m = mods[layer]
        sh1, sc1, g1, sh2, sc2, g2 = (m[:, j * d:(j + 1) * d] for j in range(6))
        i = layer // 2
        if layer % 2 == 0:
            lam_init = 0.8 - 0.6 * math.exp(-0.3 * layer)
            w = attn_w_in[i]
            w = jnp.concatenate([w[:, 0:2 * da], w[:, 3 * da:3 * da + sb_ + skv], w[:, 2 * da:3 * da],
                                 w[:, 3 * da + sb_ + skv:]], axis=1).astype(BF16)
            qkv = norm_mod_matmul(xs, norm1_g[layer], sh1, sc1, w, nc, rope_args=(cos, sin, cmask, cscale),
                                  tn_candidates=(512, 256, 128))
            oa = diff_attention(qkv, diff_lambda[i], diff_subln_g[i], nc, lam_init, col_qa, col_ka, col_va)
            ob = swa_attention(qkv, swa_sink[i], nc, col_qb, col_kb, col_vb)
            tk = 512
            n_a = da // tk
            xs = proj_residual(
                functools.partial(_attn_out_prologue, n_a=n_a), [oa, ob],
                [pl.BlockSpec((tm_p, tk), lambda r, k: (r, jnp.minimum(k, n_a - 1))),
                 pl.BlockSpec((tm_p, tk), lambda r, k: (r, jnp.maximum(k - n_a, 0)))],
                attn_w_out[i].astype(BF16), xs, g1, nc, tm_p, tk)
        else:
            w = dn_w_in[i]
            u = norm_mod_matmul(xs, norm1_g[layer], sh1, sc1, w[:, :qkv_cols + dn_out].astype(BF16), nc)
            gates = norm_mod_matmul(xs, norm1_g[layer], sh1, sc1, w[:, qkv_cols + dn_out:].astype(BF16), nc, out_dtype=F32)
            qkvc = dn_conv(u, dn_conv_w[i], nc, qkv_cols)
            zero = jnp.zeros((2, DN_V_HEADS), F32)
            nega = jnp.stack([zero, -jnp.exp(dn_a_log[i].astype(F32))], axis=1).reshape(1, 4 * DN_V_HEADS)
            bias = jnp.stack([zero, dn_dt_bias[i].astype(F32)], axis=1).reshape(1, 4 * DN_V_HEADS)
            o2 = gated_delta(qkvc, gates, jnp.broadcast_to(nega, (8, LANES)),
                             jnp.broadcast_to(bias, (8, LANES)), nc)
            tk = 512
            zoff = qkv_cols // tk
            xs = proj_residual(
                functools.partial(_dn_out_prologue, tk=tk), [o2, o2, u, dn_norm_g[i].reshape(1, DN_DV)],
                [pl.BlockSpec((1, tm_p, tk), lambda r, k: (0, r, k)),
                 pl.BlockSpec((1, tm_p, tk), lambda r, k: (1, r, k)),
                 pl.BlockSpec((tm_p, tk), lambda r, k: (r, zoff + k)),
                 pl.BlockSpec((1, DN_DV), lambda r, k: (0, 0))],
                dn_w_out[i].astype(BF16), xs, g1, nc, tm_p, tk)
        uf = norm_mod_matmul(xs, norm2_g[layer], sh2, sc2, ffn_w_up[layer].astype(BF16), nc)
        cw = ffn_conv_w[layer]
        xs = proj_residual(
            functools.partial(_ffn_prologue, tm=tm_p, nc=nc, t=t), [uf, uf, uf, uf, uf, uf, cw, cw],
            _halo_specs(tm_p, tk_f, t, 0) + _halo_specs(tm_p, tk_f, t, hidden // tk_f)
            + [pl.BlockSpec((3, tk_f), lambda r, k: (0, k)),
               pl.BlockSpec((3, tk_f), lambda r, k: (0, hidden // tk_f + k))],
            ffn_w_down[layer].astype(BF16), xs, g2, nc, tm_p, tk_f)
    return final_norm(xs, final_norm_g, nc)[None]
```

```python
import functools
import math

import jax
import jax.numpy as jnp
from jax import lax
from jax.experimental import pallas as pl
from jax.experimental.pallas import tpu as pltpu

F32 = jnp.float32
BF16 = jnp.bfloat16

EPS = 1e-6
NEG_INF = -1e30
GRID_W = 64
ROPE_BASE = 10000.0
ROPE_AXIS_FREQS = 32
DIFF_HEADS = 4
DIFF_DH = 128
SWA_HEADS = 8
SWA_KV_HEADS = 2
SWA_GROUP = SWA_HEADS // SWA_KV_HEADS
SWA_DH = 128
WINDOW = 128
DN_QK_HEADS = 16
DN_V_HEADS = 32
DN_DK = 128
DN_DV = 128
DN_CHUNK = 64
LANES = 128
HALO = 16
VMEM_LIMIT = 56 * 1024 * 1024


def _cp(sem):
    return pltpu.CompilerParams(dimension_semantics=sem, vmem_limit_bytes=VMEM_LIMIT)


def _pick(n, candidates):
    for c in candidates:
        if n % c == 0:
            return c
    raise ValueError(f"no tile for {n} among {candidates}")


def _silu(x):
    return x * jax.nn.sigmoid(x)


def _rows(i, tm):
    return i * tm + lax.broadcasted_iota(jnp.int32, (tm, 1), 0)


def _mod_kernel(a_ref, w_ref, b_ref, o_ref):
    a = _silu(a_ref[...]).astype(BF16)
    w = w_ref[0].astype(BF16)
    o_ref[0] = jnp.dot(a, w, preferred_element_type=F32) + b_ref[0]


def mod_vectors(cc, w_mod, b_mod):
    depth, d, n = w_mod.shape
    tn = _pick(n, (1024, 512, 256, 128))
    return pl.pallas_call(
        _mod_kernel,
        grid=(depth, n // tn),
        in_specs=[pl.BlockSpec((8, d), lambda l, j: (0, 0)),
                  pl.BlockSpec((1, d, tn), lambda l, j: (l, 0, j)),
                  pl.BlockSpec((1, 1, tn), lambda l, j: (l, 0, j))],
        out_specs=pl.BlockSpec((1, 8, tn), lambda l, j: (l, 0, j)),
        out_shape=jax.ShapeDtypeStruct((depth, 8, n), F32),
        compiler_params=_cp(("arbitrary", "arbitrary")),
        name="mod_vectors",
    )(cc, w_mod, b_mod.reshape(depth, 1, n))


def _nmm_kernel(x_ref, g_ref, sh_ref, sc_ref, w_ref, *rest, tm, tn, nc, rope):
    if rope:
        cos_ref, sin_ref, cmask_ref, cscale_ref, o_ref, h_ref = rest
    else:
        o_ref, h_ref = rest
    i = pl.program_id(0)
    j = pl.program_id(1)

    @pl.when(j == 0)
    def _():
        sub = math.gcd(math.gcd(tm, 32), nc)
        gain = g_ref[...] * (1.0 + sc_ref[0:2, :])
        for r0 in range(0, tm, sub):
            x = x_ref[r0:r0 + sub, :]
            ms = jnp.mean(x * x, axis=-1, keepdims=True)
            is_ctx = i * tm + r0 < nc
            gs = jnp.where(is_ctx, gain[1:2, :], gain[0:1, :])
            sh = jnp.where(is_ctx, sh_ref[1:2, :], sh_ref[0:1, :])
            h_ref[r0:r0 + sub, :] = (x * lax.rsqrt(ms + EPS) * gs + sh).astype(BF16)

    acc = jnp.dot(h_ref[...], w_ref[...], preferred_element_type=F32)
    if not rope:
        o_ref[...] = acc.astype(o_ref.dtype)
        return
    rs = math.gcd(tm, 64)
    lane = lax.broadcasted_iota(jnp.int32, (rs, LANES), 1)
    first = (lane % (2 * ROPE_AXIS_FREQS)) < ROPE_AXIS_FREQS
    for r0 in range(0, tm, rs):
        cos = cos_ref[r0:r0 + rs, :]
        sin = sin_ref[r0:r0 + rs, :]
        for c0 in range(0, tn, LANES):
            a = acc[r0:r0 + rs, c0:c0 + LANES]
            partner = jnp.where(first, pltpu.roll(a, LANES - ROPE_AXIS_FREQS, 1), pltpu.roll(a, ROPE_AXIS_FREQS, 1))
            roped = a * cos + partner * sin
            a = jnp.where(cmask_ref[:, c0:c0 + LANES] > 0.5, roped, a) * cscale_ref[:, c0:c0 + LANES]
            o_ref[r0:r0 + rs, c0:c0 + LANES] = a.astype(o_ref.dtype)


def norm_mod_matmul(x, g, sh, sc, w, nc, rope_args=None, tn_candidates=(1024, 512, 256, 128), out_dtype=BF16):
    t, d = x.shape
    n = w.shape[1]
    tm = _pick(t, (1280, 640, 512, 320, 256, 128))
    tn = _pick(n, tn_candidates)
    rope = rope_args is not None
    in_specs = [pl.BlockSpec((tm, d), lambda i, j: (i, 0), pipeline_mode=pl.Buffered(1)),
                pl.BlockSpec((1, d), lambda i, j: (0, 0)),
                pl.BlockSpec((8, d), lambda i, j: (0, 0)),
                pl.BlockSpec((8, d), lambda i, j: (0, 0)),
                pl.BlockSpec((d, tn), lambda i, j: (0, j))]
    args = [x, g.reshape(1, d), sh, sc, w]
    if rope:
        cos, sin, cmask, cscale = rope_args
        in_specs += [pl.BlockSpec((tm, LANES), lambda i, j: (i, 0)),
                     pl.BlockSpec((tm, LANES), lambda i, j: (i, 0)),
                     pl.BlockSpec((1, tn), lambda i, j: (0, j)),
                     pl.BlockSpec((1, tn), lambda i, j: (0, j))]
        args += [cos, sin, cmask, cscale]
    return pl.pallas_call(
        functools.partial(_nmm_kernel, tm=tm, tn=tn, nc=nc, rope=rope),
        grid=(t // tm, n // tn),
        in_specs=in_specs,
        out_specs=pl.BlockSpec((tm, tn), lambda i, j: (i, j)),
        out_shape=jax.ShapeDtypeStruct((t, n), out_dtype),
        scratch_shapes=[pltpu.VMEM((tm, d), BF16)],
        compiler_params=_cp(("arbitrary", "arbitrary")),
        name="norm_mod_matmul",
    )(*args)


def _proj_res_kernel(*refs, prologue, n_lhs, nk, tm, nc):
    lhs = refs[:n_lhs]
    w_ref, x_ref, gate_ref, o_ref, acc_ref = refs[n_lhs:]
    i = pl.program_id(0)
    k = pl.program_id(1)
    a = prologue(i, k, *lhs)
    p = jnp.dot(a, w_ref[...], preferred_element_type=F32)

    @pl.when(k == 0)
    def _():
        acc_ref[...] = p

    @pl.when(k > 0)
    def _():
        acc_ref[...] += p

    @pl.when(k == nk - 1)
    def _():
        is_ctx = _rows(i, tm) < nc
        gate = jnp.where(is_ctx, gate_ref[1:2, :], gate_ref[0:1, :])
        o_ref[...] = x_ref[...] + gate * acc_ref[...]


def proj_residual(prologue, lhs_args, lhs_specs, w, x, gate, nc, tm, tk):
    t, d = x.shape
    kdim = w.shape[0]
    nk = kdim // tk
    in_specs = list(lhs_specs) + [pl.BlockSpec((tk, d), lambda i, k: (k, 0)),
                                  pl.BlockSpec((tm, d), lambda i, k: (i, 0)),
                                  pl.BlockSpec((8, d), lambda i, k: (0, 0))]
    return pl.pallas_call(
        functools.partial(_proj_res_kernel, prologue=prologue, n_lhs=len(lhs_args), nk=nk, tm=tm, nc=nc),
        grid=(t // tm, nk),
        in_specs=in_specs,
        out_specs=pl.BlockSpec((tm, d), lambda i, k: (i, 0)),
        out_shape=jax.ShapeDtypeStruct((t, d), F32),
        scratch_shapes=[pltpu.VMEM((tm, d), F32)],
        compiler_params=_cp(("arbitrary", "arbitrary")),
        name="proj_residual",
    )(*lhs_args, w, x, gate)


def _halo_specs(tm, tk, t, col_off):
    nh = t // HALO
    per = tm // HALO
    return [pl.BlockSpec((tm, tk), lambda i, k: (i, col_off + k)),
            pl.BlockSpec((HALO, tk), lambda i, k: (jnp.maximum(i * per - 1, 0), col_off + k)),
            pl.BlockSpec((HALO, tk), lambda i, k: (jnp.minimum((i + 1) * per, nh - 1), col_off + k))]


CONV_STRIP = 32
SUBLANES = 8


def _patch_rows(x, g0, patch):
    parts = [x[:g0], patch(x[g0:g0 + SUBLANES]), x[g0 + SUBLANES:]]
    return jnp.concatenate([p for p in parts if p.shape[0]], axis=0)


def _conv3_strips(u_ref, prev_ref, next_ref, cw_ref, i, tm, nc, t):
    s = CONV_STRIP
    sub = lax.broadcasted_iota(jnp.int32, (SUBLANES, 1), 0)
    w = cw_ref[...]
    w0, w1, w2 = w[0:1, :], w[1:2, :], w[2:3, :]
    halo_prev = jnp.where(i == 0, 0.0, prev_ref[...].astype(F32)[HALO - 1:HALO, :])
    halo_next = jnp.where(i == t // tm - 1, 0.0, next_ref[...].astype(F32)[0:1, :])
    n = tm // s
    strips = [u_ref[j * s:(j + 1) * s, :].astype(F32) for j in range(n)]
    first_blk, first_row = nc // tm, nc % tm
    last_blk, last_row = (nc - 1) // tm, (nc - 1) % tm
    for j in range(n):
        u = strips[j]
        before = halo_prev if j == 0 else strips[j - 1][s - 1:s, :]
        after = halo_next if j == n - 1 else strips[j + 1][0:1, :]
        up = _patch_rows(pltpu.roll(u, 1, 0), 0, lambda g: jnp.where(sub == 0, before, g))
        un = _patch_rows(pltpu.roll(u, s - 1, 0), s - SUBLANES, lambda g: jnp.where(sub == SUBLANES - 1, after, g))
        if 0 < nc < t and first_row // s == j:
            r = first_row % s
            up = _patch_rows(up, r // SUBLANES * SUBLANES,
                             lambda g: jnp.where((sub == r % SUBLANES) & (i == first_blk), 0.0, g))
        if 0 < nc < t and last_row // s == j:
            r = last_row % s
            un = _patch_rows(un, r // SUBLANES * SUBLANES,
                             lambda g: jnp.where((sub == r % SUBLANES) & (i == last_blk), 0.0, g))
        yield up * w0 + u * w1 + un * w2


ATTN_STRIP = 16
LOG2E = math.log2(math.e)


def _diff_attn_kernel(lam_ref, g_ref, q_ref, k_ref, v_ref, o_ref, m_ref, l_ref, acc_ref, p_ref, alpha_ref,
                      *, tq, tk, nc, nk, lam_init):
    i = pl.program_id(1)
    kk = pl.program_id(2)

    @pl.when(kk == 0)
    def _():
        m_ref[...] = jnp.full(m_ref.shape, NEG_INF, F32)
        l_ref[...] = jnp.zeros(l_ref.shape, F32)
        acc_ref[...] = jnp.zeros(acc_ref.shape, F32)

    def step(masked):
        nt = (((1,), (1,)), ((), ()))
        v = v_ref[...]
        for c in range(2):
            s_all = lax.dot_general(q_ref[:, c * DIFF_DH:(c + 1) * DIFF_DH], k_ref[:, c * DIFF_DH:(c + 1) * DIFF_DH], nt,
                                    preferred_element_type=F32)
            for r0 in range(0, tq, ATTN_STRIP):
                rows = slice(r0, r0 + ATTN_STRIP)
                s = s_all[rows, :]
                if masked:
                    qrow = i * tq + r0 + lax.broadcasted_iota(jnp.int32, (ATTN_STRIP, 1), 0)
                    kcol = kk * tk + lax.broadcasted_iota(jnp.int32, (1, tk), 1)
                    s = jnp.where((qrow >= nc) | (kcol < nc), s, NEG_INF)
                m_prev = m_ref[c, rows, :]
                m_new = jnp.maximum(m_prev, jnp.max(s, axis=-1, keepdims=True))
                alpha = jnp.exp2(m_prev - m_new)
                p = jnp.exp2(s - m_new)
                l_ref[c, rows, :] = alpha * l_ref[c, rows, :] + jnp.sum(p, axis=-1, keepdims=True)
                m_ref[c, rows, :] = m_new
                alpha_ref[c, rows, :] = alpha
                p_ref[c, rows, :] = p.astype(BF16)
            acc_ref[c] = alpha_ref[c] * acc_ref[c] + jnp.dot(p_ref[c], v, preferred_element_type=F32)

    has_ctx_rows = i * tq < nc

    @pl.when(has_ctx_rows)
    def _():
        step(True)

    @pl.when(jnp.logical_not(has_ctx_rows))
    def _():
        step(False)

    @pl.when(kk == nk - 1)
    def _():
        lf = lam_ref[...]
        lam = (jnp.exp(jnp.sum(lf[0:1] * lf[1:2], axis=-1, keepdims=True))
               - jnp.exp(jnp.sum(lf[2:3] * lf[3:4], axis=-1, keepdims=True)) + lam_init)
        o = acc_ref[0] / l_ref[0] - lam * (acc_ref[1] / l_ref[1])
        ms = jnp.mean(o * o, axis=-1, keepdims=True)
        o_ref[...] = (o * lax.rsqrt(ms + EPS) * g_ref[...] * (1.0 - lam_init)).astype(o_ref.dtype)


def diff_attention(qkv, lam_vec, subln_g, nc, lam_init, col_q, col_k, col_v):
    t = qkv.shape[0]
    tq = _pick(t, (1280, 640, 256, 128))
    tk = _pick(t, (1280, 640, 256, 128))
    nk = t // tk
    hw = 2 * DIFF_DH
    return pl.pallas_call(
        functools.partial(_diff_attn_kernel, tq=tq, tk=tk, nc=nc, nk=nk, lam_init=lam_init),
        grid=(DIFF_HEADS, t // tq, nk),
        in_specs=[pl.BlockSpec((4, DIFF_DH), lambda h, i, k: (0, 0)),
                  pl.BlockSpec((1, hw), lambda h, i, k: (0, 0)),
                  pl.BlockSpec((tq, hw), lambda h, i, k: (i, col_q // hw + h)),
                  pl.BlockSpec((tk, hw), lambda h, i, k: (k, col_k // hw + h)),
                  pl.BlockSpec((tk, hw), lambda h, i, k: (k, col_v // hw + h))],
        out_specs=pl.BlockSpec((tq, hw), lambda h, i, k: (i, h)),
        out_shape=jax.ShapeDtypeStruct((t, DIFF_HEADS * hw), BF16),
        scratch_shapes=[pltpu.VMEM((2, tq, 1), F32), pltpu.VMEM((2, tq, 1), F32), pltpu.VMEM((2, tq, hw), F32),
                        pltpu.VMEM((2, tq, tk), BF16), pltpu.VMEM((2, tq, 1), F32)],
        compiler_params=_cp(("arbitrary", "arbitrary", "arbitrary")),
        name="diff_attention",
    )(lam_vec, subln_g.reshape(1, hw), qkv, qkv, qkv)


def _swa_kernel(sink_ref, q_ref, kc_ref, k0_ref, k1_ref, k2_ref, vc_ref, v0_ref, v1_ref, v2_ref, o_ref,
                *, nc, t):
    r = pl.program_id(1)
    qb = WINDOW
    q = jnp.concatenate([q_ref[:, g * SWA_DH:(g + 1) * SWA_DH] for g in range(SWA_GROUP)], axis=0)
    sk = sink_ref[0]
    sink = jnp.concatenate([jnp.broadcast_to(sk[g:g + 1, 0:1], (qb, 1)) for g in range(SWA_GROUP)], axis=0)
    nt = (((1,), (1,)), ((), ()))
    local_q = lax.broadcasted_iota(jnp.int32, (qb, 1), 0)
    qrow = jnp.concatenate([r * qb + local_q] * SWA_GROUP, axis=0)
    s_list = [lax.dot_general(q, kc_ref[...], nt, preferred_element_type=F32)]
    for w, k_ref in zip((-1, 0, 1), (k0_ref, k1_ref, k2_ref)):
        s = lax.dot_general(q, k_ref[...], nt, preferred_element_type=F32)
        krow = (r + w) * qb + lax.broadcasted_iota(jnp.int32, (1, qb), 1)
        valid = (jnp.abs(qrow - krow) <= WINDOW) & (krow >= nc) & (krow < t) & (qrow >= nc)
        s_list.append(jnp.where(valid, s, NEG_INF))
    m = sink
    for s in s_list:
        m = jnp.maximum(m, jnp.max(s, axis=-1, keepdims=True))
    denom = jnp.exp(sink - m)
    out = jnp.zeros((SWA_GROUP * qb, SWA_DH), F32)
    for s, v_ref in zip(s_list, (vc_ref, v0_ref, v1_ref, v2_ref)):
        e = jnp.exp(s - m)
        denom = denom + jnp.sum(e, axis=-1, keepdims=True)
        out = out + jnp.dot(e.astype(BF16), v_ref[...], preferred_element_type=F32)
    out = out / denom
    for g in range(SWA_GROUP):
        o_ref[:, g * SWA_DH:(g + 1) * SWA_DH] = out[g * qb:(g + 1) * qb, :].astype(o_ref.dtype)


def swa_attention(qkv, sink, nc, col_q, col_k, col_v):
    t = qkv.shape[0]
    qb = WINDOW
    nb = t // qb
    gw = SWA_GROUP * SWA_DH
    sink_b = jnp.broadcast_to(sink.astype(F32).reshape(SWA_KV_HEADS, SWA_GROUP, 1), (SWA_KV_HEADS, SWA_GROUP, LANES))
    kcb, vcb = col_k // SWA_DH, col_v // SWA_DH

    def win(cb, w):
        return pl.BlockSpec((qb, SWA_DH), lambda kv, r: (jnp.clip(r + w, 0, nb - 1), cb + kv))

    return pl.pallas_call(
        functools.partial(_swa_kernel, nc=nc, t=t),
        grid=(SWA_KV_HEADS, nb),
        in_specs=[pl.BlockSpec((1, SWA_GROUP, LANES), lambda kv, r: (kv, 0, 0)),
                  pl.BlockSpec((qb, gw), lambda kv, r: (r, col_q // gw + kv)),
                  pl.BlockSpec((nc, SWA_DH), lambda kv, r: (0, kcb + kv)),
                  win(kcb, -1), win(kcb, 0), win(kcb, 1),
                  pl.BlockSpec((nc, SWA_DH), lambda kv, r: (0, vcb + kv)),
                  win(vcb, -1), win(vcb, 0), win(vcb, 1)],
        out_specs=pl.BlockSpec((qb, gw), lambda kv, r: (r, kv)),
        out_shape=jax.ShapeDtypeStruct((t, SWA_HEADS * SWA_DH), BF16),
        compiler_params=_cp(("arbitrary", "arbitrary")),
        name="swa_attention",
    )(sink_b, qkv, qkv, qkv, qkv, qkv, qkv, qkv, qkv, qkv)


def _dn_conv_kernel(u_ref, prev_ref, next_ref, cw_ref, o_ref, *, tm, tc, nc, t, n_q, n_qk):
    i = pl.program_id(0)
    j = pl.program_id(1)
    s = CONV_STRIP

    @pl.when(j < n_qk)
    def _():
        scale = jnp.where(j < n_q, DN_DK ** -0.5, 1.0).astype(F32)
        for n, conv in enumerate(_conv3_strips(u_ref, prev_ref, next_ref, cw_ref, i, tm, nc, t)):
            y = _silu(conv)
            for c0 in range(0, tc, DN_DK):
                a = y[:, c0:c0 + DN_DK]
                ss = jnp.sum(a * a, axis=-1, keepdims=True)
                o_ref[n * s:(n + 1) * s, c0:c0 + DN_DK] = (a * lax.rsqrt(ss + EPS) * scale).astype(o_ref.dtype)

    @pl.when(j >= n_qk)
    def _():
        for n, conv in enumerate(_conv3_strips(u_ref, prev_ref, next_ref, cw_ref, i, tm, nc, t)):
            o_ref[n * s:(n + 1) * s, :] = _silu(conv).astype(o_ref.dtype)


def dn_conv(u, conv_w, nc, n_cols):
    t = u.shape[0]
    tm = _pick(t, (640, 320, 256, 128))
    tc = 512
    qk_cols = DN_QK_HEADS * DN_DK
    return pl.pallas_call(
        functools.partial(_dn_conv_kernel, tm=tm, tc=tc, nc=nc, t=t, n_q=qk_cols // tc, n_qk=2 * qk_cols // tc),
        grid=(t // tm, n_cols // tc),
        in_specs=_halo_specs(tm, tc, t, 0) + [pl.BlockSpec((3, tc), lambda i, k: (0, k))],
        out_specs=pl.BlockSpec((tm, tc), lambda i, k: (i, k)),
        out_shape=jax.ShapeDtypeStruct((t, n_cols), BF16),
        compiler_params=_cp(("arbitrary", "arbitrary")),
        name="dn_conv",
    )(u, u, u, conv_w)


DN_GQ = 16


def _split3(x):
    hi = x.astype(BF16)
    r1 = x - hi.astype(F32)
    mid = r1.astype(BF16)
    lo = (r1 - mid.astype(F32)).astype(BF16)
    return hi, mid, lo


def _delta_kernel(gates_ref, nega_ref, bias_ref, q_ref, k_ref, v_ref, o_ref, s_ref):
    d = pl.program_id(0)
    hg = pl.program_id(1)
    s_idx = pl.program_id(2)
    c = DN_CHUNK
    nt = (((1,), (1,)), ((), ()))
    tn = (((0,), (0,)), ((), ()))

    @pl.when(s_idx == 0)
    def _():
        s_ref[...] = jnp.zeros(s_ref.shape, F32)

    gt = gates_ref[...]
    shift = (2 * LANES - d * (2 * DN_V_HEADS) - hg * (2 * DN_GQ)) % LANES
    gt = pltpu.roll(gt, shift, 1)
    nega = pltpu.roll(nega_ref[...], shift, 1)
    bias = pltpu.roll(bias_ref[...], shift, 1)
    beta_all = jax.nn.sigmoid(gt)
    z = gt + bias[0:1, :]
    softplus = jnp.maximum(z, 0.0) + jnp.log(1.0 + jnp.exp(-jnp.abs(z)))
    g_all = nega[0:1, :] * softplus

    ri = lax.broadcasted_iota(jnp.int32, (c, c), 0)
    ci = lax.broadcasted_iota(jnp.int32, (c, c), 1)
    order = (ri - ci) * jnp.where(d == 0, 1, -1)
    incl = order >= 0
    strict = order > 0
    eye = ri == ci
    eye_f = eye.astype(F32)
    csum = incl.astype(BF16)
    hi, mid, lo = _split3(g_all)
    gam_all = (jnp.dot(csum, hi, preferred_element_type=F32) + jnp.dot(csum, mid, preferred_element_type=F32)
               + jnp.dot(csum, lo, preferred_element_type=F32))
    tot_all = jnp.sum(g_all, axis=0, keepdims=True)

    def dot(x, y):
        return jnp.dot(x, y, preferred_element_type=F32)

    hqs = range(DN_GQ)
    hvs = range(2 * DN_GQ)
    kb = [k_ref[:, h * DN_DK:(h + 1) * DN_DK] for h in hqs]
    qb = [q_ref[:, h * DN_DK:(h + 1) * DN_DK] for h in hqs]
    kq = [lax.dot_general(jnp.concatenate([kb[h], qb[h]], axis=0), kb[h], nt, preferred_element_type=F32)
          for h in hqs]
    kf = [x.astype(F32) for x in kb]
    qf = [x.astype(F32) for x in qb]
    beta = [beta_all[:, hv:hv + 1] for hv in hvs]
    gam = [gam_all[:, DN_V_HEADS + hv:DN_V_HEADS + hv + 1] for hv in hvs]
    tot = [tot_all[:, DN_V_HEADS + hv:DN_V_HEADS + hv + 1] for hv in hvs]
    gam_row = [jnp.sum(jnp.where(eye, g, 0.0), axis=0, keepdims=True) for g in gam]
    dec = [jnp.where(incl, jnp.exp(jnp.where(incl, g - gr, 0.0)), 0.0) for g, gr in zip(gam, gam_row)]
    a = [jnp.where(strict, beta[hv] * kq[hv // 2][:c] * dec[hv], 0.0) for hv in hvs]
    pw = [-x for x in a]
    tinv = [eye_f + p for p in pw]
    for _ in range(int(math.log2(c)) - 2):
        pwb = [p.astype(BF16) for p in pw]
        pw = [dot(p, p) for p in pwb]
        tinv = [t + dot(t.astype(BF16), p.astype(BF16)) for t, p in zip(tinv, pw)]
    tb = [t.astype(BF16) for t in tinv]
    resid = [eye_f - t - dot(x.astype(BF16), t_b) for t, x, t_b in zip(tinv, a, tb)]
    tinv = [t + dot(t_b, r.astype(BF16)) for t, t_b, r in zip(tinv, tb, resid)]
    tb = [t.astype(BF16) for t in tinv]
    eg = [jnp.exp(g) for g in gam]
    vf = [v_ref[:, hv * DN_DV:(hv + 1) * DN_DV].astype(F32) for hv in hvs]
    rhs = [jnp.concatenate([kf[hv // 2] * (beta[hv] * eg[hv]), vf[hv] * beta[hv]], axis=1).astype(BF16) for hv in hvs]
    sol = [dot(t_b, r) for t_b, r in zip(tb, rhs)]
    aqk = [jnp.where(incl, kq[hv // 2][c:] * dec[hv], 0.0).astype(BF16) for hv in hvs]
    wq = [jnp.concatenate([sol[hv][:, :DN_DK], qf[hv // 2] * eg[hv]], axis=0).astype(BF16) for hv in hvs]
    kd = [(kf[hv // 2] * jnp.exp(tot[hv] - gam[hv])).astype(BF16) for hv in hvs]
    state = [s_ref[hv] for hv in hvs]
    ws = [dot(x, s.astype(BF16)) for x, s in zip(wq, state)]
    vnb = [(sol[hv][:, DN_DK:] - ws[hv][:c]).astype(BF16) for hv in hvs]
    o = [ws[hv][c:] + dot(aqk[hv], vnb[hv]) for hv in hvs]
    upd = [lax.dot_general(kd[hv], vnb[hv], tn, preferred_element_type=F32) for hv in hvs]
    for hv in hvs:
        s_ref[hv] = jnp.exp(tot[hv]) * state[hv] + upd[hv]
        o_ref[0, :, hv * DN_DV:(hv + 1) * DN_DV] = o[hv].astype(o_ref.dtype)


def gated_delta(qkvc, gates, nega, bias, nc):
    t = qkvc.shape[0]
    c = DN_CHUNK
    nch = t // c
    ncc = nc // c
    qw = DN_GQ * DN_DK
    vw = 2 * DN_GQ * DN_DV
    qk_cols = DN_QK_HEADS * DN_DK

    def chunk(d, s):
        rev = jnp.where(s < ncc, ncc - 1 - s, nch - 1 - (s - ncc))
        return jnp.where(d == 0, s, rev)

    return pl.pallas_call(
        _delta_kernel,
        grid=(2, DN_QK_HEADS // DN_GQ, nch),
        in_specs=[pl.BlockSpec((c, LANES), lambda d, h, s: (chunk(d, s), 0)),
                  pl.BlockSpec((8, LANES), lambda d, h, s: (0, 0)),
                  pl.BlockSpec((8, LANES), lambda d, h, s: (0, 0)),
                  pl.BlockSpec((c, qw), lambda d, h, s: (chunk(d, s), h)),
                  pl.BlockSpec((c, qw), lambda d, h, s: (chunk(d, s), qk_cols // qw + h)),
                  pl.BlockSpec((c, vw), lambda d, h, s: (chunk(d, s), 2 * qk_cols // vw + h))],
        out_specs=pl.BlockSpec((1, c, vw), lambda d, h, s: (d, chunk(d, s), h)),
        out_shape=jax.ShapeDtypeStruct((2, t, DN_V_HEADS * DN_DV), BF16),
        scratch_shapes=[pltpu.VMEM((2 * DN_GQ, DN_DK, DN_DV), F32)],
        compiler_params=_cp(("arbitrary", "arbitrary", "arbitrary")),
        name="gated_delta",
    )(gates, nega, bias, qkvc, qkvc, qkvc)


def _attn_out_prologue(i, k, oa_ref, ob_ref, *, n_a):
    return jnp.where(k < n_a, oa_ref[...], ob_ref[...])


def _dn_out_prologue(i, k, of_ref, or_ref, z_ref, g_ref, *, tk):
    tm = z_ref.shape[0]
    rs = math.gcd(tm, 16)
    strips = []
    for r0 in range(0, tm, rs):
        parts = []
        for c0 in range(0, tk, DN_DV):
            a = (of_ref[0, r0:r0 + rs, c0:c0 + DN_DV].astype(F32) + or_ref[0, r0:r0 + rs, c0:c0 + DN_DV].astype(F32))
            ms = jnp.mean(a * a, axis=-1, keepdims=True)
            z = z_ref[r0:r0 + rs, c0:c0 + DN_DV].astype(F32)
            parts.append((a * lax.rsqrt(ms + EPS) * g_ref[...] * _silu(z)).astype(BF16))
        strips.append(jnp.concatenate(parts, axis=-1))
    return jnp.concatenate(strips, axis=0)


def _ffn_prologue(i, k, ug, pg, ng, uu, pu, nu, cwg, cwu, *, tm, nc, t):
    gates = _conv3_strips(ug, pg, ng, cwg, i, tm, nc, t)
    ups = _conv3_strips(uu, pu, nu, cwu, i, tm, nc, t)
    return jnp.concatenate([(_silu(g) * u).astype(BF16) for g, u in zip(gates, ups)], axis=0)


def _final_norm_kernel(x_ref, g_ref, o_ref):
    x = x_ref[...]
    ms = jnp.mean(x * x, axis=-1, keepdims=True)
    o_ref[...] = x * lax.rsqrt(ms + EPS) * g_ref[...]


def final_norm(x, g, nc):
    t, d = x.shape
    tm = math.gcd(nc, 256)
    off = nc // tm
    return pl.pallas_call(
        _final_norm_kernel,
        grid=((t - nc) // tm,),
        in_specs=[pl.BlockSpec((tm, d), lambda i: (i + off, 0)), pl.BlockSpec((1, d), lambda i: (0, 0))],
        out_specs=pl.BlockSpec((tm, d), lambda i: (i, 0)),
        out_shape=jax.ShapeDtypeStruct((t - nc, d), F32),
        compiler_params=_cp(("arbitrary",)),
        name="final_norm",
    )(x, g.reshape(1, d))


def _rope_tables(n, nc):
    rows = n // GRID_W
    row = jnp.broadcast_to(jnp.arange(rows)[:, None], (rows, GRID_W)).reshape(-1).astype(F32)
    col = jnp.broadcast_to(jnp.arange(GRID_W)[None, :], (rows, GRID_W)).reshape(-1).astype(F32)
    inv = ROPE_BASE ** (-jnp.arange(ROPE_AXIS_FREQS, dtype=F32) / ROPE_AXIS_FREQS)
    ar, ac = row[:, None] * inv, col[:, None] * inv
    cos = jnp.concatenate([jnp.cos(ar), jnp.cos(ar), jnp.cos(ac), jnp.cos(ac)], axis=-1)
    sin = jnp.concatenate([-jnp.sin(ar), jnp.sin(ar), -jnp.sin(ac), jnp.sin(ac)], axis=-1)
    cos = jnp.concatenate([jnp.ones((nc, LANES), F32), cos], axis=0)
    sin = jnp.concatenate([jnp.zeros((nc, LANES), F32), sin], axis=0)
    return cos, sin


def kernel(x, c, ctx, c_ctx, w_mod, b_mod, norm1_g, norm2_g, attn_w_in, diff_lambda, diff_subln_g, swa_sink,
           attn_w_out, dn_w_in, dn_conv_w, dn_a_log, dn_dt_bias, dn_norm_g, dn_w_out, ffn_w_up, ffn_conv_w,
           ffn_w_down, final_norm_g):
    assert x.shape[0] == 1, "single-sequence kernel"
    n, d = x.shape[1], x.shape[2]
    nc = ctx.shape[1]
    t = nc + n
    depth = w_mod.shape[0]
    xs = jnp.concatenate([ctx[0], x[0]], axis=0)
    cc = jnp.zeros((8, d), F32).at[0].set(c[0]).at[1].set(c_ctx)
    mods = mod_vectors(cc, w_mod, b_mod)
    cos, sin = _rope_tables(n, nc)

    da, sb_, skv = DIFF_HEADS * 2 * DIFF_DH, SWA_HEADS * SWA_DH, SWA_KV_HEADS * SWA_DH
    col_qa, col_ka, col_qb = 0, da, 2 * da
    col_kb = col_qb + sb_
    col_va = col_kb + skv
    col_vb = col_va + da
    n_attn = col_vb + skv
    cols = jnp.arange(n_attn)
    cmask = (cols < col_va).astype(F32).reshape(1, n_attn)
    cscale = jnp.where(cols < col_ka, DIFF_DH ** -0.5 * LOG2E,
                       jnp.where((cols >= col_qb) & (cols < col_kb), SWA_DH ** -0.5, 1.0))
    cscale = cscale.astype(F32).reshape(1, n_attn)

    tm_p = _pick(t, (640, 320, 256, 128))
    hidden = ffn_w_down.shape[1]
    tk_f = _pick(hidden, (1408, 512, 256, 128))
    qkv_cols = 2 * DN_QK_HEADS * DN_DK + DN_V_HEADS * DN_DV
    dn_out = DN_V_HEADS * DN_DV

    for layer in range(depth):
        m = mods[layer]
        sh1, sc1, g1, sh2, sc2, g2 = (m[:, j * d:(j + 1) * d] for j in range(6))
        i = layer // 2
        if layer % 2 == 0:
            lam_init = 0.8 - 0.6 * math.exp(-0.3 * layer)
            w = attn_w_in[i]
            w = jnp.concatenate([w[:, 0:2 * da], w[:, 3 * da:3 * da + sb_ + skv], w[:, 2 * da:3 * da],
                                 w[:, 3 * da + sb_ + skv:]], axis=1).astype(BF16)
            qkv = norm_mod_matmul(xs, norm1_g[layer], sh1, sc1, w, nc, rope_args=(cos, sin, cmask, cscale),
                                  tn_candidates=(512, 256, 128))
            oa = diff_attention(qkv, diff_lambda[i], diff_subln_g[i], nc, lam_init, col_qa, col_ka, col_va)
            ob = swa_attention(qkv, swa_sink[i], nc, col_qb, col_kb, col_vb)
            tk = da
            n_a = da // tk
            xs = proj_residual(
                functools.partial(_attn_out_prologue, n_a=n_a), [oa, ob],
                [pl.BlockSpec((tm_p, tk), lambda r, k: (r, jnp.minimum(k, n_a - 1))),
                 pl.BlockSpec((tm_p, tk), lambda r, k: (r, jnp.maximum(k - n_a, 0)))],
                attn_w_out[i].astype(BF16), xs, g1, nc, tm_p, tk)
        else:
            w = dn_w_in[i]
            u = norm_mod_matmul(xs, norm1_g[layer], sh1, sc1, w[:, :qkv_cols + dn_out].astype(BF16), nc)
            gates = norm_mod_matmul(xs, norm1_g[layer], sh1, sc1, w[:, qkv_cols + dn_out:].astype(BF16), nc, out_dtype=F32)
            qkvc = dn_conv(u, dn_conv_w[i], nc, qkv_cols)
            zero = jnp.zeros((2, DN_V_HEADS), F32)
            nega = jnp.stack([zero, -jnp.exp(dn_a_log[i].astype(F32))], axis=1).reshape(1, 4 * DN_V_HEADS)
            bias = jnp.stack([zero, dn_dt_bias[i].astype(F32)], axis=1).reshape(1, 4 * DN_V_HEADS)
            o2 = gated_delta(qkvc, gates, jnp.broadcast_to(nega, (8, LANES)),
                             jnp.broadcast_to(bias, (8, LANES)), nc)
            tk = 1024
            zoff = qkv_cols // tk
            xs = proj_residual(
                functools.partial(_dn_out_prologue, tk=tk), [o2, o2, u, dn_norm_g[i].reshape(1, DN_DV)],
                [pl.BlockSpec((1, tm_p, tk), lambda r, k: (0, r, k)),
                 pl.BlockSpec((1, tm_p, tk), lambda r, k: (1, r, k)),
                 pl.BlockSpec((tm_p, tk), lambda r, k: (r, zoff + k)),
                 pl.BlockSpec((1, DN_DV), lambda r, k: (0, 0))],
                dn_w_out[i].astype(BF16), xs, g1, nc, tm_p, tk)
        uf = norm_mod_matmul(xs, norm2_g[layer], sh2, sc2, ffn_w_up[layer].astype(BF16), nc)
        cw = ffn_conv_w[layer]
        xs = proj_residual(
            functools.partial(_ffn_prologue, tm=tm_p, nc=nc, t=t), [uf, uf, uf, uf, uf, uf, cw, cw],
            _halo_specs(tm_p, tk_f, t, 0) + _halo_specs(tm_p, tk_f, t, hidden // tk_f)
            + [pl.BlockSpec((3, tk_f), lambda r, k: (0, k)),
               pl.BlockSpec((3, tk_f), lambda r, k: (0, hidden // tk_f + k))],
            ffn_w_down[layer].astype(BF16), xs, g2, nc, tm_p, tk_f)
    return final_norm(xs, final_norm_g, nc)[None]
```

```python
import functools
import math

import jax
import jax.numpy as jnp
from jax import lax
from jax.experimental import pallas as pl
from jax.experimental.pallas import tpu as pltpu

F32 = jnp.float32
BF16 = jnp.bfloat16

EPS = 1e-6
NEG_INF = -1e30
GRID_W = 64
ROPE_BASE = 10000.0
ROPE_AXIS_FREQS = 32
DIFF_HEADS = 4
DIFF_DH = 128
SWA_HEADS = 8
SWA_KV_HEADS = 2
SWA_GROUP = SWA_HEADS // SWA_KV_HEADS
SWA_DH = 128
WINDOW = 128
DN_QK_HEADS = 16
DN_V_HEADS = 32
DN_DK = 128
DN_DV = 128
DN_CHUNK = 64
LANES = 128
HALO = 16
VMEM_LIMIT = 56 * 1024 * 1024


def _cp(sem):
    return pltpu.CompilerParams(dimension_semantics=sem, vmem_limit_bytes=VMEM_LIMIT)


def _pick(n, candidates):
    for c in candidates:
        if n % c == 0:
            return c
    raise ValueError(f"no tile for {n} among {candidates}")


def _silu(x):
    return x * jax.nn.sigmoid(x)


def _rows(i, tm):
    return i * tm + lax.broadcasted_iota(jnp.int32, (tm, 1), 0)


def _mod_kernel(a_ref, w_ref, b_ref, o_ref):
    a = _silu(a_ref[...]).astype(BF16)
    w = w_ref[0].astype(BF16)
    o_ref[0] = jnp.dot(a, w, preferred_element_type=F32) + b_ref[0]


def mod_vectors(cc, w_mod, b_mod):
    depth, d, n = w_mod.shape
    tn = _pick(n, (1024, 512, 256, 128))
    return pl.pallas_call(
        _mod_kernel,
        grid=(depth, n // tn),
        in_specs=[pl.BlockSpec((8, d), lambda l, j: (0, 0)),
                  pl.BlockSpec((1, d, tn), lambda l, j: (l, 0, j)),
                  pl.BlockSpec((1, 1, tn), lambda l, j: (l, 0, j))],
        out_specs=pl.BlockSpec((1, 8, tn), lambda l, j: (l, 0, j)),
        out_shape=jax.ShapeDtypeStruct((depth, 8, n), F32),
        compiler_params=_cp(("arbitrary", "arbitrary")),
        name="mod_vectors",
    )(cc, w_mod, b_mod.reshape(depth, 1, n))


def _nmm_kernel(x_ref, g_ref, sh_ref, sc_ref, w_ref, *rest, tm, tn, nc, rope):
    if rope:
        cos_ref, sin_ref, cmask_ref, cscale_ref, o_ref, h_ref = rest
    else:
        o_ref, h_ref = rest
    i = pl.program_id(0)
    j = pl.program_id(1)

    @pl.when(j == 0)
    def _():
        sub = math.gcd(math.gcd(tm, 32), nc)
        gain = g_ref[...] * (1.0 + sc_ref[0:2, :])
        for r0 in range(0, tm, sub):
            x = x_ref[r0:r0 + sub, :]
            ms = jnp.mean(x * x, axis=-1, keepdims=True)
            is_ctx = i * tm + r0 < nc
            gs = jnp.where(is_ctx, gain[1:2, :], gain[0:1, :])
            sh = jnp.where(is_ctx, sh_ref[1:2, :], sh_ref[0:1, :])
            h_ref[r0:r0 + sub, :] = (x * lax.rsqrt(ms + EPS) * gs + sh).astype(BF16)

    acc = jnp.dot(h_ref[...], w_ref[...], preferred_element_type=F32)
    if not rope:
        o_ref[...] = acc.astype(o_ref.dtype)
        return
    rs = math.gcd(tm, 64)
    lane = lax.broadcasted_iota(jnp.int32, (rs, LANES), 1)
    first = (lane % (2 * ROPE_AXIS_FREQS)) < ROPE_AXIS_FREQS
    for r0 in range(0, tm, rs):
        cos = cos_ref[r0:r0 + rs, :]
        sin = sin_ref[r0:r0 + rs, :]
        for c0 in range(0, tn, LANES):
            a = acc[r0:r0 + rs, c0:c0 + LANES]
            partner = jnp.where(first, pltpu.roll(a, LANES - ROPE_AXIS_FREQS, 1), pltpu.roll(a, ROPE_AXIS_FREQS, 1))
            roped = a * cos + partner * sin
            a = jnp.where(cmask_ref[:, c0:c0 + LANES] > 0.5, roped, a) * cscale_ref[:, c0:c0 + LANES]
            o_ref[r0:r0 + rs, c0:c0 + LANES] = a.astype(o_ref.dtype)


def norm_mod_matmul(x, g, sh, sc, w, nc, rope_args=None, tn_candidates=(1024, 512, 256, 128), out_dtype=BF16):
    t, d = x.shape
    n = w.shape[1]
    tm = _pick(t, (1280, 640, 512, 320, 256, 128))
    tn = _pick(n, tn_candidates)
    rope = rope_args is not None
    in_specs = [pl.BlockSpec((tm, d), lambda i, j: (i, 0)),
                pl.BlockSpec((1, d), lambda i, j: (0, 0)),
                pl.BlockSpec((8, d), lambda i, j: (0, 0)),
                pl.BlockSpec((8, d), lambda i, j: (0, 0)),
                pl.BlockSpec((d, tn), lambda i, j: (0, j))]
    args = [x, g.reshape(1, d), sh, sc, w]
    if rope:
        cos, sin, cmask, cscale = rope_args
        in_specs += [pl.BlockSpec((tm, LANES), lambda i, j: (i, 0)),
                     pl.BlockSpec((tm, LANES), lambda i, j: (i, 0)),
                     pl.BlockSpec((1, tn), lambda i, j: (0, j)),
                     pl.BlockSpec((1, tn), lambda i, j: (0, j))]
        args += [cos, sin, cmask, cscale]
    return pl.pallas_call(
        functools.partial(_nmm_kernel, tm=tm, tn=tn, nc=nc, rope=rope),
        grid=(t // tm, n // tn),
        in_specs=in_specs,
        out_specs=pl.BlockSpec((tm, tn), lambda i, j: (i, j)),
        out_shape=jax.ShapeDtypeStruct((t, n), out_dtype),
        scratch_shapes=[pltpu.VMEM((tm, d), BF16)],
        compiler_params=_cp(("arbitrary", "arbitrary")),
        name="norm_mod_matmul",
    )(*args)


def _proj_res_kernel(*refs, prologue, n_lhs, nk, tm, nc):
    lhs = refs[:n_lhs]
    w_ref, x_ref, gate_ref, o_ref, acc_ref = refs[n_lhs:]
    i = pl.program_id(0)
    k = pl.program_id(1)
    a = prologue(i, k, *lhs)
    p = jnp.dot(a, w_ref[...], preferred_element_type=F32)

    @pl.when(k == 0)
    def _():
        acc_ref[...] = p

    @pl.when(k > 0)
    def _():
        acc_ref[...] += p

    @pl.when(k == nk - 1)
    def _():
        is_ctx = _rows(i, tm) < nc
        gate = jnp.where(is_ctx, gate_ref[1:2, :], gate_ref[0:1, :])
        o_ref[...] = x_ref[...] + gate * acc_ref[...]


def proj_residual(prologue, lhs_args, lhs_specs, w, x, gate, nc, tm, tk):
    t, d = x.shape
    kdim = w.shape[0]
    nk = kdim // tk
    in_specs = list(lhs_specs) + [pl.BlockSpec((tk, d), lambda i, k: (k, 0)),
                                  pl.BlockSpec((tm, d), lambda i, k: (i, 0)),
                                  pl.BlockSpec((8, d), lambda i, k: (0, 0))]
    return pl.pallas_call(
        functools.partial(_proj_res_kernel, prologue=prologue, n_lhs=len(lhs_args), nk=nk, tm=tm, nc=nc),
        grid=(t // tm, nk),
        in_specs=in_specs,
        out_specs=pl.BlockSpec((tm, d), lambda i, k: (i, 0)),
        out_shape=jax.ShapeDtypeStruct((t, d), F32),
        scratch_shapes=[pltpu.VMEM((tm, d), F32)],
        compiler_params=_cp(("arbitrary", "arbitrary")),
        name="proj_residual",
    )(*lhs_args, w, x, gate)


def _halo_specs(tm, tk, t, col_off):
    nh = t // HALO
    per = tm // HALO
    return [pl.BlockSpec((tm, tk), lambda i, k: (i, col_off + k)),
            pl.BlockSpec((HALO, tk), lambda i, k: (jnp.maximum(i * per - 1, 0), col_off + k)),
            pl.BlockSpec((HALO, tk), lambda i, k: (jnp.minimum((i + 1) * per, nh - 1), col_off + k))]


CONV_STRIP = 32
SUBLANES = 8


def _patch_rows(x, g0, patch):
    parts = [x[:g0], patch(x[g0:g0 + SUBLANES]), x[g0 + SUBLANES:]]
    return jnp.concatenate([p for p in parts if p.shape[0]], axis=0)


def _conv3_strips(u_ref, prev_ref, next_ref, cw_ref, i, tm, nc, t):
    s = CONV_STRIP
    sub = lax.broadcasted_iota(jnp.int32, (SUBLANES, 1), 0)
    w = cw_ref[...]
    w0, w1, w2 = w[0:1, :], w[1:2, :], w[2:3, :]
    halo_prev = jnp.where(i == 0, 0.0, prev_ref[...].astype(F32)[HALO - 1:HALO, :])
    halo_next = jnp.where(i == t // tm - 1, 0.0, next_ref[...].astype(F32)[0:1, :])
    n = tm // s
    strips = [u_ref[j * s:(j + 1) * s, :].astype(F32) for j in range(n)]
    first_blk, first_row = nc // tm, nc % tm
    last_blk, last_row = (nc - 1) // tm, (nc - 1) % tm
    for j in range(n):
        u = strips[j]
        before = halo_prev if j == 0 else strips[j - 1][s - 1:s, :]
        after = halo_next if j == n - 1 else strips[j + 1][0:1, :]
        up = _patch_rows(pltpu.roll(u, 1, 0), 0, lambda g: jnp.where(sub == 0, before, g))
        un = _patch_rows(pltpu.roll(u, s - 1, 0), s - SUBLANES, lambda g: jnp.where(sub == SUBLANES - 1, after, g))
        if 0 < nc < t and first_row // s == j:
            r = first_row % s
            up = _patch_rows(up, r // SUBLANES * SUBLANES,
                             lambda g: jnp.where((sub == r % SUBLANES) & (i == first_blk), 0.0, g))
        if 0 < nc < t and last_row // s == j:
            r = last_row % s
            un = _patch_rows(un, r // SUBLANES * SUBLANES,
                             lambda g: jnp.where((sub == r % SUBLANES) & (i == last_blk), 0.0, g))
        yield up * w0 + u * w1 + un * w2


ATTN_STRIP = 32
LOG2E = math.log2(math.e)


def _diff_attn_kernel(lam_ref, g_ref, q_ref, k_ref, v_ref, o_ref, m_ref, l_ref, acc_ref, p_ref, alpha_ref,
                      *, tq, tk, nc, nk, lam_init):
    i = pl.program_id(1)
    kk = pl.program_id(2)

    @pl.when(kk == 0)
    def _():
        m_ref[...] = jnp.full(m_ref.shape, NEG_INF, F32)
        l_ref[...] = jnp.zeros(l_ref.shape, F32)
        acc_ref[...] = jnp.zeros(acc_ref.shape, F32)

    def step(masked):
        nt = (((1,), (1,)), ((), ()))
        v = v_ref[...]
        for c in range(2):
            cols = slice(c * DIFF_DH, (c + 1) * DIFF_DH)
            s_all = lax.dot_general(q_ref[:, cols], k_ref[:, cols], nt, preferred_element_type=F32)
            for r0 in range(0, tq, ATTN_STRIP):
                rows = slice(r0, r0 + ATTN_STRIP)
                s = s_all[rows, :]
                if masked:
                    qrow = i * tq + r0 + lax.broadcasted_iota(jnp.int32, (ATTN_STRIP, 1), 0)
                    kcol = kk * tk + lax.broadcasted_iota(jnp.int32, (1, tk), 1)
                    s = jnp.where((qrow >= nc) | (kcol < nc), s, NEG_INF)
                m_prev = m_ref[c, rows, :]
                m_new = jnp.maximum(m_prev, jnp.max(s, axis=-1, keepdims=True))
                alpha = jnp.exp2(m_prev - m_new)
                p = jnp.exp2(s - m_new)
                l_ref[c, rows, :] = alpha * l_ref[c, rows, :] + jnp.sum(p, axis=-1, keepdims=True)
                m_ref[c, rows, :] = m_new
                alpha_ref[c, rows, :] = alpha
                p_ref[c, rows, :] = p.astype(BF16)
            acc_ref[c] = alpha_ref[c] * acc_ref[c] + jnp.dot(p_ref[c], v, preferred_element_type=F32)

    has_ctx_rows = i * tq < nc

    @pl.when(has_ctx_rows)
    def _():
        step(True)

    @pl.when(jnp.logical_not(has_ctx_rows))
    def _():
        step(False)

    @pl.when(kk == nk - 1)
    def _():
        lf = lam_ref[...]
        lam = (jnp.exp(jnp.sum(lf[0:1] * lf[1:2], axis=-1, keepdims=True))
               - jnp.exp(jnp.sum(lf[2:3] * lf[3:4], axis=-1, keepdims=True)) + lam_init)
        o = acc_ref[0] / l_ref[0] - lam * (acc_ref[1] / l_ref[1])
        ms = jnp.mean(o * o, axis=-1, keepdims=True)
        o_ref[...] = (o * lax.rsqrt(ms + EPS) * g_ref[...] * (1.0 - lam_init)).astype(o_ref.dtype)


def diff_attention(qkv, lam_vec, subln_g, nc, lam_init, col_q, col_k, col_v):
    t = qkv.shape[0]
    tq = _pick(t, (1280, 640, 256, 128))
    tk = _pick(t, (1280, 256, 128))
    nk = t // tk
    hw = 2 * DIFF_DH
    return pl.pallas_call(
        functools.partial(_diff_attn_kernel, tq=tq, tk=tk, nc=nc, nk=nk, lam_init=lam_init),
        grid=(DIFF_HEADS, t // tq, nk),
        in_specs=[pl.BlockSpec((4, DIFF_DH), lambda h, i, k: (0, 0)),
                  pl.BlockSpec((1, hw), lambda h, i, k: (0, 0)),
                  pl.BlockSpec((tq, hw), lambda h, i, k: (i, col_q // hw + h)),
                  pl.BlockSpec((tk, hw), lambda h, i, k: (k, col_k // hw + h)),
                  pl.BlockSpec((tk, hw), lambda h, i, k: (k, col_v // hw + h))],
        out_specs=pl.BlockSpec((tq, hw), lambda h, i, k: (i, h)),
        out_shape=jax.ShapeDtypeStruct((t, DIFF_HEADS * hw), BF16),
        scratch_shapes=[pltpu.VMEM((2, tq, 1), F32), pltpu.VMEM((2, tq, 1), F32), pltpu.VMEM((2, tq, hw), F32),
                        pltpu.VMEM((2, tq, tk), BF16), pltpu.VMEM((2, tq, 1), F32)],
        compiler_params=_cp(("arbitrary", "arbitrary", "arbitrary")),
        name="diff_attention",
    )(lam_vec, subln_g.reshape(1, hw), qkv, qkv, qkv)


def _swa_kernel(sink_ref, q_ref, kc_ref, k0_ref, k1_ref, k2_ref, vc_ref, v0_ref, v1_ref, v2_ref, o_ref,
                *, nc, t):
    r = pl.program_id(1)
    qb = WINDOW
    q = jnp.concatenate([q_ref[:, g * SWA_DH:(g + 1) * SWA_DH] for g in range(SWA_GROUP)], axis=0)
    sk = sink_ref[0]
    sink = jnp.concatenate([jnp.broadcast_to(sk[g:g + 1, 0:1], (qb, 1)) for g in range(SWA_GROUP)], axis=0)
    nt = (((1,), (1,)), ((), ()))
    local_q = lax.broadcasted_iota(jnp.int32, (qb, 1), 0)
    qrow = jnp.concatenate([r * qb + local_q] * SWA_GROUP, axis=0)
    s_list = [lax.dot_general(q, kc_ref[...], nt, preferred_element_type=F32)]
    for w, k_ref in zip((-1, 0, 1), (k0_ref, k1_ref, k2_ref)):
        s = lax.dot_general(q, k_ref[...], nt, preferred_element_type=F32)
        krow = (r + w) * qb + lax.broadcasted_iota(jnp.int32, (1, qb), 1)
        valid = (jnp.abs(qrow - krow) <= WINDOW) & (krow >= nc) & (krow < t) & (qrow >= nc)
        s_list.append(jnp.where(valid, s, NEG_INF))
    m = sink
    for s in s_list:
        m = jnp.maximum(m, jnp.max(s, axis=-1, keepdims=True))
    denom = jnp.exp(sink - m)
    out = jnp.zeros((SWA_GROUP * qb, SWA_DH), F32)
    for s, v_ref in zip(s_list, (vc_ref, v0_ref, v1_ref, v2_ref)):
        e = jnp.exp(s - m)
        denom = denom + jnp.sum(e, axis=-1, keepdims=True)
        out = out + jnp.dot(e.astype(BF16), v_ref[...], preferred_element_type=F32)
    out = out / denom
    for g in range(SWA_GROUP):
        o_ref[:, g * SWA_DH:(g + 1) * SWA_DH] = out[g * qb:(g + 1) * qb, :].astype(o_ref.dtype)


def swa_attention(qkv, sink, nc, col_q, col_k, col_v):
    t = qkv.shape[0]
    qb = WINDOW
    nb = t // qb
    gw = SWA_GROUP * SWA_DH
    sink_b = jnp.broadcast_to(sink.astype(F32).reshape(SWA_KV_HEADS, SWA_GROUP, 1), (SWA_KV_HEADS, SWA_GROUP, LANES))
    kcb, vcb = col_k // SWA_DH, col_v // SWA_DH

    def win(cb, w):
        return pl.BlockSpec((qb, SWA_DH), lambda kv, r: (jnp.clip(r + w, 0, nb - 1), cb + kv))

    return pl.pallas_call(
        functools.partial(_swa_kernel, nc=nc, t=t),
        grid=(SWA_KV_HEADS, nb),
        in_specs=[pl.BlockSpec((1, SWA_GROUP, LANES), lambda kv, r: (kv, 0, 0)),
                  pl.BlockSpec((qb, gw), lambda kv, r: (r, col_q // gw + kv)),
                  pl.BlockSpec((nc, SWA_DH), lambda kv, r: (0, kcb + kv)),
                  win(kcb, -1), win(kcb, 0), win(kcb, 1),
                  pl.BlockSpec((nc, SWA_DH), lambda kv, r: (0, vcb + kv)),
                  win(vcb, -1), win(vcb, 0), win(vcb, 1)],
        out_specs=pl.BlockSpec((qb, gw), lambda kv, r: (r, kv)),
        out_shape=jax.ShapeDtypeStruct((t, SWA_HEADS * SWA_DH), BF16),
        compiler_params=_cp(("arbitrary", "arbitrary")),
        name="swa_attention",
    )(sink_b, qkv, qkv, qkv, qkv, qkv, qkv, qkv, qkv, qkv)


def _dn_conv_kernel(u_ref, prev_ref, next_ref, cw_ref, o_ref, *, tm, tc, nc, t, n_q, n_qk):
    i = pl.program_id(0)
    j = pl.program_id(1)
    s = CONV_STRIP

    @pl.when(j < n_qk)
    def _():
        scale = jnp.where(j < n_q, DN_DK ** -0.5, 1.0).astype(F32)
        for n, conv in enumerate(_conv3_strips(u_ref, prev_ref, next_ref, cw_ref, i, tm, nc, t)):
            y = _silu(conv)
            for c0 in range(0, tc, DN_DK):
                a = y[:, c0:c0 + DN_DK]
                ss = jnp.sum(a * a, axis=-1, keepdims=True)
                o_ref[n * s:(n + 1) * s, c0:c0 + DN_DK] = (a * lax.rsqrt(ss + EPS) * scale).astype(o_ref.dtype)

    @pl.when(j >= n_qk)
    def _():
        for n, conv in enumerate(_conv3_strips(u_ref, prev_ref, next_ref, cw_ref, i, tm, nc, t)):
            o_ref[n * s:(n + 1) * s, :] = _silu(conv).astype(o_ref.dtype)


def dn_conv(u, conv_w, nc, n_cols):
    t = u.shape[0]
    tm = _pick(t, (640, 320, 256, 128))
    tc = 1024
    qk_cols = DN_QK_HEADS * DN_DK
    return pl.pallas_call(
        functools.partial(_dn_conv_kernel, tm=tm, tc=tc, nc=nc, t=t, n_q=qk_cols // tc, n_qk=2 * qk_cols // tc),
        grid=(t // tm, n_cols // tc),
        in_specs=_halo_specs(tm, tc, t, 0) + [pl.BlockSpec((3, tc), lambda i, k: (0, k))],
        out_specs=pl.BlockSpec((tm, tc), lambda i, k: (i, k)),
        out_shape=jax.ShapeDtypeStruct((t, n_cols), BF16),
        compiler_params=_cp(("arbitrary", "arbitrary")),
        name="dn_conv",
    )(u, u, u, conv_w)


DN_GQ = 16


def _split3(x):
    hi = x.astype(BF16)
    r1 = x - hi.astype(F32)
    mid = r1.astype(BF16)
    lo = (r1 - mid.astype(F32)).astype(BF16)
    return hi, mid, lo


def _delta_kernel(gates_ref, nega_ref, bias_ref, q_ref, k_ref, v_ref, o_ref, s_ref):
    d = pl.program_id(0)
    hg = pl.program_id(1)
    s_idx = pl.program_id(2)
    c = DN_CHUNK
    nt = (((1,), (1,)), ((), ()))
    tn = (((0,), (0,)), ((), ()))

    @pl.when(s_idx == 0)
    def _():
        s_ref[...] = jnp.zeros(s_ref.shape, F32)

    gt = gates_ref[...]
    shift = (2 * LANES - d * (2 * DN_V_HEADS) - hg * (2 * DN_GQ)) % LANES
    gt = pltpu.roll(gt, shift, 1)
    nega = pltpu.roll(nega_ref[...], shift, 1)
    bias = pltpu.roll(bias_ref[...], shift, 1)
    beta_all = jax.nn.sigmoid(gt)
    z = gt + bias[0:1, :]
    softplus = jnp.maximum(z, 0.0) + jnp.log(1.0 + jnp.exp(-jnp.abs(z)))
    g_all = nega[0:1, :] * softplus

    ri = lax.broadcasted_iota(jnp.int32, (c, c), 0)
    ci = lax.broadcasted_iota(jnp.int32, (c, c), 1)
    order = (ri - ci) * jnp.where(d == 0, 1, -1)
    incl = order >= 0
    strict = order > 0
    eye = ri == ci
    eye_f = eye.astype(F32)
    csum = incl.astype(BF16)
    hi, mid, lo = _split3(g_all)
    gam_all = (jnp.dot(csum, hi, preferred_element_type=F32) + jnp.dot(csum, mid, preferred_element_type=F32)
               + jnp.dot(csum, lo, preferred_element_type=F32))
    tot_all = jnp.sum(g_all, axis=0, keepdims=True)

    def dot(x, y):
        return jnp.dot(x, y, preferred_element_type=F32)

    hqs = range(DN_GQ)
    hvs = range(2 * DN_GQ)
    kb = [k_ref[:, h * DN_DK:(h + 1) * DN_DK] for h in hqs]
    qb = [q_ref[:, h * DN_DK:(h + 1) * DN_DK] for h in hqs]
    kq = [lax.dot_general(jnp.concatenate([kb[h], qb[h]], axis=0), kb[h], nt, preferred_element_type=F32)
          for h in hqs]
    kf = [x.astype(F32) for x in kb]
    qf = [x.astype(F32) for x in qb]
    beta = [beta_all[:, hv:hv + 1] for hv in hvs]
    gam = [gam_all[:, DN_V_HEADS + hv:DN_V_HEADS + hv + 1] for hv in hvs]
    tot = [tot_all[:, DN_V_HEADS + hv:DN_V_HEADS + hv + 1] for hv in hvs]
    gam_row = [jnp.sum(jnp.where(eye, g, 0.0), axis=0, keepdims=True) for g in gam]
    dec = [jnp.where(incl, jnp.exp(jnp.where(incl, g - gr, 0.0)), 0.0) for g, gr in zip(gam, gam_row)]
    a = [jnp.where(strict, beta[hv] * kq[hv // 2][:c] * dec[hv], 0.0) for hv in hvs]
    pw = [-x for x in a]
    tinv = [eye_f + p for p in pw]
    for _ in range(int(math.log2(c)) - 2):
        pwb = [p.astype(BF16) for p in pw]
        pw = [dot(p, p) for p in pwb]
        tinv = [t + dot(t.astype(BF16), p.astype(BF16)) for t, p in zip(tinv, pw)]
    tb = [t.astype(BF16) for t in tinv]
    resid = [eye_f - t - dot(x.astype(BF16), t_b) for t, x, t_b in zip(tinv, a, tb)]
    tinv = [t + dot(t_b, r.astype(BF16)) for t, t_b, r in zip(tinv, tb, resid)]
    tb = [t.astype(BF16) for t in tinv]
    eg = [jnp.exp(g) for g in gam]
    vf = [v_ref[:, hv * DN_DV:(hv + 1) * DN_DV].astype(F32) for hv in hvs]
    rhs = [jnp.concatenate([kf[hv // 2] * (beta[hv] * eg[hv]), vf[hv] * beta[hv]], axis=1).astype(BF16) for hv in hvs]
    sol = [dot(t_b, r) for t_b, r in zip(tb, rhs)]
    aqk = [jnp.where(incl, kq[hv // 2][c:] * dec[hv], 0.0).astype(BF16) for hv in hvs]
    wq = [jnp.concatenate([sol[hv][:, :DN_DK], qf[hv // 2] * eg[hv]], axis=0).astype(BF16) for hv in hvs]
    kd = [(kf[hv // 2] * jnp.exp(tot[hv] - gam[hv])).astype(BF16) for hv in hvs]
    state = [s_ref[hv] for hv in hvs]
    ws = [dot(x, s.astype(BF16)) for x, s in zip(wq, state)]
    vnb = [(sol[hv][:, DN_DK:] - ws[hv][:c]).astype(BF16) for hv in hvs]
    o = [ws[hv][c:] + dot(aqk[hv], vnb[hv]) for hv in hvs]
    upd = [lax.dot_general(kd[hv], vnb[hv], tn, preferred_element_type=F32) for hv in hvs]
    for hv in hvs:
        s_ref[hv] = jnp.exp(tot[hv]) * state[hv] + upd[hv]
        o_ref[0, :, hv * DN_DV:(hv + 1) * DN_DV] = o[hv].astype(o_ref.dtype)


def gated_delta(qkvc, gates, nega, bias, nc):
    t = qkvc.shape[0]
    c = DN_CHUNK
    nch = t // c
    ncc = nc // c
    qw = DN_GQ * DN_DK
    vw = 2 * DN_GQ * DN_DV
    qk_cols = DN_QK_HEADS * DN_DK

    def chunk(d, s):
        rev = jnp.where(s < ncc, ncc - 1 - s, nch - 1 - (s - ncc))
        return jnp.where(d == 0, s, rev)

    return pl.pallas_call(
        _delta_kernel,
        grid=(2, DN_QK_HEADS // DN_GQ, nch),
        in_specs=[pl.BlockSpec((c, LANES), lambda d, h, s: (chunk(d, s), 0)),
                  pl.BlockSpec((8, LANES), lambda d, h, s: (0, 0)),
                  pl.BlockSpec((8, LANES), lambda d, h, s: (0, 0)),
                  pl.BlockSpec((c, qw), lambda d, h, s: (chunk(d, s), h)),
                  pl.BlockSpec((c, qw), lambda d, h, s: (chunk(d, s), qk_cols // qw + h)),
                  pl.BlockSpec((c, vw), lambda d, h, s: (chunk(d, s), 2 * qk_cols // vw + h))],
        out_specs=pl.BlockSpec((1, c, vw), lambda d, h, s: (d, chunk(d, s), h)),
        out_shape=jax.ShapeDtypeStruct((2, t, DN_V_HEADS * DN_DV), BF16),
        scratch_shapes=[pltpu.VMEM((2 * DN_GQ, DN_DK, DN_DV), F32)],
        compiler_params=_cp(("arbitrary", "arbitrary", "arbitrary")),
        name="gated_delta",
    )(gates, nega, bias, qkvc, qkvc, qkvc)


def _attn_out_prologue(i, k, oa_ref, ob_ref, *, n_a):
    return jnp.where(k < n_a, oa_ref[...], ob_ref[...])


def _dn_out_prologue(i, k, of_ref, or_ref, z_ref, g_ref, *, tk):
    tm = z_ref.shape[0]
    rs = math.gcd(tm, 16)
    strips = []
    for r0 in range(0, tm, rs):
        parts = []
        for c0 in range(0, tk, DN_DV):
            a = (of_ref[0, r0:r0 + rs, c0:c0 + DN_DV].astype(F32) + or_ref[0, r0:r0 + rs, c0:c0 + DN_DV].astype(F32))
            ms = jnp.mean(a * a, axis=-1, keepdims=True)
            z = z_ref[r0:r0 + rs, c0:c0 + DN_DV].astype(F32)
            parts.append((a * lax.rsqrt(ms + EPS) * g_ref[...] * _silu(z)).astype(BF16))
        strips.append(jnp.concatenate(parts, axis=-1))
    return jnp.concatenate(strips, axis=0)


def _ffn_prologue(i, k, ug, pg, ng, uu, pu, nu, cwg, cwu, *, tm, nc, t):
    gates = _conv3_strips(ug, pg, ng, cwg, i, tm, nc, t)
    ups = _conv3_strips(uu, pu, nu, cwu, i, tm, nc, t)
    return jnp.concatenate([(_silu(g) * u).astype(BF16) for g, u in zip(gates, ups)], axis=0)


def _final_norm_kernel(x_ref, g_ref, o_ref):
    x = x_ref[...]
    ms = jnp.mean(x * x, axis=-1, keepdims=True)
    o_ref[...] = x * lax.rsqrt(ms + EPS) * g_ref[...]


def final_norm(x, g, nc):
    t, d = x.shape
    tm = math.gcd(nc, 256)
    off = nc // tm
    return pl.pallas_call(
        _final_norm_kernel,
        grid=((t - nc) // tm,),
        in_specs=[pl.BlockSpec((tm, d), lambda i: (i + off, 0)), pl.BlockSpec((1, d), lambda i: (0, 0))],
        out_specs=pl.BlockSpec((tm, d), lambda i: (i, 0)),
        out_shape=jax.ShapeDtypeStruct((t - nc, d), F32),
        compiler_params=_cp(("arbitrary",)),
        name="final_norm",
    )(x, g.reshape(1, d))


def _rope_tables(n, nc):
    rows = n // GRID_W
    row = jnp.broadcast_to(jnp.arange(rows)[:, None], (rows, GRID_W)).reshape(-1).astype(F32)
    col = jnp.broadcast_to(jnp.arange(GRID_W)[None, :], (rows, GRID_W)).reshape(-1).astype(F32)
    inv = ROPE_BASE ** (-jnp.arange(ROPE_AXIS_FREQS, dtype=F32) / ROPE_AXIS_FREQS)
    ar, ac = row[:, None] * inv, col[:, None] * inv
    cos = jnp.concatenate([jnp.cos(ar), jnp.cos(ar), jnp.cos(ac), jnp.cos(ac)], axis=-1)
    sin = jnp.concatenate([-jnp.sin(ar), jnp.sin(ar), -jnp.sin(ac), jnp.sin(ac)], axis=-1)
    cos = jnp.concatenate([jnp.ones((nc, LANES), F32), cos], axis=0)
    sin = jnp.concatenate([jnp.zeros((nc, LANES), F32), sin], axis=0)
    return cos, sin


def kernel(x, c, ctx, c_ctx, w_mod, b_mod, norm1_g, norm2_g, attn_w_in, diff_lambda, diff_subln_g, swa_sink,
           attn_w_out, dn_w_in, dn_conv_w, dn_a_log, dn_dt_bias, dn_norm_g, dn_w_out, ffn_w_up, ffn_conv_w,
           ffn_w_down, final_norm_g):
    assert x.shape[0] == 1, "single-sequence kernel"
    n, d = x.shape[1], x.shape[2]
    nc = ctx.shape[1]
    t = nc + n
    depth = w_mod.shape[0]
    xs = jnp.concatenate([ctx[0], x[0]], axis=0)
    cc = jnp.zeros((8, d), F32).at[0].set(c[0]).at[1].set(c_ctx)
    mods = mod_vectors(cc, w_mod, b_mod)
    cos, sin = _rope_tables(n, nc)

    da, sb_, skv = DIFF_HEADS * 2 * DIFF_DH, SWA_HEADS * SWA_DH, SWA_KV_HEADS * SWA_DH
    col_qa, col_ka, col_qb = 0, da, 2 * da
    col_va = col_qb + sb_
    col_kb = col_va + da
    col_vb = col_kb + skv
    n_attn = col_vb + skv
    cols = jnp.arange(n_attn)
    cmask = ((cols < col_va) | ((cols >= col_kb) & (cols < col_vb))).astype(F32).reshape(1, n_attn)
    cscale = jnp.where(cols < col_ka, DIFF_DH ** -0.5 * LOG2E,
                       jnp.where((cols >= col_qb) & (cols < col_va), SWA_DH ** -0.5, 1.0))
    cscale = cscale.astype(F32).reshape(1, n_attn)

    tm_p = _pick(t, (640, 320, 256, 128))
    hidden = ffn_w_down.shape[1]
    tk_f = _pick(hidden, (1408, 512, 256, 128))
    qkv_cols = 2 * DN_QK_HEADS * DN_DK + DN_V_HEADS * DN_DV
    dn_out = DN_V_HEADS * DN_DV

    for layer in range(depth):
        m = mods[layer]
        sh1, sc1, g1, sh2, sc2, g2 = (m[:, j * d:(j + 1) * d] for j in range(6))
        i = layer // 2
        if layer % 2 == 0:
            lam_init = 0.8 - 0.6 * math.exp(-0.3 * layer)
            w = attn_w_in[i]
            w = jnp.concatenate([w[:, 0:2 * da], w[:, 3 * da:3 * da + sb_], w[:, 2 * da:3 * da],
                                 w[:, 3 * da + sb_:]], axis=1).astype(BF16)
            qkv = norm_mod_matmul(xs, norm1_g[layer], sh1, sc1, w, nc, rope_args=(cos, sin, cmask, cscale),
                                  tn_candidates=(512, 256, 128))
            oa = diff_attention(qkv, diff_lambda[i], diff_subln_g[i], nc, lam_init, col_qa, col_ka, col_va)
            ob = swa_attention(qkv, swa_sink[i], nc, col_qb, col_kb, col_vb)
            tk = da
            n_a = da // tk
            xs = proj_residual(
                functools.partial(_attn_out_prologue, n_a=n_a), [oa, ob],
                [pl.BlockSpec((tm_p, tk), lambda r, k: (r, jnp.minimum(k, n_a - 1))),
                 pl.BlockSpec((tm_p, tk), lambda r, k: (r, jnp.maximum(k - n_a, 0)))],
                attn_w_out[i].astype(BF16), xs, g1, nc, tm_p, tk)
        else:
            w = dn_w_in[i]
            u = norm_mod_matmul(xs, norm1_g[layer], sh1, sc1, w[:, :qkv_cols + dn_out].astype(BF16), nc)
            gates = norm_mod_matmul(xs, norm1_g[layer], sh1, sc1, w[:, qkv_cols + dn_out:].astype(BF16), nc, out_dtype=F32)
            qkvc = dn_conv(u, dn_conv_w[i], nc, qkv_cols)
            zero = jnp.zeros((2, DN_V_HEADS), F32)
            nega = jnp.stack([zero, -jnp.exp(dn_a_log[i].astype(F32))], axis=1).reshape(1, 4 * DN_V_HEADS)
            bias = jnp.stack([zero, dn_dt_bias[i].astype(F32)], axis=1).reshape(1, 4 * DN_V_HEADS)
            o2 = gated_delta(qkvc, gates, jnp.broadcast_to(nega, (8, LANES)),
                             jnp.broadcast_to(bias, (8, LANES)), nc)
            tk = 1024
            zoff = qkv_cols // tk
            xs = proj_residual(
                functools.partial(_dn_out_prologue, tk=tk), [o2, o2, u, dn_norm_g[i].reshape(1, DN_DV)],
                [pl.BlockSpec((1, tm_p, tk), lambda r, k: (0, r, k)),
                 pl.BlockSpec((1, tm_p, tk), lambda r, k: (1, r, k)),
                 pl.BlockSpec((tm_p, tk), lambda r, k: (r, zoff + k)),
                 pl.BlockSpec((1, DN_DV), lambda r, k: (0, 0))],
                dn_w_out[i].astype(BF16), xs, g1, nc, tm_p, tk)
        uf = norm_mod_matmul(xs, norm2_g[layer], sh2, sc2, ffn_w_up[layer].astype(BF16), nc)
        cw = ffn_conv_w[layer]
        xs = proj_residual(
            functools.partial(_ffn_prologue, tm=tm_p, nc=nc, t=t), [uf, uf, uf, uf, uf, uf, cw, cw],
            _halo_specs(tm_p, tk_f, t, 0) + _halo_specs(tm_p, tk_f, t, hidden // tk_f)
            + [pl.BlockSpec((3, tk_f), lambda r, k: (0, k)),
               pl.BlockSpec((3, tk_f), lambda r, k: (0, hidden // tk_f + k))],
            ffn_w_down[layer].astype(BF16), xs, g2, nc, tm_p, tk_f)
    return final_norm(xs, final_norm_g, nc)[None]
```

```python
import functools
import math

import jax
import jax.numpy as jnp
from jax import lax
from jax.experimental import pallas as pl
from jax.experimental.pallas import tpu as pltpu

F32 = jnp.float32
BF16 = jnp.bfloat16

EPS = 1e-6
NEG_INF = -1e30
GRID_W = 64
ROPE_BASE = 10000.0
ROPE_AXIS_FREQS = 32
DIFF_HEADS = 4
DIFF_DH = 128
SWA_HEADS = 8
SWA_KV_HEADS = 2
SWA_GROUP = SWA_HEADS // SWA_KV_HEADS
SWA_DH = 128
WINDOW = 128
DN_QK_HEADS = 16
DN_V_HEADS = 32
DN_DK = 128
DN_DV = 128
DN_CHUNK = 64
LANES = 128
HALO = 16
VMEM_LIMIT = 56 * 1024 * 1024


def _cp(sem):
    return pltpu.CompilerParams(dimension_semantics=sem, vmem_limit_bytes=VMEM_LIMIT)


def _pick(n, candidates):
    for c in candidates:
        if n % c == 0:
            return c
    raise ValueError(f"no tile for {n} among {candidates}")


def _silu(x):
    return x * jax.nn.sigmoid(x)


def _rows(i, tm):
    return i * tm + lax.broadcasted_iota(jnp.int32, (tm, 1), 0)


def _mod_kernel(a_ref, w_ref, b_ref, o_ref):
    a = _silu(a_ref[...]).astype(BF16)
    w = w_ref[0].astype(BF16)
    o_ref[0] = jnp.dot(a, w, preferred_element_type=F32) + b_ref[0]


def mod_vectors(cc, w_mod, b_mod):
    depth, d, n = w_mod.shape
    tn = _pick(n, (1024, 512, 256, 128))
    return pl.pallas_call(
        _mod_kernel,
        grid=(depth, n // tn),
        in_specs=[pl.BlockSpec((8, d), lambda l, j: (0, 0)),
                  pl.BlockSpec((1, d, tn), lambda l, j: (l, 0, j)),
                  pl.BlockSpec((1, 1, tn), lambda l, j: (l, 0, j))],
        out_specs=pl.BlockSpec((1, 8, tn), lambda l, j: (l, 0, j)),
        out_shape=jax.ShapeDtypeStruct((depth, 8, n), F32),
        compiler_params=_cp(("arbitrary", "arbitrary")),
        name="mod_vectors",
    )(cc, w_mod, b_mod.reshape(depth, 1, n))


def _nmm_kernel(x_ref, g_ref, sh_ref, sc_ref, w_ref, *rest, tm, tn, nc, rope):
    if rope:
        cos_ref, sin_ref, cmask_ref, cscale_ref, o_ref, h_ref = rest
    else:
        o_ref, h_ref = rest
    i = pl.program_id(0)
    j = pl.program_id(1)

    @pl.when(j == 0)
    def _():
        sub = math.gcd(math.gcd(tm, 32), nc)
        gain = g_ref[...] * (1.0 + sc_ref[0:2, :])
        for r0 in range(0, tm, sub):
            x = x_ref[r0:r0 + sub, :]
            ms = jnp.mean(x * x, axis=-1, keepdims=True)
            is_ctx = i * tm + r0 < nc
            gs = jnp.where(is_ctx, gain[1:2, :], gain[0:1, :])
            sh = jnp.where(is_ctx, sh_ref[1:2, :], sh_ref[0:1, :])
            h_ref[r0:r0 + sub, :] = (x * lax.rsqrt(ms + EPS) * gs + sh).astype(BF16)

    acc = jnp.dot(h_ref[...], w_ref[...], preferred_element_type=F32)
    if not rope:
        o_ref[...] = acc.astype(o_ref.dtype)
        return
    rs = math.gcd(tm, 64)
    lane = lax.broadcasted_iota(jnp.int32, (rs, LANES), 1)
    first = (lane % (2 * ROPE_AXIS_FREQS)) < ROPE_AXIS_FREQS
    for r0 in range(0, tm, rs):
        cos = cos_ref[r0:r0 + rs, :]
        sin = sin_ref[r0:r0 + rs, :]
        for c0 in range(0, tn, LANES):
            a = acc[r0:r0 + rs, c0:c0 + LANES]
            partner = jnp.where(first, pltpu.roll(a, LANES - ROPE_AXIS_FREQS, 1), pltpu.roll(a, ROPE_AXIS_FREQS, 1))
            roped = a * cos + partner * sin
            a = jnp.where(cmask_ref[:, c0:c0 + LANES] > 0.5, roped, a) * cscale_ref[:, c0:c0 + LANES]
            o_ref[r0:r0 + rs, c0:c0 + LANES] = a.astype(o_ref.dtype)


def norm_mod_matmul(x, g, sh, sc, w, nc, rope_args=None, tn_candidates=(1024, 512, 256, 128), out_dtype=BF16):
    t, d = x.shape
    n = w.shape[1]
    tm = _pick(t, (1280, 640, 512, 320, 256, 128))
    tn = _pick(n, tn_candidates)
    rope = rope_args is not None
    in_specs = [pl.BlockSpec((tm, d), lambda i, j: (i, 0)),
                pl.BlockSpec((1, d), lambda i, j: (0, 0)),
                pl.BlockSpec((8, d), lambda i, j: (0, 0)),
                pl.BlockSpec((8, d), lambda i, j: (0, 0)),
                pl.BlockSpec((d, tn), lambda i, j: (0, j))]
    args = [x, g.reshape(1, d), sh, sc, w]
    if rope:
        cos, sin, cmask, cscale = rope_args
        in_specs += [pl.BlockSpec((tm, LANES), lambda i, j: (i, 0)),
                     pl.BlockSpec((tm, LANES), lambda i, j: (i, 0)),
                     pl.BlockSpec((1, tn), lambda i, j: (0, j)),
                     pl.BlockSpec((1, tn), lambda i, j: (0, j))]
        args += [cos, sin, cmask, cscale]
    return pl.pallas_call(
        functools.partial(_nmm_kernel, tm=tm, tn=tn, nc=nc, rope=rope),
        grid=(t // tm, n // tn),
        in_specs=in_specs,
        out_specs=pl.BlockSpec((tm, tn), lambda i, j: (i, j)),
        out_shape=jax.ShapeDtypeStruct((t, n), out_dtype),
        scratch_shapes=[pltpu.VMEM((tm, d), BF16)],
        compiler_params=_cp(("arbitrary", "arbitrary")),
        name="norm_mod_matmul",
    )(*args)


def _proj_res_kernel(*refs, prologue, n_lhs, nk, tm, nc):
    lhs = refs[:n_lhs]
    w_ref, x_ref, gate_ref, o_ref, acc_ref = refs[n_lhs:]
    i = pl.program_id(0)
    k = pl.program_id(1)
    a = prologue(i, k, *lhs)
    p = jnp.dot(a, w_ref[...], preferred_element_type=F32)

    @pl.when(k == 0)
    def _():
        acc_ref[...] = p

    @pl.when(k > 0)
    def _():
        acc_ref[...] += p

    @pl.when(k == nk - 1)
    def _():
        is_ctx = _rows(i, tm) < nc
        gate = jnp.where(is_ctx, gate_ref[1:2, :], gate_ref[0:1, :])
        o_ref[...] = x_ref[...] + gate * acc_ref[...]


def proj_residual(prologue, lhs_args, lhs_specs, w, x, gate, nc, tm, tk):
    t, d = x.shape
    kdim = w.shape[0]
    nk = kdim // tk
    in_specs = list(lhs_specs) + [pl.BlockSpec((tk, d), lambda i, k: (k, 0)),
                                  pl.BlockSpec((tm, d), lambda i, k: (i, 0)),
                                  pl.BlockSpec((8, d), lambda i, k: (0, 0))]
    return pl.pallas_call(
        functools.partial(_proj_res_kernel, prologue=prologue, n_lhs=len(lhs_args), nk=nk, tm=tm, nc=nc),
        grid=(t // tm, nk),
        in_specs=in_specs,
        out_specs=pl.BlockSpec((tm, d), lambda i, k: (i, 0)),
        out_shape=jax.ShapeDtypeStruct((t, d), F32),
        scratch_shapes=[pltpu.VMEM((tm, d), F32)],
        compiler_params=_cp(("arbitrary", "arbitrary")),
        name="proj_residual",
    )(*lhs_args, w, x, gate)


def _halo_specs(tm, tk, t, col_off):
    nh = t // HALO
    per = tm // HALO
    return [pl.BlockSpec((tm, tk), lambda i, k: (i, col_off + k)),
            pl.BlockSpec((HALO, tk), lambda i, k: (jnp.maximum(i * per - 1, 0), col_off + k)),
            pl.BlockSpec((HALO, tk), lambda i, k: (jnp.minimum((i + 1) * per, nh - 1), col_off + k))]


CONV_STRIP = 128
SUBLANES = 8


def _patch_rows(x, g0, patch):
    parts = [x[:g0], patch(x[g0:g0 + SUBLANES]), x[g0 + SUBLANES:]]
    return jnp.concatenate([p for p in parts if p.shape[0]], axis=0)


def _conv3_strips(u_ref, prev_ref, next_ref, cw_ref, i, tm, nc, t):
    s = CONV_STRIP
    sub = lax.broadcasted_iota(jnp.int32, (SUBLANES, 1), 0)
    w = cw_ref[...]
    w0, w1, w2 = w[0:1, :], w[1:2, :], w[2:3, :]
    halo_prev = jnp.where(i == 0, 0.0, prev_ref[...].astype(F32)[HALO - 1:HALO, :])
    halo_next = jnp.where(i == t // tm - 1, 0.0, next_ref[...].astype(F32)[0:1, :])
    n = tm // s
    strips = [u_ref[j * s:(j + 1) * s, :].astype(F32) for j in range(n)]
    first_blk, first_row = nc // tm, nc % tm
    last_blk, last_row = (nc - 1) // tm, (nc - 1) % tm
    for j in range(n):
        u = strips[j]
        before = halo_prev if j == 0 else strips[j - 1][s - 1:s, :]
        after = halo_next if j == n - 1 else strips[j + 1][0:1, :]
        up = _patch_rows(pltpu.roll(u, 1, 0), 0, lambda g: jnp.where(sub == 0, before, g))
        un = _patch_rows(pltpu.roll(u, s - 1, 0), s - SUBLANES, lambda g: jnp.where(sub == SUBLANES - 1, after, g))
        if 0 < nc < t and first_row // s == j:
            r = first_row % s
            up = _patch_rows(up, r // SUBLANES * SUBLANES,
                             lambda g: jnp.where((sub == r % SUBLANES) & (i == first_blk), 0.0, g))
        if 0 < nc < t and last_row // s == j:
            r = last_row % s
            un = _patch_rows(un, r // SUBLANES * SUBLANES,
                             lambda g: jnp.where((sub == r % SUBLANES) & (i == last_blk), 0.0, g))
        yield up * w0 + u * w1 + un * w2


ATTN_STRIP = 32
LOG2E = math.log2(math.e)


def _diff_attn_kernel(lam_ref, g_ref, q_ref, k_ref, v_ref, o_ref, m_ref, l_ref, acc_ref, p_ref, alpha_ref,
                      *, tq, tk, nc, nk, lam_init):
    i = pl.program_id(1)
    kk = pl.program_id(2)

    @pl.when(kk == 0)
    def _():
        m_ref[...] = jnp.full(m_ref.shape, NEG_INF, F32)
        l_ref[...] = jnp.zeros(l_ref.shape, F32)
        acc_ref[...] = jnp.zeros(acc_ref.shape, F32)

    def step(masked):
        nt = (((1,), (1,)), ((), ()))
        v = v_ref[...]
        for c in range(2):
            cols = slice(c * DIFF_DH, (c + 1) * DIFF_DH)
            s_all = lax.dot_general(q_ref[:, cols], k_ref[:, cols], nt, preferred_element_type=F32)
            for r0 in range(0, tq, ATTN_STRIP):
                rows = slice(r0, r0 + ATTN_STRIP)
                s = s_all[rows, :]
                if masked:
                    qrow = i * tq + r0 + lax.broadcasted_iota(jnp.int32, (ATTN_STRIP, 1), 0)
                    kcol = kk * tk + lax.broadcasted_iota(jnp.int32, (1, tk), 1)
                    s = jnp.where((qrow >= nc) | (kcol < nc), s, NEG_INF)
                m_prev = m_ref[c, rows, :]
                m_new = jnp.maximum(m_prev, jnp.max(s, axis=-1, keepdims=True))
                alpha = jnp.exp2(m_prev - m_new)
                p = jnp.exp2(s - m_new)
                l_ref[c, rows, :] = alpha * l_ref[c, rows, :] + jnp.sum(p, axis=-1, keepdims=True)
                m_ref[c, rows, :] = m_new
                alpha_ref[c, rows, :] = alpha
                p_ref[c, rows, :] = p.astype(BF16)
            acc_ref[c] = alpha_ref[c] * acc_ref[c] + jnp.dot(p_ref[c], v, preferred_element_type=F32)

    has_ctx_rows = i * tq < nc

    @pl.when(has_ctx_rows)
    def _():
        step(True)

    @pl.when(jnp.logical_not(has_ctx_rows))
    def _():
        step(False)

    @pl.when(kk == nk - 1)
    def _():
        lf = lam_ref[...]
        lam = (jnp.exp(jnp.sum(lf[0:1] * lf[1:2], axis=-1, keepdims=True))
               - jnp.exp(jnp.sum(lf[2:3] * lf[3:4], axis=-1, keepdims=True)) + lam_init)
        o = acc_ref[0] / l_ref[0] - lam * (acc_ref[1] / l_ref[1])
        ms = jnp.mean(o * o, axis=-1, keepdims=True)
        o_ref[...] = (o * lax.rsqrt(ms + EPS) * g_ref[...] * (1.0 - lam_init)).astype(o_ref.dtype)


def diff_attention(qkv, lam_vec, subln_g, nc, lam_init, col_q, col_k, col_v):
    t = qkv.shape[0]
    tq = _pick(t, (1280, 640, 256, 128))
    tk = _pick(t, (1280, 256, 128))
    nk = t // tk
    hw = 2 * DIFF_DH
    return pl.pallas_call(
        functools.partial(_diff_attn_kernel, tq=tq, tk=tk, nc=nc, nk=nk, lam_init=lam_init),
        grid=(DIFF_HEADS, t // tq, nk),
        in_specs=[pl.BlockSpec((4, DIFF_DH), lambda h, i, k: (0, 0)),
                  pl.BlockSpec((1, hw), lambda h, i, k: (0, 0)),
                  pl.BlockSpec((tq, hw), lambda h, i, k: (i, col_q // hw + h)),
                  pl.BlockSpec((tk, hw), lambda h, i, k: (k, col_k // hw + h)),
                  pl.BlockSpec((tk, hw), lambda h, i, k: (k, col_v // hw + h))],
        out_specs=pl.BlockSpec((tq, hw), lambda h, i, k: (i, h)),
        out_shape=jax.ShapeDtypeStruct((t, DIFF_HEADS * hw), BF16),
        scratch_shapes=[pltpu.VMEM((2, tq, 1), F32), pltpu.VMEM((2, tq, 1), F32), pltpu.VMEM((2, tq, hw), F32),
                        pltpu.VMEM((2, tq, tk), BF16), pltpu.VMEM((2, tq, 1), F32)],
        compiler_params=_cp(("arbitrary", "arbitrary", "arbitrary")),
        name="diff_attention",
    )(lam_vec, subln_g.reshape(1, hw), qkv, qkv, qkv)


def _swa_kernel(sink_ref, q_ref, kc_ref, k0_ref, k1_ref, k2_ref, vc_ref, v0_ref, v1_ref, v2_ref, o_ref,
                *, nc, t):
    r = pl.program_id(1)
    qb = WINDOW
    q = jnp.concatenate([q_ref[:, g * SWA_DH:(g + 1) * SWA_DH] for g in range(SWA_GROUP)], axis=0)
    sk = sink_ref[0]
    sink = jnp.concatenate([jnp.broadcast_to(sk[g:g + 1, 0:1], (qb, 1)) for g in range(SWA_GROUP)], axis=0)
    nt = (((1,), (1,)), ((), ()))
    local_q = lax.broadcasted_iota(jnp.int32, (qb, 1), 0)
    qrow = jnp.concatenate([r * qb + local_q] * SWA_GROUP, axis=0)
    s_list = [lax.dot_general(q, kc_ref[...], nt, preferred_element_type=F32)]
    for w, k_ref in zip((-1, 0, 1), (k0_ref, k1_ref, k2_ref)):
        s = lax.dot_general(q, k_ref[...], nt, preferred_element_type=F32)
        krow = (r + w) * qb + lax.broadcasted_iota(jnp.int32, (1, qb), 1)
        valid = (jnp.abs(qrow - krow) <= WINDOW) & (krow >= nc) & (krow < t) & (qrow >= nc)
        s_list.append(jnp.where(valid, s, NEG_INF))
    m = sink
    for s in s_list:
        m = jnp.maximum(m, jnp.max(s, axis=-1, keepdims=True))
    denom = jnp.exp(sink - m)
    out = jnp.zeros((SWA_GROUP * qb, SWA_DH), F32)
    for s, v_ref in zip(s_list, (vc_ref, v0_ref, v1_ref, v2_ref)):
        e = jnp.exp(s - m)
        denom = denom + jnp.sum(e, axis=-1, keepdims=True)
        out = out + jnp.dot(e.astype(BF16), v_ref[...], preferred_element_type=F32)
    out = out / denom
    for g in range(SWA_GROUP):
        o_ref[:, g * SWA_DH:(g + 1) * SWA_DH] = out[g * qb:(g + 1) * qb, :].astype(o_ref.dtype)


def swa_attention(qkv, sink, nc, col_q, col_k, col_v):
    t = qkv.shape[0]
    qb = WINDOW
    nb = t // qb
    gw = SWA_GROUP * SWA_DH
    sink_b = jnp.broadcast_to(sink.astype(F32).reshape(SWA_KV_HEADS, SWA_GROUP, 1), (SWA_KV_HEADS, SWA_GROUP, LANES))
    kcb, vcb = col_k // SWA_DH, col_v // SWA_DH

    def win(cb, w):
        return pl.BlockSpec((qb, SWA_DH), lambda kv, r: (jnp.clip(r + w, 0, nb - 1), cb + kv))

    return pl.pallas_call(
        functools.partial(_swa_kernel, nc=nc, t=t),
        grid=(SWA_KV_HEADS, nb),
        in_specs=[pl.BlockSpec((1, SWA_GROUP, LANES), lambda kv, r: (kv, 0, 0)),
                  pl.BlockSpec((qb, gw), lambda kv, r: (r, col_q // gw + kv)),
                  pl.BlockSpec((nc, SWA_DH), lambda kv, r: (0, kcb + kv)),
                  win(kcb, -1), win(kcb, 0), win(kcb, 1),
                  pl.BlockSpec((nc, SWA_DH), lambda kv, r: (0, vcb + kv)),
                  win(vcb, -1), win(vcb, 0), win(vcb, 1)],
        out_specs=pl.BlockSpec((qb, gw), lambda kv, r: (r, kv)),
        out_shape=jax.ShapeDtypeStruct((t, SWA_HEADS * SWA_DH), BF16),
        compiler_params=_cp(("arbitrary", "arbitrary")),
        name="swa_attention",
    )(sink_b, qkv, qkv, qkv, qkv, qkv, qkv, qkv, qkv, qkv)


def _dn_conv_kernel(u_ref, prev_ref, next_ref, cw_ref, o_ref, *, tm, tc, nc, t, n_q, n_qk):
    i = pl.program_id(0)
    j = pl.program_id(1)
    s = CONV_STRIP

    @pl.when(j < n_qk)
    def _():
        scale = jnp.where(j < n_q, DN_DK ** -0.5, 1.0).astype(F32)
        for n, conv in enumerate(_conv3_strips(u_ref, prev_ref, next_ref, cw_ref, i, tm, nc, t)):
            y = _silu(conv)
            for c0 in range(0, tc, DN_DK):
                a = y[:, c0:c0 + DN_DK]
                ss = jnp.sum(a * a, axis=-1, keepdims=True)
                o_ref[n * s:(n + 1) * s, c0:c0 + DN_DK] = (a * lax.rsqrt(ss + EPS) * scale).astype(o_ref.dtype)

    @pl.when(j >= n_qk)
    def _():
        for n, conv in enumerate(_conv3_strips(u_ref, prev_ref, next_ref, cw_ref, i, tm, nc, t)):
            o_ref[n * s:(n + 1) * s, :] = _silu(conv).astype(o_ref.dtype)


def dn_conv(u, conv_w, nc, n_cols):
    t = u.shape[0]
    tm = _pick(t, (640, 320, 256, 128))
    tc = 1024
    qk_cols = DN_QK_HEADS * DN_DK
    return pl.pallas_call(
        functools.partial(_dn_conv_kernel, tm=tm, tc=tc, nc=nc, t=t, n_q=qk_cols // tc, n_qk=2 * qk_cols // tc),
        grid=(t // tm, n_cols // tc),
        in_specs=_halo_specs(tm, tc, t, 0) + [pl.BlockSpec((3, tc), lambda i, k: (0, k))],
        out_specs=pl.BlockSpec((tm, tc), lambda i, k: (i, k)),
        out_shape=jax.ShapeDtypeStruct((t, n_cols), BF16),
        compiler_params=_cp(("arbitrary", "arbitrary")),
        name="dn_conv",
    )(u, u, u, conv_w)


def _split3(x):
    hi = x.astype(BF16)
    r1 = x - hi.astype(F32)
    mid = r1.astype(BF16)
    lo = (r1 - mid.astype(F32)).astype(BF16)
    return hi, mid, lo


def _delta_kernel(gf_ref, gr_ref, nega_ref, bias_ref, qf_ref, kf_ref, vf_ref, qr_ref, kr_ref, vr_ref,
                       of_ref, or_ref, s_ref):
    s_idx = pl.program_id(0)
    c = DN_CHUNK
    nt = (((1,), (1,)), ((), ()))
    tn = (((0,), (0,)), ((), ()))

    @pl.when(s_idx == 0)
    def _():
        s_ref[...] = jnp.zeros(s_ref.shape, F32)

    def dot(x, y):
        return jnp.dot(x, y, preferred_element_type=F32)

    ri = lax.broadcasted_iota(jnp.int32, (c, c), 0)
    ci = lax.broadcasted_iota(jnp.int32, (c, c), 1)
    eye = ri == ci
    eye_f = eye.astype(F32)
    nh = DN_V_HEADS
    ctx = []
    for d, (g_ref, q_ref, k_ref, v_ref, o_ref) in enumerate(((gf_ref, qf_ref, kf_ref, vf_ref, of_ref),
                                                           (gr_ref, qr_ref, kr_ref, vr_ref, or_ref))):
        shift = (2 * LANES - d * 2 * nh) % LANES
        gt = pltpu.roll(g_ref[...], shift, 1) if shift else g_ref[...]
        nega = pltpu.roll(nega_ref[...], shift, 1) if shift else nega_ref[...]
        bias = pltpu.roll(bias_ref[...], shift, 1) if shift else bias_ref[...]
        z = gt + bias[0:1, :]
        g_all = nega[0:1, :] * (jnp.maximum(z, 0.0) + jnp.log(1.0 + jnp.exp(-jnp.abs(z))))
        order = (ri - ci) * (1 if d == 0 else -1)
        incl = order >= 0
        csum = incl.astype(BF16)
        hi, mid, lo = _split3(g_all)
        ctx.append(dict(beta=jax.nn.sigmoid(gt), gam=dot(csum, hi) + dot(csum, mid) + dot(csum, lo),
                        tot=jnp.sum(g_all, axis=0, keepdims=True), incl=incl, strict=order > 0,
                        q=q_ref, k=k_ref, v=v_ref, o=o_ref))

    heads = [(d, hv) for d in range(2) for hv in range(nh)]
    pairs = [(d, hq) for d in range(2) for hq in range(DN_QK_HEADS)]
    kb = {p: ctx[p[0]]["k"][:, p[1] * DN_DK:(p[1] + 1) * DN_DK] for p in pairs}
    qb = {p: ctx[p[0]]["q"][:, p[1] * DN_DK:(p[1] + 1) * DN_DK] for p in pairs}
    kq = {p: lax.dot_general(jnp.concatenate([kb[p], qb[p]], axis=0), kb[p], nt, preferred_element_type=F32)
          for p in pairs}
    kf = {p: kb[p].astype(F32) for p in pairs}
    qf = {p: qb[p].astype(F32) for p in pairs}
    qk_of = {h: (h[0], h[1] // 2) for h in heads}
    beta = {h: ctx[h[0]]["beta"][:, h[1]:h[1] + 1] for h in heads}
    gam = {h: ctx[h[0]]["gam"][:, nh + h[1]:nh + h[1] + 1] for h in heads}
    tot = {h: ctx[h[0]]["tot"][:, nh + h[1]:nh + h[1] + 1] for h in heads}
    incl = {h: ctx[h[0]]["incl"] for h in heads}
    gam_row = {h: jnp.sum(jnp.where(eye, gam[h], 0.0), axis=0, keepdims=True) for h in heads}
    dec = {h: jnp.where(incl[h], jnp.exp(jnp.where(incl[h], gam[h] - gam_row[h], 0.0)), 0.0) for h in heads}
    a = {h: jnp.where(ctx[h[0]]["strict"], beta[h] * kq[qk_of[h]][:c] * dec[h], 0.0) for h in heads}
    pw = {h: -a[h] for h in heads}
    tinv = {h: eye_f + pw[h] for h in heads}
    for _ in range(int(math.log2(c)) - 2):
        pwb = {h: pw[h].astype(BF16) for h in heads}
        pw = {h: dot(pwb[h], pwb[h]) for h in heads}
        tinv = {h: tinv[h] + dot(tinv[h].astype(BF16), pw[h].astype(BF16)) for h in heads}
    tb = {h: tinv[h].astype(BF16) for h in heads}
    resid = {h: eye_f - tinv[h] - dot(a[h].astype(BF16), tb[h]) for h in heads}
    tinv = {h: tinv[h] + dot(tb[h], resid[h].astype(BF16)) for h in heads}
    tb = {h: tinv[h].astype(BF16) for h in heads}
    eg = {h: jnp.exp(gam[h]) for h in heads}
    vf = {h: ctx[h[0]]["v"][:, h[1] * DN_DV:(h[1] + 1) * DN_DV].astype(F32) for h in heads}
    rhs = {h: jnp.concatenate([kf[qk_of[h]] * (beta[h] * eg[h]), vf[h] * beta[h]], axis=1).astype(BF16) for h in heads}
    sol = {h: dot(tb[h], rhs[h]) for h in heads}
    aqk = {h: jnp.where(incl[h], kq[qk_of[h]][c:] * dec[h], 0.0).astype(BF16) for h in heads}
    wq = {h: jnp.concatenate([sol[h][:, :DN_DK], qf[qk_of[h]] * eg[h]], axis=0).astype(BF16) for h in heads}
    kd = {h: (kf[qk_of[h]] * jnp.exp(tot[h] - gam[h])).astype(BF16) for h in heads}
    state = {h: s_ref[h[0] * nh + h[1]] for h in heads}
    ws = {h: dot(wq[h], state[h].astype(BF16)) for h in heads}
    vnb = {h: (sol[h][:, DN_DK:] - ws[h][:c]).astype(BF16) for h in heads}
    o = {h: ws[h][c:] + dot(aqk[h], vnb[h]) for h in heads}
    upd = {h: lax.dot_general(kd[h], vnb[h], tn, preferred_element_type=F32) for h in heads}
    for h in heads:
        s_ref[h[0] * nh + h[1]] = jnp.exp(tot[h]) * state[h] + upd[h]
        ctx[h[0]]["o"][:, h[1] * DN_DV:(h[1] + 1) * DN_DV] = o[h].astype(of_ref.dtype)


def gated_delta(qkvc, gates, nega, bias, nc):
    t = qkvc.shape[0]
    c = DN_CHUNK
    nch = t // c
    ncc = nc // c
    qw = DN_QK_HEADS * DN_DK
    vw = DN_V_HEADS * DN_DV

    def rev(s):
        return jnp.where(s < ncc, ncc - 1 - s, nch - 1 - (s - ncc))

    def specs(chunk):
        return [pl.BlockSpec((c, qw), lambda s: (chunk(s), 0)),
                pl.BlockSpec((c, qw), lambda s: (chunk(s), 1)),
                pl.BlockSpec((c, vw), lambda s: (chunk(s), 2 * qw // vw))]

    fwd = lambda s: s
    out = jax.ShapeDtypeStruct((t, vw), BF16)
    return pl.pallas_call(
        _delta_kernel,
        grid=(nch,),
        in_specs=[pl.BlockSpec((c, LANES), lambda s: (s, 0)),
                  pl.BlockSpec((c, LANES), lambda s: (rev(s), 0)),
                  pl.BlockSpec((8, LANES), lambda s: (0, 0)),
                  pl.BlockSpec((8, LANES), lambda s: (0, 0))] + specs(fwd) + specs(rev),
        out_specs=[pl.BlockSpec((c, vw), lambda s: (s, 0)), pl.BlockSpec((c, vw), lambda s: (rev(s), 0))],
        out_shape=[out, out],
        scratch_shapes=[pltpu.VMEM((2 * DN_V_HEADS, DN_DK, DN_DV), F32)],
        compiler_params=_cp(("arbitrary",)),
        name="gated_delta",
    )(gates, gates, nega, bias, qkvc, qkvc, qkvc, qkvc, qkvc, qkvc)


def _attn_out_prologue(i, k, oa_ref, ob_ref, *, n_a):
    return jnp.where(k < n_a, oa_ref[...], ob_ref[...])


def _dn_out_prologue(i, k, of_ref, or_ref, z_ref, g_ref, *, tk):
    tm = z_ref.shape[0]
    rs = math.gcd(tm, 16)
    strips = []
    for r0 in range(0, tm, rs):
        parts = []
        for c0 in range(0, tk, DN_DV):
            a = of_ref[r0:r0 + rs, c0:c0 + DN_DV].astype(F32) + or_ref[r0:r0 + rs, c0:c0 + DN_DV].astype(F32)
            ms = jnp.mean(a * a, axis=-1, keepdims=True)
            z = z_ref[r0:r0 + rs, c0:c0 + DN_DV].astype(F32)
            parts.append((a * lax.rsqrt(ms + EPS) * g_ref[...] * _silu(z)).astype(BF16))
        strips.append(jnp.concatenate(parts, axis=-1))
    return jnp.concatenate(strips, axis=0)


def _ffn_prologue(i, k, ug, pg, ng, uu, pu, nu, cwg, cwu, *, tm, nc, t):
    gates = _conv3_strips(ug, pg, ng, cwg, i, tm, nc, t)
    ups = _conv3_strips(uu, pu, nu, cwu, i, tm, nc, t)
    return jnp.concatenate([(_silu(g) * u).astype(BF16) for g, u in zip(gates, ups)], axis=0)


def _final_norm_kernel(x_ref, g_ref, o_ref):
    x = x_ref[...]
    ms = jnp.mean(x * x, axis=-1, keepdims=True)
    o_ref[...] = x * lax.rsqrt(ms + EPS) * g_ref[...]


def final_norm(x, g, nc):
    t, d = x.shape
    tm = math.gcd(nc, 256)
    off = nc // tm
    return pl.pallas_call(
        _final_norm_kernel,
        grid=((t - nc) // tm,),
        in_specs=[pl.BlockSpec((tm, d), lambda i: (i + off, 0)), pl.BlockSpec((1, d), lambda i: (0, 0))],
        out_specs=pl.BlockSpec((tm, d), lambda i: (i, 0)),
        out_shape=jax.ShapeDtypeStruct((t - nc, d), F32),
        compiler_params=_cp(("arbitrary",)),
        name="final_norm",
    )(x, g.reshape(1, d))


def _rope_tables(n, nc):
    rows = n // GRID_W
    inv = ROPE_BASE ** (-jnp.arange(ROPE_AXIS_FREQS, dtype=F32) / ROPE_AXIS_FREQS)
    ar = jnp.arange(rows, dtype=F32)[:, None] * inv
    ac = jnp.arange(GRID_W, dtype=F32)[:, None] * inv

    def per_token(row_table, col_table):
        r = jnp.broadcast_to(row_table[:, None, :], (rows, GRID_W, ROPE_AXIS_FREQS)).reshape(n, ROPE_AXIS_FREQS)
        c = jnp.broadcast_to(col_table[None, :, :], (rows, GRID_W, ROPE_AXIS_FREQS)).reshape(n, ROPE_AXIS_FREQS)
        return r, c

    cr, cc = per_token(jnp.cos(ar), jnp.cos(ac))
    sr, sc = per_token(jnp.sin(ar), jnp.sin(ac))
    cos = jnp.concatenate([cr, cr, cc, cc], axis=-1)
    sin = jnp.concatenate([-sr, sr, -sc, sc], axis=-1)
    cos = jnp.concatenate([jnp.ones((nc, LANES), F32), cos], axis=0)
    sin = jnp.concatenate([jnp.zeros((nc, LANES), F32), sin], axis=0)
    return cos, sin


def kernel(x, c, ctx, c_ctx, w_mod, b_mod, norm1_g, norm2_g, attn_w_in, diff_lambda, diff_subln_g, swa_sink,
           attn_w_out, dn_w_in, dn_conv_w, dn_a_log, dn_dt_bias, dn_norm_g, dn_w_out, ffn_w_up, ffn_conv_w,
           ffn_w_down, final_norm_g):
    assert x.shape[0] == 1, "single-sequence kernel"
    n, d = x.shape[1], x.shape[2]
    nc = ctx.shape[1]
    t = nc + n
    depth = w_mod.shape[0]
    xs = jnp.concatenate([ctx[0], x[0]], axis=0)
    cc = jnp.zeros((8, d), F32).at[0].set(c[0]).at[1].set(c_ctx)
    mods = mod_vectors(cc, w_mod, b_mod)
    cos, sin = _rope_tables(n, nc)

    da, sb_, skv = DIFF_HEADS * 2 * DIFF_DH, SWA_HEADS * SWA_DH, SWA_KV_HEADS * SWA_DH
    col_qa, col_ka, col_qb = 0, da, 2 * da
    col_va = col_qb + sb_
    col_kb = col_va + da
    col_vb = col_kb + skv
    n_attn = col_vb + skv
    cols = jnp.arange(n_attn)
    cmask = ((cols < col_va) | ((cols >= col_kb) & (cols < col_vb))).astype(F32).reshape(1, n_attn)
    cscale = jnp.where(cols < col_ka, DIFF_DH ** -0.5 * LOG2E,
                       jnp.where((cols >= col_qb) & (cols < col_va), SWA_DH ** -0.5, 1.0))
    cscale = cscale.astype(F32).reshape(1, n_attn)

    tm_p = _pick(t, (640, 320, 256, 128))
    hidden = ffn_w_down.shape[1]
    tk_f = _pick(hidden, (1408, 512, 256, 128))
    qkv_cols = 2 * DN_QK_HEADS * DN_DK + DN_V_HEADS * DN_DV
    dn_out = DN_V_HEADS * DN_DV

    for layer in range(depth):
        m = mods[layer]
        sh1, sc1, g1, sh2, sc2, g2 = (m[:, j * d:(j + 1) * d] for j in range(6))
        i = layer // 2
        if layer % 2 == 0:
            lam_init = 0.8 - 0.6 * math.exp(-0.3 * layer)
            w = attn_w_in[i]
            w = jnp.concatenate([w[:, 0:2 * da], w[:, 3 * da:3 * da + sb_], w[:, 2 * da:3 * da],
                                 w[:, 3 * da + sb_:]], axis=1).astype(BF16)
            qkv = norm_mod_matmul(xs, norm1_g[layer], sh1, sc1, w, nc, rope_args=(cos, sin, cmask, cscale),
                                  tn_candidates=(512, 256, 128))
            oa = diff_attention(qkv, diff_lambda[i], diff_subln_g[i], nc, lam_init, col_qa, col_ka, col_va)
            ob = swa_attention(qkv, swa_sink[i], nc, col_qb, col_kb, col_vb)
            tk = da
            n_a = da // tk
            xs = proj_residual(
                functools.partial(_attn_out_prologue, n_a=n_a), [oa, ob],
                [pl.BlockSpec((tm_p, tk), lambda r, k: (r, jnp.minimum(k, n_a - 1))),
                 pl.BlockSpec((tm_p, tk), lambda r, k: (r, jnp.maximum(k - n_a, 0)))],
                attn_w_out[i].astype(BF16), xs, g1, nc, tm_p, tk)
        else:
            w = dn_w_in[i]
            u = norm_mod_matmul(xs, norm1_g[layer], sh1, sc1, w[:, :qkv_cols + dn_out].astype(BF16), nc)
            gates = norm_mod_matmul(xs, norm1_g[layer], sh1, sc1, w[:, qkv_cols + dn_out:].astype(BF16), nc, out_dtype=F32)
            qkvc = dn_conv(u, dn_conv_w[i], nc, qkv_cols)
            zero = jnp.zeros((2, DN_V_HEADS), F32)
            nega = jnp.stack([zero, -jnp.exp(dn_a_log[i].astype(F32))], axis=1).reshape(1, 4 * DN_V_HEADS)
            bias = jnp.stack([zero, dn_dt_bias[i].astype(F32)], axis=1).reshape(1, 4 * DN_V_HEADS)
            o_f, o_r = gated_delta(qkvc, gates, jnp.broadcast_to(nega, (8, LANES)),
                                   jnp.broadcast_to(bias, (8, LANES)), nc)
            tk = 1024
            zoff = qkv_cols // tk
            xs = proj_residual(
                functools.partial(_dn_out_prologue, tk=tk), [o_f, o_r, u, dn_norm_g[i].reshape(1, DN_DV)],
                [pl.BlockSpec((tm_p, tk), lambda r, k: (r, k)),
                 pl.BlockSpec((tm_p, tk), lambda r, k: (r, k)),
                 pl.BlockSpec((tm_p, tk), lambda r, k: (r, zoff + k)),
                 pl.BlockSpec((1, DN_DV), lambda r, k: (0, 0))],
                dn_w_out[i].astype(BF16), xs, g1, nc, tm_p, tk)
        uf = norm_mod_matmul(xs, norm2_g[layer], sh2, sc2, ffn_w_up[layer].astype(BF16), nc)
        cw = ffn_conv_w[layer]
        xs = proj_residual(
            functools.partial(_ffn_prologue, tm=tm_p, nc=nc, t=t), [uf, uf, uf, uf, uf, uf, cw, cw],
            _halo_specs(tm_p, tk_f, t, 0) + _halo_specs(tm_p, tk_f, t, hidden // tk_f)
            + [pl.BlockSpec((3, tk_f), lambda r, k: (0, k)),
               pl.BlockSpec((3, tk_f), lambda r, k: (0, hidden // tk_f + k))],
            ffn_w_down[layer].astype(BF16), xs, g2, nc, tm_p, tk_f)
    return final_norm(xs, final_norm_g, nc)[None]
```

```python
import functools
import math

import jax
import jax.numpy as jnp
from jax import lax
from jax.experimental import pallas as pl
from jax.experimental.pallas import tpu as pltpu

F32 = jnp.float32
BF16 = jnp.bfloat16

EPS = 1e-6
NEG_INF = -1e30
GRID_W = 64
ROPE_BASE = 10000.0
ROPE_AXIS_FREQS = 32
DIFF_HEADS = 4
DIFF_DH = 128
SWA_HEADS = 8
SWA_KV_HEADS = 2
SWA_GROUP = SWA_HEADS // SWA_KV_HEADS
SWA_DH = 128
WINDOW = 128
DN_QK_HEADS = 16
DN_V_HEADS = 32
DN_DK = 128
DN_DV = 128
DN_CHUNK = 64
LANES = 128
HALO = 16
VMEM_LIMIT = 56 * 1024 * 1024


def _cp(sem):
    return pltpu.CompilerParams(dimension_semantics=sem, vmem_limit_bytes=VMEM_LIMIT)


def _pick(n, candidates):
    for c in candidates:
        if n % c == 0:
            return c
    raise ValueError(f"no tile for {n} among {candidates}")


def _silu(x):
    return x * jax.nn.sigmoid(x)


def _rows(i, tm):
    return i * tm + lax.broadcasted_iota(jnp.int32, (tm, 1), 0)


def _mod_kernel(a_ref, w_ref, b_ref, o_ref):
    a = _silu(a_ref[...]).astype(BF16)
    w = w_ref[0].astype(BF16)
    o_ref[0] = jnp.dot(a, w, preferred_element_type=F32) + b_ref[0]


def mod_vectors(cc, w_mod, b_mod):
    depth, d, n = w_mod.shape
    tn = _pick(n, (1024, 512, 256, 128))
    return pl.pallas_call(
        _mod_kernel,
        grid=(depth, n // tn),
        in_specs=[pl.BlockSpec((8, d), lambda l, j: (0, 0)),
                  pl.BlockSpec((1, d, tn), lambda l, j: (l, 0, j)),
                  pl.BlockSpec((1, 1, tn), lambda l, j: (l, 0, j))],
        out_specs=pl.BlockSpec((1, 8, tn), lambda l, j: (l, 0, j)),
        out_shape=jax.ShapeDtypeStruct((depth, 8, n), F32),
        compiler_params=_cp(("arbitrary", "arbitrary")),
        name="mod_vectors",
    )(cc, w_mod, b_mod.reshape(depth, 1, n))


def _nmm_kernel(x_ref, g_ref, sh_ref, sc_ref, w_ref, *rest, tm, tn, nc, rope):
    if rope:
        cos_ref, sin_ref, cmask_ref, cscale_ref, o_ref, h_ref = rest
    else:
        o_ref, h_ref = rest
    i = pl.program_id(0)
    j = pl.program_id(1)

    @pl.when(j == 0)
    def _():
        sub = math.gcd(math.gcd(tm, 128), nc)
        gain = g_ref[...] * (1.0 + sc_ref[0:2, :])
        for r0 in range(0, tm, sub):
            x = x_ref[r0:r0 + sub, :]
            ms = jnp.mean(x * x, axis=-1, keepdims=True)
            is_ctx = i * tm + r0 < nc
            gs = jnp.where(is_ctx, gain[1:2, :], gain[0:1, :])
            sh = jnp.where(is_ctx, sh_ref[1:2, :], sh_ref[0:1, :])
            h_ref[r0:r0 + sub, :] = (x * lax.rsqrt(ms + EPS) * gs + sh).astype(BF16)

    acc = jnp.dot(h_ref[...], w_ref[...], preferred_element_type=F32)
    if not rope:
        o_ref[...] = acc.astype(o_ref.dtype)
        return
    rs = math.gcd(tm, 128)
    lane = lax.broadcasted_iota(jnp.int32, (rs, LANES), 1)
    first = (lane % (2 * ROPE_AXIS_FREQS)) < ROPE_AXIS_FREQS
    for r0 in range(0, tm, rs):
        cos = cos_ref[r0:r0 + rs, :]
        sin = sin_ref[r0:r0 + rs, :]
        for c0 in range(0, tn, LANES):
            a = acc[r0:r0 + rs, c0:c0 + LANES]
            partner = jnp.where(first, pltpu.roll(a, LANES - ROPE_AXIS_FREQS, 1), pltpu.roll(a, ROPE_AXIS_FREQS, 1))
            roped = a * cos + partner * sin
            a = jnp.where(cmask_ref[:, c0:c0 + LANES] > 0.5, roped, a) * cscale_ref[:, c0:c0 + LANES]
            o_ref[r0:r0 + rs, c0:c0 + LANES] = a.astype(o_ref.dtype)


def norm_mod_matmul(x, g, sh, sc, w, nc, rope_args=None, tn_candidates=(1024, 512, 256, 128), out_dtype=BF16):
    t, d = x.shape
    n = w.shape[1]
    tm = _pick(t, (1280, 640, 512, 320, 256, 128))
    tn = _pick(n, tn_candidates)
    rope = rope_args is not None
    in_specs = [pl.BlockSpec((tm, d), lambda i, j: (i, 0)),
                pl.BlockSpec((1, d), lambda i, j: (0, 0)),
                pl.BlockSpec((8, d), lambda i, j: (0, 0)),
                pl.BlockSpec((8, d), lambda i, j: (0, 0)),
                pl.BlockSpec((d, tn), lambda i, j: (0, j))]
    args = [x, g.reshape(1, d), sh, sc, w]
    if rope:
        cos, sin, cmask, cscale = rope_args
        in_specs += [pl.BlockSpec((tm, LANES), lambda i, j: (i, 0)),
                     pl.BlockSpec((tm, LANES), lambda i, j: (i, 0)),
                     pl.BlockSpec((1, tn), lambda i, j: (0, j)),
                     pl.BlockSpec((1, tn), lambda i, j: (0, j))]
        args += [cos, sin, cmask, cscale]
    return pl.pallas_call(
        functools.partial(_nmm_kernel, tm=tm, tn=tn, nc=nc, rope=rope),
        grid=(t // tm, n // tn),
        in_specs=in_specs,
        out_specs=pl.BlockSpec((tm, tn), lambda i, j: (i, j)),
        out_shape=jax.ShapeDtypeStruct((t, n), out_dtype),
        scratch_shapes=[pltpu.VMEM((tm, d), BF16)],
        compiler_params=_cp(("arbitrary", "arbitrary")),
        name="norm_mod_matmul",
    )(*args)


def _proj_res_kernel(*refs, prologue, n_lhs, nk, tm, nc):
    lhs = refs[:n_lhs]
    w_ref, x_ref, gate_ref, o_ref, acc_ref = refs[n_lhs:]
    i = pl.program_id(0)
    k = pl.program_id(1)
    a = prologue(i, k, *lhs)
    p = jnp.dot(a, w_ref[...], preferred_element_type=F32)

    @pl.when(k == 0)
    def _():
        acc_ref[...] = p

    @pl.when(k > 0)
    def _():
        acc_ref[...] += p

    @pl.when(k == nk - 1)
    def _():
        is_ctx = _rows(i, tm) < nc
        gate = jnp.where(is_ctx, gate_ref[1:2, :], gate_ref[0:1, :])
        o_ref[...] = x_ref[...] + gate * acc_ref[...]


def proj_residual(prologue, lhs_args, lhs_specs, w, x, gate, nc, tm, tk):
    t, d = x.shape
    kdim = w.shape[0]
    nk = kdim // tk
    in_specs = list(lhs_specs) + [pl.BlockSpec((tk, d), lambda i, k: (k, 0)),
                                  pl.BlockSpec((tm, d), lambda i, k: (i, 0)),
                                  pl.BlockSpec((8, d), lambda i, k: (0, 0))]
    return pl.pallas_call(
        functools.partial(_proj_res_kernel, prologue=prologue, n_lhs=len(lhs_args), nk=nk, tm=tm, nc=nc),
        grid=(t // tm, nk),
        in_specs=in_specs,
        out_specs=pl.BlockSpec((tm, d), lambda i, k: (i, 0)),
        out_shape=jax.ShapeDtypeStruct((t, d), F32),
        scratch_shapes=[pltpu.VMEM((tm, d), F32)],
        compiler_params=_cp(("arbitrary", "arbitrary")),
        name="proj_residual",
    )(*lhs_args, w, x, gate)


def _halo_specs(tm, tk, t, col_off):
    nh = t // HALO
    per = tm // HALO
    return [pl.BlockSpec((tm, tk), lambda i, k: (i, col_off + k)),
            pl.BlockSpec((HALO, tk), lambda i, k: (jnp.maximum(i * per - 1, 0), col_off + k)),
            pl.BlockSpec((HALO, tk), lambda i, k: (jnp.minimum((i + 1) * per, nh - 1), col_off + k))]


CONV_STRIP = 128
SUBLANES = 8


def _patch_rows(x, g0, patch):
    parts = [x[:g0], patch(x[g0:g0 + SUBLANES]), x[g0 + SUBLANES:]]
    return jnp.concatenate([p for p in parts if p.shape[0]], axis=0)


def _conv3_strips(u_ref, prev_ref, next_ref, cw_ref, i, tm, nc, t):
    s = CONV_STRIP
    sub = lax.broadcasted_iota(jnp.int32, (SUBLANES, 1), 0)
    w = cw_ref[...]
    w0, w1, w2 = w[0:1, :], w[1:2, :], w[2:3, :]
    halo_prev = jnp.where(i == 0, 0.0, prev_ref[...].astype(F32)[HALO - 1:HALO, :])
    halo_next = jnp.where(i == t // tm - 1, 0.0, next_ref[...].astype(F32)[0:1, :])
    n = tm // s
    strips = [u_ref[j * s:(j + 1) * s, :].astype(F32) for j in range(n)]
    first_blk, first_row = nc // tm, nc % tm
    last_blk, last_row = (nc - 1) // tm, (nc - 1) % tm
    for j in range(n):
        u = strips[j]
        before = halo_prev if j == 0 else strips[j - 1][s - 1:s, :]
        after = halo_next if j == n - 1 else strips[j + 1][0:1, :]
        up = _patch_rows(pltpu.roll(u, 1, 0), 0, lambda g: jnp.where(sub == 0, before, g))
        un = _patch_rows(pltpu.roll(u, s - 1, 0), s - SUBLANES, lambda g: jnp.where(sub == SUBLANES - 1, after, g))
        if 0 < nc < t and first_row // s == j:
            r = first_row % s
            up = _patch_rows(up, r // SUBLANES * SUBLANES,
                             lambda g: jnp.where((sub == r % SUBLANES) & (i == first_blk), 0.0, g))
        if 0 < nc < t and last_row // s == j:
            r = last_row % s
            un = _patch_rows(un, r // SUBLANES * SUBLANES,
                             lambda g: jnp.where((sub == r % SUBLANES) & (i == last_blk), 0.0, g))
        yield up * w0 + u * w1 + un * w2


ATTN_STRIP = 64
LOG2E = math.log2(math.e)


def _diff_attn_kernel(lam_ref, g_ref, q_ref, k_ref, v_ref, o_ref, m_ref, l_ref, acc_ref, p_ref, alpha_ref,
                      *, tq, tk, nc, nk, lam_init):
    i = pl.program_id(1)
    kk = pl.program_id(2)

    @pl.when(kk == 0)
    def _():
        m_ref[...] = jnp.full(m_ref.shape, NEG_INF, F32)
        l_ref[...] = jnp.zeros(l_ref.shape, F32)
        acc_ref[...] = jnp.zeros(acc_ref.shape, F32)

    def step(masked):
        nt = (((1,), (1,)), ((), ()))
        v = v_ref[...]
        for c in range(2):
            cols = slice(c * DIFF_DH, (c + 1) * DIFF_DH)
            s_all = lax.dot_general(q_ref[:, cols], k_ref[:, cols], nt, preferred_element_type=F32)
            for r0 in range(0, tq, ATTN_STRIP):
                rows = slice(r0, r0 + ATTN_STRIP)
                s = s_all[rows, :]
                if masked:
                    qrow = i * tq + r0 + lax.broadcasted_iota(jnp.int32, (ATTN_STRIP, 1), 0)
                    kcol = kk * tk + lax.broadcasted_iota(jnp.int32, (1, tk), 1)
                    s = jnp.where((qrow >= nc) | (kcol < nc), s, NEG_INF)
                m_prev = m_ref[c, rows, :]
                m_new = jnp.maximum(m_prev, jnp.max(s, axis=-1, keepdims=True))
                alpha = jnp.exp2(m_prev - m_new)
                p = jnp.exp2(s - m_new)
                l_ref[c, rows, :] = alpha * l_ref[c, rows, :] + jnp.sum(p, axis=-1, keepdims=True)
                m_ref[c, rows, :] = m_new
                alpha_ref[c, rows, :] = alpha
                p_ref[c, rows, :] = p.astype(BF16)
            acc_ref[c] = alpha_ref[c] * acc_ref[c] + jnp.dot(p_ref[c], v, preferred_element_type=F32)

    has_ctx_rows = i * tq < nc

    @pl.when(has_ctx_rows)
    def _():
        step(True)

    @pl.when(jnp.logical_not(has_ctx_rows))
    def _():
        step(False)

    @pl.when(kk == nk - 1)
    def _():
        lf = lam_ref[...]
        lam = (jnp.exp(jnp.sum(lf[0:1] * lf[1:2], axis=-1, keepdims=True))
               - jnp.exp(jnp.sum(lf[2:3] * lf[3:4], axis=-1, keepdims=True)) + lam_init)
        o = acc_ref[0] / l_ref[0] - lam * (acc_ref[1] / l_ref[1])
        ms = jnp.mean(o * o, axis=-1, keepdims=True)
        o_ref[...] = (o * lax.rsqrt(ms + EPS) * g_ref[...] * (1.0 - lam_init)).astype(o_ref.dtype)


def diff_attention(qkv, lam_vec, subln_g, nc, lam_init, col_q, col_k, col_v):
    t = qkv.shape[0]
    tq = _pick(t, (1280, 640, 256, 128))
    tk = _pick(t, (1280, 256, 128))
    nk = t // tk
    hw = 2 * DIFF_DH
    return pl.pallas_call(
        functools.partial(_diff_attn_kernel, tq=tq, tk=tk, nc=nc, nk=nk, lam_init=lam_init),
        grid=(DIFF_HEADS, t // tq, nk),
        in_specs=[pl.BlockSpec((4, DIFF_DH), lambda h, i, k: (0, 0)),
                  pl.BlockSpec((1, hw), lambda h, i, k: (0, 0)),
                  pl.BlockSpec((tq, hw), lambda h, i, k: (i, col_q // hw + h)),
                  pl.BlockSpec((tk, hw), lambda h, i, k: (k, col_k // hw + h)),
                  pl.BlockSpec((tk, hw), lambda h, i, k: (k, col_v // hw + h))],
        out_specs=pl.BlockSpec((tq, hw), lambda h, i, k: (i, h)),
        out_shape=jax.ShapeDtypeStruct((t, DIFF_HEADS * hw), BF16),
        scratch_shapes=[pltpu.VMEM((2, tq, 1), F32), pltpu.VMEM((2, tq, 1), F32), pltpu.VMEM((2, tq, hw), F32),
                        pltpu.VMEM((2, tq, tk), BF16), pltpu.VMEM((2, tq, 1), F32)],
        compiler_params=_cp(("arbitrary", "arbitrary", "arbitrary")),
        name="diff_attention",
    )(lam_vec, subln_g.reshape(1, hw), qkv, qkv, qkv)


def _swa_kernel(sink_ref, q_ref, kc_ref, k0_ref, k1_ref, k2_ref, vc_ref, v0_ref, v1_ref, v2_ref, o_ref,
                *, nc, t):
    r = pl.program_id(1)
    qb = WINDOW
    q = jnp.concatenate([q_ref[:, g * SWA_DH:(g + 1) * SWA_DH] for g in range(SWA_GROUP)], axis=0)
    sk = sink_ref[0]
    sink = jnp.concatenate([jnp.broadcast_to(sk[g:g + 1, 0:1], (qb, 1)) for g in range(SWA_GROUP)], axis=0)
    nt = (((1,), (1,)), ((), ()))
    local_q = lax.broadcasted_iota(jnp.int32, (qb, 1), 0)
    qrow = jnp.concatenate([r * qb + local_q] * SWA_GROUP, axis=0)
    s_list = [lax.dot_general(q, kc_ref[...], nt, preferred_element_type=F32)]
    for w, k_ref in zip((-1, 0, 1), (k0_ref, k1_ref, k2_ref)):
        s = lax.dot_general(q, k_ref[...], nt, preferred_element_type=F32)
        krow = (r + w) * qb + lax.broadcasted_iota(jnp.int32, (1, qb), 1)
        valid = (jnp.abs(qrow - krow) <= WINDOW) & (krow >= nc) & (krow < t) & (qrow >= nc)
        s_list.append(jnp.where(valid, s, NEG_INF))
    m = sink
    for s in s_list:
        m = jnp.maximum(m, jnp.max(s, axis=-1, keepdims=True))
    denom = jnp.exp(sink - m)
    out = jnp.zeros((SWA_GROUP * qb, SWA_DH), F32)
    for s, v_ref in zip(s_list, (vc_ref, v0_ref, v1_ref, v2_ref)):
        e = jnp.exp(s - m)
        denom = denom + jnp.sum(e, axis=-1, keepdims=True)
        out = out + jnp.dot(e.astype(BF16), v_ref[...], preferred_element_type=F32)
    out = out / denom
    for g in range(SWA_GROUP):
        o_ref[:, g * SWA_DH:(g + 1) * SWA_DH] = out[g * qb:(g + 1) * qb, :].astype(o_ref.dtype)


def swa_attention(qkv, sink, nc, col_q, col_k, col_v):
    t = qkv.shape[0]
    qb = WINDOW
    nb = t // qb
    gw = SWA_GROUP * SWA_DH
    sink_b = jnp.broadcast_to(sink.astype(F32).reshape(SWA_KV_HEADS, SWA_GROUP, 1), (SWA_KV_HEADS, SWA_GROUP, LANES))
    kcb, vcb = col_k // SWA_DH, col_v // SWA_DH

    def win(cb, w):
        return pl.BlockSpec((qb, SWA_DH), lambda kv, r: (jnp.clip(r + w, 0, nb - 1), cb + kv))

    return pl.pallas_call(
        functools.partial(_swa_kernel, nc=nc, t=t),
        grid=(SWA_KV_HEADS, nb),
        in_specs=[pl.BlockSpec((1, SWA_GROUP, LANES), lambda kv, r: (kv, 0, 0)),
                  pl.BlockSpec((qb, gw), lambda kv, r: (r, col_q // gw + kv)),
                  pl.BlockSpec((nc, SWA_DH), lambda kv, r: (0, kcb + kv)),
                  win(kcb, -1), win(kcb, 0), win(kcb, 1),
                  pl.BlockSpec((nc, SWA_DH), lambda kv, r: (0, vcb + kv)),
                  win(vcb, -1), win(vcb, 0), win(vcb, 1)],
        out_specs=pl.BlockSpec((qb, gw), lambda kv, r: (r, kv)),
        out_shape=jax.ShapeDtypeStruct((t, SWA_HEADS * SWA_DH), BF16),
        compiler_params=_cp(("arbitrary", "arbitrary")),
        name="swa_attention",
    )(sink_b, qkv, qkv, qkv, qkv, qkv, qkv, qkv, qkv, qkv)


def _dn_conv_kernel(u_ref, prev_ref, next_ref, cw_ref, o_ref, *, tm, tc, nc, t, n_q, n_qk):
    i = pl.program_id(0)
    j = pl.program_id(1)
    s = CONV_STRIP

    @pl.when(j < n_qk)
    def _():
        scale = jnp.where(j < n_q, DN_DK ** -0.5, 1.0).astype(F32)
        for n, conv in enumerate(_conv3_strips(u_ref, prev_ref, next_ref, cw_ref, i, tm, nc, t)):
            y = _silu(conv)
            for c0 in range(0, tc, DN_DK):
                a = y[:, c0:c0 + DN_DK]
                ss = jnp.sum(a * a, axis=-1, keepdims=True)
                o_ref[n * s:(n + 1) * s, c0:c0 + DN_DK] = (a * lax.rsqrt(ss + EPS) * scale).astype(o_ref.dtype)

    @pl.when(j >= n_qk)
    def _():
        for n, conv in enumerate(_conv3_strips(u_ref, prev_ref, next_ref, cw_ref, i, tm, nc, t)):
            o_ref[n * s:(n + 1) * s, :] = _silu(conv).astype(o_ref.dtype)


def dn_conv(u, conv_w, nc, n_cols):
    t = u.shape[0]
    tm = _pick(t, (640, 320, 256, 128))
    tc = 1024
    qk_cols = DN_QK_HEADS * DN_DK
    return pl.pallas_call(
        functools.partial(_dn_conv_kernel, tm=tm, tc=tc, nc=nc, t=t, n_q=qk_cols // tc, n_qk=2 * qk_cols // tc),
        grid=(t // tm, n_cols // tc),
        in_specs=_halo_specs(tm, tc, t, 0) + [pl.BlockSpec((3, tc), lambda i, k: (0, k))],
        out_specs=pl.BlockSpec((tm, tc), lambda i, k: (i, k)),
        out_shape=jax.ShapeDtypeStruct((t, n_cols), BF16),
        compiler_params=_cp(("arbitrary", "arbitrary")),
        name="dn_conv",
    )(u, u, u, conv_w)


def _split3(x):
    hi = x.astype(BF16)
    r1 = x - hi.astype(F32)
    mid = r1.astype(BF16)
    lo = (r1 - mid.astype(F32)).astype(BF16)
    return hi, mid, lo


def _delta_kernel(gf_ref, gr_ref, nega_ref, bias_ref, qf_ref, kf_ref, vf_ref, qr_ref, kr_ref, vr_ref,
                       of_ref, or_ref, s_ref):
    s_idx = pl.program_id(0)
    c = DN_CHUNK
    nt = (((1,), (1,)), ((), ()))
    tn = (((0,), (0,)), ((), ()))

    @pl.when(s_idx == 0)
    def _():
        s_ref[...] = jnp.zeros(s_ref.shape, F32)

    def dot(x, y):
        return jnp.dot(x, y, preferred_element_type=F32)

    ri = lax.broadcasted_iota(jnp.int32, (c, c), 0)
    ci = lax.broadcasted_iota(jnp.int32, (c, c), 1)
    eye = ri == ci
    eye_f = eye.astype(F32)
    nh = DN_V_HEADS
    ctx = []
    for d, (g_ref, q_ref, k_ref, v_ref, o_ref) in enumerate(((gf_ref, qf_ref, kf_ref, vf_ref, of_ref),
                                                           (gr_ref, qr_ref, kr_ref, vr_ref, or_ref))):
        shift = (2 * LANES - d * 2 * nh) % LANES
        gt = pltpu.roll(g_ref[...], shift, 1) if shift else g_ref[...]
        nega = pltpu.roll(nega_ref[...], shift, 1) if shift else nega_ref[...]
        bias = pltpu.roll(bias_ref[...], shift, 1) if shift else bias_ref[...]
        z = gt + bias[0:1, :]
        g_all = nega[0:1, :] * (jnp.maximum(z, 0.0) + jnp.log(1.0 + jnp.exp(-jnp.abs(z))))
        order = (ri - ci) * (1 if d == 0 else -1)
        incl = order >= 0
        csum = incl.astype(BF16)
        hi, mid, lo = _split3(g_all)
        ctx.append(dict(beta=jax.nn.sigmoid(gt), gam=dot(csum, hi) + dot(csum, mid) + dot(csum, lo),
                        tot=jnp.sum(g_all, axis=0, keepdims=True), incl=incl, strict=order > 0,
                        q=q_ref, k=k_ref, v=v_ref, o=o_ref))

    heads = [(d, hv) for d in range(2) for hv in range(nh)]
    pairs = [(d, hq) for d in range(2) for hq in range(DN_QK_HEADS)]
    kb = {p: ctx[p[0]]["k"][:, p[1] * DN_DK:(p[1] + 1) * DN_DK] for p in pairs}
    qb = {p: ctx[p[0]]["q"][:, p[1] * DN_DK:(p[1] + 1) * DN_DK] for p in pairs}
    kq = {p: lax.dot_general(jnp.concatenate([kb[p], qb[p]], axis=0), kb[p], nt, preferred_element_type=F32)
          for p in pairs}
    kf = {p: kb[p].astype(F32) for p in pairs}
    qf = {p: qb[p].astype(F32) for p in pairs}
    qk_of = {h: (h[0], h[1] // 2) for h in heads}
    beta = {h: ctx[h[0]]["beta"][:, h[1]:h[1] + 1] for h in heads}
    gam = {h: ctx[h[0]]["gam"][:, nh + h[1]:nh + h[1] + 1] for h in heads}
    tot = {h: ctx[h[0]]["tot"][:, nh + h[1]:nh + h[1] + 1] for h in heads}
    incl = {h: ctx[h[0]]["incl"] for h in heads}
    gam_row = {h: jnp.sum(jnp.where(eye, gam[h], 0.0), axis=0, keepdims=True) for h in heads}
    dec = {h: jnp.where(incl[h], jnp.exp(jnp.where(incl[h], gam[h] - gam_row[h], 0.0)), 0.0) for h in heads}
    a = {h: jnp.where(ctx[h[0]]["strict"], beta[h] * kq[qk_of[h]][:c] * dec[h], 0.0) for h in heads}
    pw = {h: -a[h] for h in heads}
    tinv = {h: eye_f + pw[h] for h in heads}
    for _ in range(int(math.log2(c)) - 2):
        pwb = {h: pw[h].astype(BF16) for h in heads}
        pw = {h: dot(pwb[h], pwb[h]) for h in heads}
        tinv = {h: tinv[h] + dot(tinv[h].astype(BF16), pw[h].astype(BF16)) for h in heads}
    tb = {h: tinv[h].astype(BF16) for h in heads}
    resid = {h: eye_f - tinv[h] - dot(a[h].astype(BF16), tb[h]) for h in heads}
    tinv = {h: tinv[h] + dot(tb[h], resid[h].astype(BF16)) for h in heads}
    tb = {h: tinv[h].astype(BF16) for h in heads}
    eg = {h: jnp.exp(gam[h]) for h in heads}
    vf = {h: ctx[h[0]]["v"][:, h[1] * DN_DV:(h[1] + 1) * DN_DV].astype(F32) for h in heads}
    rhs = {h: jnp.concatenate([kf[qk_of[h]] * (beta[h] * eg[h]), vf[h] * beta[h]], axis=1).astype(BF16) for h in heads}
    sol = {h: dot(tb[h], rhs[h]) for h in heads}
    aqk = {h: jnp.where(incl[h], kq[qk_of[h]][c:] * dec[h], 0.0).astype(BF16) for h in heads}
    wq = {h: jnp.concatenate([sol[h][:, :DN_DK], qf[qk_of[h]] * eg[h]], axis=0).astype(BF16) for h in heads}
    kd = {h: (kf[qk_of[h]] * jnp.exp(tot[h] - gam[h])).astype(BF16) for h in heads}
    state = {h: s_ref[h[0] * nh + h[1]] for h in heads}
    ws = {h: dot(wq[h], state[h].astype(BF16)) for h in heads}
    vnb = {h: (sol[h][:, DN_DK:] - ws[h][:c]).astype(BF16) for h in heads}
    o = {h: ws[h][c:] + dot(aqk[h], vnb[h]) for h in heads}
    upd = {h: lax.dot_general(kd[h], vnb[h], tn, preferred_element_type=F32) for h in heads}
    for h in heads:
        s_ref[h[0] * nh + h[1]] = jnp.exp(tot[h]) * state[h] + upd[h]
        ctx[h[0]]["o"][:, h[1] * DN_DV:(h[1] + 1) * DN_DV] = o[h].astype(of_ref.dtype)


def gated_delta(qkvc, gates, nega, bias, nc):
    t = qkvc.shape[0]
    c = DN_CHUNK
    nch = t // c
    ncc = nc // c
    qw = DN_QK_HEADS * DN_DK
    vw = DN_V_HEADS * DN_DV

    def rev(s):
        return jnp.where(s < ncc, ncc - 1 - s, nch - 1 - (s - ncc))

    def specs(chunk):
        return [pl.BlockSpec((c, qw), lambda s: (chunk(s), 0)),
                pl.BlockSpec((c, qw), lambda s: (chunk(s), 1)),
                pl.BlockSpec((c, vw), lambda s: (chunk(s), 2 * qw // vw))]

    fwd = lambda s: s
    out = jax.ShapeDtypeStruct((t, vw), BF16)
    return pl.pallas_call(
        _delta_kernel,
        grid=(nch,),
        in_specs=[pl.BlockSpec((c, LANES), lambda s: (s, 0)),
                  pl.BlockSpec((c, LANES), lambda s: (rev(s), 0)),
                  pl.BlockSpec((8, LANES), lambda s: (0, 0)),
                  pl.BlockSpec((8, LANES), lambda s: (0, 0))] + specs(fwd) + specs(rev),
        out_specs=[pl.BlockSpec((c, vw), lambda s: (s, 0)), pl.BlockSpec((c, vw), lambda s: (rev(s), 0))],
        out_shape=[out, out],
        scratch_shapes=[pltpu.VMEM((2 * DN_V_HEADS, DN_DK, DN_DV), F32)],
        compiler_params=_cp(("arbitrary",)),
        name="gated_delta",
    )(gates, gates, nega, bias, qkvc, qkvc, qkvc, qkvc, qkvc, qkvc)


def _attn_out_prologue(i, k, oa_ref, ob_ref, *, n_a):
    return jnp.where(k < n_a, oa_ref[...], ob_ref[...])


def _dn_out_prologue(i, k, of_ref, or_ref, z_ref, g_ref, *, tk):
    tm = z_ref.shape[0]
    rs = math.gcd(tm, 64)
    strips = []
    for r0 in range(0, tm, rs):
        parts = []
        for c0 in range(0, tk, DN_DV):
            a = of_ref[r0:r0 + rs, c0:c0 + DN_DV].astype(F32) + or_ref[r0:r0 + rs, c0:c0 + DN_DV].astype(F32)
            ms = jnp.mean(a * a, axis=-1, keepdims=True)
            z = z_ref[r0:r0 + rs, c0:c0 + DN_DV].astype(F32)
            parts.append((a * lax.rsqrt(ms + EPS) * g_ref[...] * _silu(z)).astype(BF16))
        strips.append(jnp.concatenate(parts, axis=-1))
    return jnp.concatenate(strips, axis=0)


def _ffn_prologue(i, k, ug, pg, ng, uu, pu, nu, cwg, cwu, *, tm, nc, t):
    gates = _conv3_strips(ug, pg, ng, cwg, i, tm, nc, t)
    ups = _conv3_strips(uu, pu, nu, cwu, i, tm, nc, t)
    return jnp.concatenate([(_silu(g) * u).astype(BF16) for g, u in zip(gates, ups)], axis=0)


def _final_norm_kernel(x_ref, g_ref, o_ref):
    x = x_ref[...]
    ms = jnp.mean(x * x, axis=-1, keepdims=True)
    o_ref[...] = x * lax.rsqrt(ms + EPS) * g_ref[...]


def final_norm(x, g, nc):
    t, d = x.shape
    tm = math.gcd(nc, 256)
    off = nc // tm
    return pl.pallas_call(
        _final_norm_kernel,
        grid=((t - nc) // tm,),
        in_specs=[pl.BlockSpec((tm, d), lambda i: (i + off, 0)), pl.BlockSpec((1, d), lambda i: (0, 0))],
        out_specs=pl.BlockSpec((tm, d), lambda i: (i, 0)),
        out_shape=jax.ShapeDtypeStruct((t - nc, d), F32),
        compiler_params=_cp(("arbitrary",)),
        name="final_norm",
    )(x, g.reshape(1, d))


def _rope_tables(n, nc):
    rows = n // GRID_W
    inv = ROPE_BASE ** (-jnp.arange(ROPE_AXIS_FREQS, dtype=F32) / ROPE_AXIS_FREQS)
    ar = jnp.arange(rows, dtype=F32)[:, None] * inv
    ac = jnp.arange(GRID_W, dtype=F32)[:, None] * inv

    def per_token(row_table, col_table):
        r = jnp.broadcast_to(row_table[:, None, :], (rows, GRID_W, ROPE_AXIS_FREQS)).reshape(n, ROPE_AXIS_FREQS)
        c = jnp.broadcast_to(col_table[None, :, :], (rows, GRID_W, ROPE_AXIS_FREQS)).reshape(n, ROPE_AXIS_FREQS)
        return r, c

    cr, cc = per_token(jnp.cos(ar), jnp.cos(ac))
    sr, sc = per_token(jnp.sin(ar), jnp.sin(ac))
    cos = jnp.concatenate([cr, cr, cc, cc], axis=-1)
    sin = jnp.concatenate([-sr, sr, -sc, sc], axis=-1)
    cos = jnp.concatenate([jnp.ones((nc, LANES), F32), cos], axis=0)
    sin = jnp.concatenate([jnp.zeros((nc, LANES), F32), sin], axis=0)
    return cos, sin


def kernel(x, c, ctx, c_ctx, w_mod, b_mod, norm1_g, norm2_g, attn_w_in, diff_lambda, diff_subln_g, swa_sink,
           attn_w_out, dn_w_in, dn_conv_w, dn_a_log, dn_dt_bias, dn_norm_g, dn_w_out, ffn_w_up, ffn_conv_w,
           ffn_w_down, final_norm_g):
    assert x.shape[0] == 1, "single-sequence kernel"
    n, d = x.shape[1], x.shape[2]
    nc = ctx.shape[1]
    t = nc + n
    depth = w_mod.shape[0]
    xs = jnp.concatenate([ctx[0], x[0]], axis=0)
    cc = jnp.zeros((8, d), F32).at[0].set(c[0]).at[1].set(c_ctx)
    mods = mod_vectors(cc, w_mod, b_mod)
    cos, sin = _rope_tables(n, nc)

    da, sb_, skv = DIFF_HEADS * 2 * DIFF_DH, SWA_HEADS * SWA_DH, SWA_KV_HEADS * SWA_DH
    col_qa, col_ka, col_qb = 0, da, 2 * da
    col_va = col_qb + sb_
    col_kb = col_va + da
    col_vb = col_kb + skv
    n_attn = col_vb + skv
    cols = jnp.arange(n_attn)
    cmask = ((cols < col_va) | ((cols >= col_kb) & (cols < col_vb))).astype(F32).reshape(1, n_attn)
    cscale = jnp.where(cols < col_ka, DIFF_DH ** -0.5 * LOG2E,
                       jnp.where((cols >= col_qb) & (cols < col_va), SWA_DH ** -0.5, 1.0))
    cscale = cscale.astype(F32).reshape(1, n_attn)

    tm_p = _pick(t, (640, 320, 256, 128))
    hidden = ffn_w_down.shape[1]
    tk_f = _pick(hidden, (1408, 512, 256, 128))
    qkv_cols = 2 * DN_QK_HEADS * DN_DK + DN_V_HEADS * DN_DV
    dn_out = DN_V_HEADS * DN_DV

    for layer in range(depth):
        m = mods[layer]
        sh1, sc1, g1, sh2, sc2, g2 = (m[:, j * d:(j + 1) * d] for j in range(6))
        i = layer // 2
        if layer % 2 == 0:
            lam_init = 0.8 - 0.6 * math.exp(-0.3 * layer)
            w = attn_w_in[i]
            w = jnp.concatenate([w[:, 0:2 * da], w[:, 3 * da:3 * da + sb_], w[:, 2 * da:3 * da],
                                 w[:, 3 * da + sb_:]], axis=1).astype(BF16)
            qkv = norm_mod_matmul(xs, norm1_g[layer], sh1, sc1, w, nc, rope_args=(cos, sin, cmask, cscale),
                                  tn_candidates=(512, 256, 128))
            oa = diff_attention(qkv, diff_lambda[i], diff_subln_g[i], nc, lam_init, col_qa, col_ka, col_va)
            ob = swa_attention(qkv, swa_sink[i], nc, col_qb, col_kb, col_vb)
            tk = da
            n_a = da // tk
            xs = proj_residual(
                functools.partial(_attn_out_prologue, n_a=n_a), [oa, ob],
                [pl.BlockSpec((tm_p, tk), lambda r, k: (r, jnp.minimum(k, n_a - 1))),
                 pl.BlockSpec((tm_p, tk), lambda r, k: (r, jnp.maximum(k - n_a, 0)))],
                attn_w_out[i].astype(BF16), xs, g1, nc, tm_p, tk)
        else:
            w = dn_w_in[i]
            u = norm_mod_matmul(xs, norm1_g[layer], sh1, sc1, w[:, :qkv_cols + dn_out].astype(BF16), nc)
            gates = norm_mod_matmul(xs, norm1_g[layer], sh1, sc1, w[:, qkv_cols + dn_out:].astype(BF16), nc, out_dtype=F32)
            qkvc = dn_conv(u, dn_conv_w[i], nc, qkv_cols)
            zero = jnp.zeros((2, DN_V_HEADS), F32)
            nega = jnp.stack([zero, -jnp.exp(dn_a_log[i].astype(F32))], axis=1).reshape(1, 4 * DN_V_HEADS)
            bias = jnp.stack([zero, dn_dt_bias[i].astype(F32)], axis=1).reshape(1, 4 * DN_V_HEADS)
            o_f, o_r = gated_delta(qkvc, gates, jnp.broadcast_to(nega, (8, LANES)),
                                   jnp.broadcast_to(bias, (8, LANES)), nc)
            tk = 1024
            zoff = qkv_cols // tk
            xs = proj_residual(
                functools.partial(_dn_out_prologue, tk=tk), [o_f, o_r, u, dn_norm_g[i].reshape(1, DN_DV)],
                [pl.BlockSpec((tm_p, tk), lambda r, k: (r, k)),
                 pl.BlockSpec((tm_p, tk), lambda r, k: (r, k)),
                 pl.BlockSpec((tm_p, tk), lambda r, k: (r, zoff + k)),
                 pl.BlockSpec((1, DN_DV), lambda r, k: (0, 0))],
                dn_w_out[i].astype(BF16), xs, g1, nc, tm_p, tk)
        uf = norm_mod_matmul(xs, norm2_g[layer], sh2, sc2, ffn_w_up[layer].astype(BF16), nc)
        cw = ffn_conv_w[layer]
        xs = proj_residual(
            functools.partial(_ffn_prologue, tm=tm_p, nc=nc, t=t), [uf, uf, uf, uf, uf, uf, cw, cw],
            _halo_specs(tm_p, tk_f, t, 0) + _halo_specs(tm_p, tk_f, t, hidden // tk_f)
            + [pl.BlockSpec((3, tk_f), lambda r, k: (0, k)),
               pl.BlockSpec((3, tk_f), lambda r, k: (0, hidden // tk_f + k))],
            ffn_w_down[layer].astype(BF16), xs, g2, nc, tm_p, tk_f)
    return final_norm(xs, final_norm_g, nc)[None]
```

```python
import functools
import math

import jax
import jax.numpy as jnp
from jax import lax
from jax.experimental import pallas as pl
from jax.experimental.pallas import tpu as pltpu

F32 = jnp.float32
BF16 = jnp.bfloat16

EPS = 1e-6
NEG_INF = -1e30
GRID_W = 64
ROPE_BASE = 10000.0
ROPE_AXIS_FREQS = 32
DIFF_HEADS = 4
DIFF_DH = 128
SWA_HEADS = 8
SWA_KV_HEADS = 2
SWA_GROUP = SWA_HEADS // SWA_KV_HEADS
SWA_DH = 128
WINDOW = 128
DN_QK_HEADS = 16
DN_V_HEADS = 32
DN_DK = 128
DN_DV = 128
DN_CHUNK = 64
LANES = 128
HALO = 16
VMEM_LIMIT = 56 * 1024 * 1024


def _cp(sem):
    return pltpu.CompilerParams(dimension_semantics=sem, vmem_limit_bytes=VMEM_LIMIT)


def _pick(n, candidates):
    for c in candidates:
        if n % c == 0:
            return c
    raise ValueError(f"no tile for {n} among {candidates}")


def _silu(x):
    return x * jax.nn.sigmoid(x)


def _rows(i, tm):
    return i * tm + lax.broadcasted_iota(jnp.int32, (tm, 1), 0)


def _mod_kernel(a_ref, w_ref, b_ref, o_ref):
    a = _silu(a_ref[...]).astype(BF16)
    w = w_ref[0].astype(BF16)
    o_ref[0] = jnp.dot(a, w, preferred_element_type=F32) + b_ref[0]


def mod_vectors(cc, w_mod, b_mod):
    depth, d, n = w_mod.shape
    tn = _pick(n, (1024, 512, 256, 128))
    return pl.pallas_call(
        _mod_kernel,
        grid=(depth, n // tn),
        in_specs=[pl.BlockSpec((8, d), lambda l, j: (0, 0)),
                  pl.BlockSpec((1, d, tn), lambda l, j: (l, 0, j)),
                  pl.BlockSpec((1, 1, tn), lambda l, j: (l, 0, j))],
        out_specs=pl.BlockSpec((1, 8, tn), lambda l, j: (l, 0, j)),
        out_shape=jax.ShapeDtypeStruct((depth, 8, n), F32),
        compiler_params=_cp(("arbitrary", "arbitrary")),
        name="mod_vectors",
    )(cc, w_mod, b_mod.reshape(depth, 1, n))


def _nmm_kernel(x_ref, g_ref, sh_ref, sc_ref, w_ref, *rest, tm, tn, nc, rope):
    if rope:
        cos_ref, sin_ref, cmask_ref, cscale_ref, o_ref, h_ref = rest
    else:
        o_ref, h_ref = rest
    i = pl.program_id(0)
    j = pl.program_id(1)

    @pl.when(j == 0)
    def _():
        sub = math.gcd(math.gcd(tm, 128), nc)
        gain = g_ref[...] * (1.0 + sc_ref[0:2, :])
        for r0 in range(0, tm, sub):
            x = x_ref[r0:r0 + sub, :]
            ms = jnp.mean(x * x, axis=-1, keepdims=True)
            is_ctx = i * tm + r0 < nc
            gs = jnp.where(is_ctx, gain[1:2, :], gain[0:1, :])
            sh = jnp.where(is_ctx, sh_ref[1:2, :], sh_ref[0:1, :])
            h_ref[r0:r0 + sub, :] = (x * lax.rsqrt(ms + EPS) * gs + sh).astype(BF16)

    acc = jnp.dot(h_ref[...], w_ref[...], preferred_element_type=F32)
    if not rope:
        o_ref[...] = acc.astype(o_ref.dtype)
        return
    rs = math.gcd(tm, 128)
    lane = lax.broadcasted_iota(jnp.int32, (rs, LANES), 1)
    first = (lane % (2 * ROPE_AXIS_FREQS)) < ROPE_AXIS_FREQS
    for r0 in range(0, tm, rs):
        cos = cos_ref[r0:r0 + rs, :]
        sin = sin_ref[r0:r0 + rs, :]
        for c0 in range(0, tn, LANES):
            a = acc[r0:r0 + rs, c0:c0 + LANES]
            partner = jnp.where(first, pltpu.roll(a, LANES - ROPE_AXIS_FREQS, 1), pltpu.roll(a, ROPE_AXIS_FREQS, 1))
            roped = a * cos + partner * sin
            a = jnp.where(cmask_ref[:, c0:c0 + LANES] > 0.5, roped, a) * cscale_ref[:, c0:c0 + LANES]
            o_ref[r0:r0 + rs, c0:c0 + LANES] = a.astype(o_ref.dtype)


def norm_mod_matmul(x, g, sh, sc, w, nc, rope_args=None, tn_candidates=(1024, 512, 256, 128), out_dtype=BF16):
    t, d = x.shape
    n = w.shape[1]
    tm = _pick(t, (1280, 640, 512, 320, 256, 128))
    tn = _pick(n, tn_candidates)
    rope = rope_args is not None
    in_specs = [pl.BlockSpec((tm, d), lambda i, j: (i, 0)),
                pl.BlockSpec((1, d), lambda i, j: (0, 0)),
                pl.BlockSpec((8, d), lambda i, j: (0, 0)),
                pl.BlockSpec((8, d), lambda i, j: (0, 0)),
                pl.BlockSpec((d, tn), lambda i, j: (0, j))]
    args = [x, g.reshape(1, d), sh, sc, w]
    if rope:
        cos, sin, cmask, cscale = rope_args
        in_specs += [pl.BlockSpec((tm, LANES), lambda i, j: (i, 0)),
                     pl.BlockSpec((tm, LANES), lambda i, j: (i, 0)),
                     pl.BlockSpec((1, tn), lambda i, j: (0, j)),
                     pl.BlockSpec((1, tn), lambda i, j: (0, j))]
        args += [cos, sin, cmask, cscale]
    return pl.pallas_call(
        functools.partial(_nmm_kernel, tm=tm, tn=tn, nc=nc, rope=rope),
        grid=(t // tm, n // tn),
        in_specs=in_specs,
        out_specs=pl.BlockSpec((tm, tn), lambda i, j: (i, j)),
        out_shape=jax.ShapeDtypeStruct((t, n), out_dtype),
        scratch_shapes=[pltpu.VMEM((tm, d), BF16)],
        compiler_params=_cp(("arbitrary", "arbitrary")),
        name="norm_mod_matmul",
    )(*args)


def _proj_res_kernel(*refs, prologue, n_lhs, nk, tm, nc):
    lhs = refs[:n_lhs]
    w_ref, x_ref, gate_ref, o_ref, acc_ref = refs[n_lhs:]
    i = pl.program_id(0)
    k = pl.program_id(1)
    a = prologue(i, k, *lhs)
    p = jnp.dot(a, w_ref[...], preferred_element_type=F32)

    @pl.when(k == 0)
    def _():
        acc_ref[...] = p

    @pl.when(k > 0)
    def _():
        acc_ref[...] += p

    @pl.when(k == nk - 1)
    def _():
        is_ctx = _rows(i, tm) < nc
        gate = jnp.where(is_ctx, gate_ref[1:2, :], gate_ref[0:1, :])
        o_ref[...] = x_ref[...] + gate * acc_ref[...]


def proj_residual(prologue, lhs_args, lhs_specs, w, x, gate, nc, tm, tk):
    t, d = x.shape
    kdim = w.shape[0]
    nk = kdim // tk
    in_specs = list(lhs_specs) + [pl.BlockSpec((tk, d), lambda i, k: (k, 0)),
                                  pl.BlockSpec((tm, d), lambda i, k: (i, 0)),
                                  pl.BlockSpec((8, d), lambda i, k: (0, 0))]
    return pl.pallas_call(
        functools.partial(_proj_res_kernel, prologue=prologue, n_lhs=len(lhs_args), nk=nk, tm=tm, nc=nc),
        grid=(t // tm, nk),
        in_specs=in_specs,
        out_specs=pl.BlockSpec((tm, d), lambda i, k: (i, 0)),
        out_shape=jax.ShapeDtypeStruct((t, d), F32),
        scratch_shapes=[pltpu.VMEM((tm, d), F32)],
        compiler_params=_cp(("arbitrary", "arbitrary")),
        name="proj_residual",
    )(*lhs_args, w, x, gate)


def _halo_specs(tm, tk, t, col_off):
    nh = t // HALO
    per = tm // HALO
    return [pl.BlockSpec((tm, tk), lambda i, k: (i, col_off + k)),
            pl.BlockSpec((HALO, tk), lambda i, k: (jnp.maximum(i * per - 1, 0), col_off + k)),
            pl.BlockSpec((HALO, tk), lambda i, k: (jnp.minimum((i + 1) * per, nh - 1), col_off + k))]


CONV_STRIP = 128
SUBLANES = 8


def _patch_rows(x, g0, patch):
    parts = [x[:g0], patch(x[g0:g0 + SUBLANES]), x[g0 + SUBLANES:]]
    return jnp.concatenate([p for p in parts if p.shape[0]], axis=0)


def _conv3_strips(u_ref, prev_ref, next_ref, cw_ref, i, tm, nc, t):
    s = CONV_STRIP
    sub = lax.broadcasted_iota(jnp.int32, (SUBLANES, 1), 0)
    w = cw_ref[...]
    w0, w1, w2 = w[0:1, :], w[1:2, :], w[2:3, :]
    halo_prev = jnp.where(i == 0, 0.0, prev_ref[...].astype(F32)[HALO - 1:HALO, :])
    halo_next = jnp.where(i == t // tm - 1, 0.0, next_ref[...].astype(F32)[0:1, :])
    n = tm // s
    strips = [u_ref[j * s:(j + 1) * s, :].astype(F32) for j in range(n)]
    first_blk, first_row = nc // tm, nc % tm
    last_blk, last_row = (nc - 1) // tm, (nc - 1) % tm
    for j in range(n):
        u = strips[j]
        before = halo_prev if j == 0 else strips[j - 1][s - 1:s, :]
        after = halo_next if j == n - 1 else strips[j + 1][0:1, :]
        up = _patch_rows(pltpu.roll(u, 1, 0), 0, lambda g: jnp.where(sub == 0, before, g))
        un = _patch_rows(pltpu.roll(u, s - 1, 0), s - SUBLANES, lambda g: jnp.where(sub == SUBLANES - 1, after, g))
        if 0 < nc < t and first_row // s == j:
            r = first_row % s
            up = _patch_rows(up, r // SUBLANES * SUBLANES,
                             lambda g: jnp.where((sub == r % SUBLANES) & (i == first_blk), 0.0, g))
        if 0 < nc < t and last_row // s == j:
            r = last_row % s
            un = _patch_rows(un, r // SUBLANES * SUBLANES,
                             lambda g: jnp.where((sub == r % SUBLANES) & (i == last_blk), 0.0, g))
        yield up * w0 + u * w1 + un * w2


ATTN_STRIP = 32
LOG2E = math.log2(math.e)


def _diff_attn_kernel(lam_ref, g_ref, q_ref, k_ref, v_ref, o_ref, m_ref, l_ref, acc_ref, p_ref, alpha_ref,
                      *, tq, tk, nc, nk, lam_init):
    i = pl.program_id(1)
    kk = pl.program_id(2)

    @pl.when(kk == 0)
    def _():
        m_ref[...] = jnp.full(m_ref.shape, NEG_INF, F32)
        l_ref[...] = jnp.zeros(l_ref.shape, F32)
        acc_ref[...] = jnp.zeros(acc_ref.shape, F32)

    def step(masked):
        nt = (((1,), (1,)), ((), ()))
        v = v_ref[...]
        for c in range(2):
            cols = slice(c * DIFF_DH, (c + 1) * DIFF_DH)
            s_all = lax.dot_general(q_ref[:, cols], k_ref[:, cols], nt, preferred_element_type=F32)
            for r0 in range(0, tq, ATTN_STRIP):
                rows = slice(r0, r0 + ATTN_STRIP)
                s = s_all[rows, :]
                if masked:
                    qrow = i * tq + r0 + lax.broadcasted_iota(jnp.int32, (ATTN_STRIP, 1), 0)
                    kcol = kk * tk + lax.broadcasted_iota(jnp.int32, (1, tk), 1)
                    s = jnp.where((qrow >= nc) | (kcol < nc), s, NEG_INF)
                m_prev = m_ref[c, rows, :]
                m_new = jnp.maximum(m_prev, jnp.max(s, axis=-1, keepdims=True))
                alpha = jnp.exp2(m_prev - m_new)
                p = jnp.exp2(s - m_new)
                l_ref[c, rows, :] = alpha * l_ref[c, rows, :] + jnp.sum(p, axis=-1, keepdims=True)
                m_ref[c, rows, :] = m_new
                alpha_ref[c, rows, :] = alpha
                p_ref[c, rows, :] = p.astype(BF16)
            acc_ref[c] = alpha_ref[c] * acc_ref[c] + jnp.dot(p_ref[c], v, preferred_element_type=F32)

    has_ctx_rows = i * tq < nc

    @pl.when(has_ctx_rows)
    def _():
        step(True)

    @pl.when(jnp.logical_not(has_ctx_rows))
    def _():
        step(False)

    @pl.when(kk == nk - 1)
    def _():
        lf = lam_ref[...]
        lam = (jnp.exp(jnp.sum(lf[0:1] * lf[1:2], axis=-1, keepdims=True))
               - jnp.exp(jnp.sum(lf[2:3] * lf[3:4], axis=-1, keepdims=True)) + lam_init)
        o = acc_ref[0] / l_ref[0] - lam * (acc_ref[1] / l_ref[1])
        ms = jnp.mean(o * o, axis=-1, keepdims=True)
        o_ref[...] = (o * lax.rsqrt(ms + EPS) * g_ref[...] * (1.0 - lam_init)).astype(o_ref.dtype)


def diff_attention(qkv, lam_vec, subln_g, nc, lam_init, col_q, col_k, col_v):
    t = qkv.shape[0]
    tq = _pick(t, (1280, 640, 256, 128))
    tk = _pick(t, (1280, 256, 128))
    nk = t // tk
    hw = 2 * DIFF_DH
    return pl.pallas_call(
        functools.partial(_diff_attn_kernel, tq=tq, tk=tk, nc=nc, nk=nk, lam_init=lam_init),
        grid=(DIFF_HEADS, t // tq, nk),
        in_specs=[pl.BlockSpec((4, DIFF_DH), lambda h, i, k: (0, 0)),
                  pl.BlockSpec((1, hw), lambda h, i, k: (0, 0)),
                  pl.BlockSpec((tq, hw), lambda h, i, k: (i, col_q // hw + h)),
                  pl.BlockSpec((tk, hw), lambda h, i, k: (k, col_k // hw + h)),
                  pl.BlockSpec((tk, hw), lambda h, i, k: (k, col_v // hw + h))],
        out_specs=pl.BlockSpec((tq, hw), lambda h, i, k: (i, h)),
        out_shape=jax.ShapeDtypeStruct((t, DIFF_HEADS * hw), BF16),
        scratch_shapes=[pltpu.VMEM((2, tq, 1), F32), pltpu.VMEM((2, tq, 1), F32), pltpu.VMEM((2, tq, hw), F32),
                        pltpu.VMEM((2, tq, tk), BF16), pltpu.VMEM((2, tq, 1), F32)],
        compiler_params=_cp(("arbitrary", "arbitrary", "arbitrary")),
        name="diff_attention",
    )(lam_vec, subln_g.reshape(1, hw), qkv, qkv, qkv)


def _swa_kernel(sink_ref, q_ref, kc_ref, k0_ref, k1_ref, k2_ref, vc_ref, v0_ref, v1_ref, v2_ref, o_ref,
                *, nc, t):
    r = pl.program_id(1)
    qb = WINDOW
    q = jnp.concatenate([q_ref[:, g * SWA_DH:(g + 1) * SWA_DH] for g in range(SWA_GROUP)], axis=0)
    sk = sink_ref[0]
    sink = jnp.concatenate([jnp.broadcast_to(sk[g:g + 1, 0:1], (qb, 1)) for g in range(SWA_GROUP)], axis=0)
    nt = (((1,), (1,)), ((), ()))
    local_q = lax.broadcasted_iota(jnp.int32, (qb, 1), 0)
    qrow = jnp.concatenate([r * qb + local_q] * SWA_GROUP, axis=0)
    s_list = [lax.dot_general(q, kc_ref[...], nt, preferred_element_type=F32)]
    for w, k_ref in zip((-1, 0, 1), (k0_ref, k1_ref, k2_ref)):
        s = lax.dot_general(q, k_ref[...], nt, preferred_element_type=F32)
        krow = (r + w) * qb + lax.broadcasted_iota(jnp.int32, (1, qb), 1)
        valid = (jnp.abs(qrow - krow) <= WINDOW) & (krow >= nc) & (krow < t) & (qrow >= nc)
        s_list.append(jnp.where(valid, s, NEG_INF))
    m = sink
    for s in s_list:
        m = jnp.maximum(m, jnp.max(s, axis=-1, keepdims=True))
    denom = jnp.exp(sink - m)
    out = jnp.zeros((SWA_GROUP * qb, SWA_DH), F32)
    for s, v_ref in zip(s_list, (vc_ref, v0_ref, v1_ref, v2_ref)):
        e = jnp.exp(s - m)
        denom = denom + jnp.sum(e, axis=-1, keepdims=True)
        out = out + jnp.dot(e.astype(BF16), v_ref[...], preferred_element_type=F32)
    out = out / denom
    for g in range(SWA_GROUP):
        o_ref[:, g * SWA_DH:(g + 1) * SWA_DH] = out[g * qb:(g + 1) * qb, :].astype(o_ref.dtype)


def swa_attention(qkv, sink, nc, col_q, col_k, col_v):
    t = qkv.shape[0]
    qb = WINDOW
    nb = t // qb
    gw = SWA_GROUP * SWA_DH
    sink_b = jnp.broadcast_to(sink.astype(F32).reshape(SWA_KV_HEADS, SWA_GROUP, 1), (SWA_KV_HEADS, SWA_GROUP, LANES))
    kcb, vcb = col_k // SWA_DH, col_v // SWA_DH

    def win(cb, w):
        return pl.BlockSpec((qb, SWA_DH), lambda kv, r: (jnp.clip(r + w, 0, nb - 1), cb + kv))

    return pl.pallas_call(
        functools.partial(_swa_kernel, nc=nc, t=t),
        grid=(SWA_KV_HEADS, nb),
        in_specs=[pl.BlockSpec((1, SWA_GROUP, LANES), lambda kv, r: (kv, 0, 0)),
                  pl.BlockSpec((qb, gw), lambda kv, r: (r, col_q // gw + kv)),
                  pl.BlockSpec((nc, SWA_DH), lambda kv, r: (0, kcb + kv)),
                  win(kcb, -1), win(kcb, 0), win(kcb, 1),
                  pl.BlockSpec((nc, SWA_DH), lambda kv, r: (0, vcb + kv)),
                  win(vcb, -1), win(vcb, 0), win(vcb, 1)],
        out_specs=pl.BlockSpec((qb, gw), lambda kv, r: (r, kv)),
        out_shape=jax.ShapeDtypeStruct((t, SWA_HEADS * SWA_DH), BF16),
        compiler_params=_cp(("arbitrary", "arbitrary")),
        name="swa_attention",
    )(sink_b, qkv, qkv, qkv, qkv, qkv, qkv, qkv, qkv, qkv)


def _dn_conv_kernel(u_ref, prev_ref, next_ref, cw_ref, o_ref, *, tm, tc, nc, t, n_q, n_qk):
    i = pl.program_id(0)
    j = pl.program_id(1)
    s = CONV_STRIP

    @pl.when(j < n_qk)
    def _():
        scale = jnp.where(j < n_q, DN_DK ** -0.5, 1.0).astype(F32)
        for n, conv in enumerate(_conv3_strips(u_ref, prev_ref, next_ref, cw_ref, i, tm, nc, t)):
            y = _silu(conv)
            for c0 in range(0, tc, DN_DK):
                a = y[:, c0:c0 + DN_DK]
                ss = jnp.sum(a * a, axis=-1, keepdims=True)
                o_ref[n * s:(n + 1) * s, c0:c0 + DN_DK] = (a * lax.rsqrt(ss + EPS) * scale).astype(o_ref.dtype)

    @pl.when(j >= n_qk)
    def _():
        for n, conv in enumerate(_conv3_strips(u_ref, prev_ref, next_ref, cw_ref, i, tm, nc, t)):
            o_ref[n * s:(n + 1) * s, :] = _silu(conv).astype(o_ref.dtype)


def dn_conv(u, conv_w, nc, n_cols):
    t = u.shape[0]
    tm = _pick(t, (640, 320, 256, 128))
    tc = 1024
    qk_cols = DN_QK_HEADS * DN_DK
    return pl.pallas_call(
        functools.partial(_dn_conv_kernel, tm=tm, tc=tc, nc=nc, t=t, n_q=qk_cols // tc, n_qk=2 * qk_cols // tc),
        grid=(t // tm, n_cols // tc),
        in_specs=_halo_specs(tm, tc, t, 0) + [pl.BlockSpec((3, tc), lambda i, k: (0, k))],
        out_specs=pl.BlockSpec((tm, tc), lambda i, k: (i, k)),
        out_shape=jax.ShapeDtypeStruct((t, n_cols), BF16),
        compiler_params=_cp(("arbitrary", "arbitrary")),
        name="dn_conv",
    )(u, u, u, conv_w)


def _split3(x):
    hi = x.astype(BF16)
    r1 = x - hi.astype(F32)
    mid = r1.astype(BF16)
    lo = (r1 - mid.astype(F32)).astype(BF16)
    return hi, mid, lo


def _delta_kernel(gf_ref, gr_ref, alog_ref, bias_ref, qf_ref, kf_ref, vf_ref, qr_ref, kr_ref, vr_ref,
                       of_ref, or_ref, s_ref):
    s_idx = pl.program_id(0)
    c = DN_CHUNK
    nt = (((1,), (1,)), ((), ()))
    tn = (((0,), (0,)), ((), ()))

    @pl.when(s_idx == 0)
    def _():
        s_ref[...] = jnp.zeros(s_ref.shape, F32)

    def dot(x, y):
        return jnp.dot(x, y, preferred_element_type=F32)

    ri = lax.broadcasted_iota(jnp.int32, (c, c), 0)
    ci = lax.broadcasted_iota(jnp.int32, (c, c), 1)
    eye = ri == ci
    eye_f = eye.astype(F32)
    nh = DN_V_HEADS
    ctx = []
    for d, (g_ref, q_ref, k_ref, v_ref, o_ref) in enumerate(((gf_ref, qf_ref, kf_ref, vf_ref, of_ref),
                                                           (gr_ref, qr_ref, kr_ref, vr_ref, or_ref))):
        shift = (2 * LANES - d * 2 * nh) % LANES
        gt = pltpu.roll(g_ref[...], shift, 1) if shift else g_ref[...]
        alog = pltpu.roll(alog_ref[...], shift, 1) if shift else alog_ref[...]
        bias = pltpu.roll(bias_ref[...], shift, 1) if shift else bias_ref[...]
        z = gt + bias[0:1, :]
        g_all = -jnp.exp(alog[0:1, :]) * (jnp.maximum(z, 0.0) + jnp.log(1.0 + jnp.exp(-jnp.abs(z))))
        order = (ri - ci) * (1 if d == 0 else -1)
        incl = order >= 0
        csum = incl.astype(BF16)
        hi, mid, lo = _split3(g_all)
        ctx.append(dict(beta=jax.nn.sigmoid(gt), gam=dot(csum, hi) + dot(csum, mid) + dot(csum, lo),
                        tot=jnp.sum(g_all, axis=0, keepdims=True), incl=incl, strict=order > 0,
                        q=q_ref, k=k_ref, v=v_ref, o=o_ref))

    heads = [(d, hv) for d in range(2) for hv in range(nh)]
    pairs = [(d, hq) for d in range(2) for hq in range(DN_QK_HEADS)]
    kb = {p: ctx[p[0]]["k"][:, p[1] * DN_DK:(p[1] + 1) * DN_DK] for p in pairs}
    qb = {p: ctx[p[0]]["q"][:, p[1] * DN_DK:(p[1] + 1) * DN_DK] for p in pairs}
    kq = {p: lax.dot_general(jnp.concatenate([kb[p], qb[p]], axis=0), kb[p], nt, preferred_element_type=F32)
          for p in pairs}
    kf = {p: kb[p].astype(F32) for p in pairs}
    qf = {p: qb[p].astype(F32) for p in pairs}
    qk_of = {h: (h[0], h[1] // 2) for h in heads}
    beta = {h: ctx[h[0]]["beta"][:, h[1]:h[1] + 1] for h in heads}
    gam = {h: ctx[h[0]]["gam"][:, nh + h[1]:nh + h[1] + 1] for h in heads}
    tot = {h: ctx[h[0]]["tot"][:, nh + h[1]:nh + h[1] + 1] for h in heads}
    incl = {h: ctx[h[0]]["incl"] for h in heads}
    gam_row = {h: jnp.sum(jnp.where(eye, gam[h], 0.0), axis=0, keepdims=True) for h in heads}
    dec = {h: jnp.where(incl[h], jnp.exp(jnp.where(incl[h], gam[h] - gam_row[h], 0.0)), 0.0) for h in heads}
    a = {h: jnp.where(ctx[h[0]]["strict"], beta[h] * kq[qk_of[h]][:c] * dec[h], 0.0) for h in heads}
    pw = {h: -a[h] for h in heads}
    tinv = {h: eye_f + pw[h] for h in heads}
    for _ in range(int(math.log2(c)) - 2):
        pwb = {h: pw[h].astype(BF16) for h in heads}
        pw = {h: dot(pwb[h], pwb[h]) for h in heads}
        tinv = {h: tinv[h] + dot(tinv[h].astype(BF16), pw[h].astype(BF16)) for h in heads}
    tb = {h: tinv[h].astype(BF16) for h in heads}
    resid = {h: eye_f - tinv[h] - dot(a[h].astype(BF16), tb[h]) for h in heads}
    tinv = {h: tinv[h] + dot(tb[h], resid[h].astype(BF16)) for h in heads}
    tb = {h: tinv[h].astype(BF16) for h in heads}
    eg = {h: jnp.exp(gam[h]) for h in heads}
    vf = {h: ctx[h[0]]["v"][:, h[1] * DN_DV:(h[1] + 1) * DN_DV].astype(F32) for h in heads}
    rhs = {h: jnp.concatenate([kf[qk_of[h]] * (beta[h] * eg[h]), vf[h] * beta[h]], axis=1).astype(BF16) for h in heads}
    sol = {h: dot(tb[h], rhs[h]) for h in heads}
    aqk = {h: jnp.where(incl[h], kq[qk_of[h]][c:] * dec[h], 0.0).astype(BF16) for h in heads}
    wq = {h: jnp.concatenate([sol[h][:, :DN_DK], qf[qk_of[h]] * eg[h]], axis=0).astype(BF16) for h in heads}
    kd = {h: (kf[qk_of[h]] * jnp.exp(tot[h] - gam[h])).astype(BF16) for h in heads}
    state = {h: s_ref[h[0] * nh + h[1]] for h in heads}
    ws = {h: dot(wq[h], state[h].astype(BF16)) for h in heads}
    vnb = {h: (sol[h][:, DN_DK:] - ws[h][:c]).astype(BF16) for h in heads}
    o = {h: ws[h][c:] + dot(aqk[h], vnb[h]) for h in heads}
    upd = {h: lax.dot_general(kd[h], vnb[h], tn, preferred_element_type=F32) for h in heads}
    for h in heads:
        s_ref[h[0] * nh + h[1]] = jnp.exp(tot[h]) * state[h] + upd[h]
        ctx[h[0]]["o"][:, h[1] * DN_DV:(h[1] + 1) * DN_DV] = o[h].astype(of_ref.dtype)


def gated_delta(qkvc, gates, alog, bias, nc):
    t = qkvc.shape[0]
    c = DN_CHUNK
    nch = t // c
    ncc = nc // c
    qw = DN_QK_HEADS * DN_DK
    vw = DN_V_HEADS * DN_DV

    def rev(s):
        return jnp.where(s < ncc, ncc - 1 - s, nch - 1 - (s - ncc))

    def specs(chunk):
        return [pl.BlockSpec((c, qw), lambda s: (chunk(s), 0)),
                pl.BlockSpec((c, qw), lambda s: (chunk(s), 1)),
                pl.BlockSpec((c, vw), lambda s: (chunk(s), 2 * qw // vw))]

    fwd = lambda s: s
    out = jax.ShapeDtypeStruct((t, vw), BF16)
    return pl.pallas_call(
        _delta_kernel,
        grid=(nch,),
        in_specs=[pl.BlockSpec((c, LANES), lambda s: (s, 0)),
                  pl.BlockSpec((c, LANES), lambda s: (rev(s), 0)),
                  pl.BlockSpec((8, LANES), lambda s: (0, 0)),
                  pl.BlockSpec((8, LANES), lambda s: (0, 0))] + specs(fwd) + specs(rev),
        out_specs=[pl.BlockSpec((c, vw), lambda s: (s, 0)), pl.BlockSpec((c, vw), lambda s: (rev(s), 0))],
        out_shape=[out, out],
        scratch_shapes=[pltpu.VMEM((2 * DN_V_HEADS, DN_DK, DN_DV), F32)],
        compiler_params=_cp(("arbitrary",)),
        name="gated_delta",
    )(gates, gates, alog, bias, qkvc, qkvc, qkvc, qkvc, qkvc, qkvc)


def _attn_out_prologue(i, k, oa_ref, ob_ref, *, n_a):
    return jnp.where(k < n_a, oa_ref[...], ob_ref[...])


def _dn_out_prologue(i, k, of_ref, or_ref, z_ref, g_ref, *, tk):
    tm = z_ref.shape[0]
    rs = math.gcd(tm, 64)
    strips = []
    for r0 in range(0, tm, rs):
        parts = []
        for c0 in range(0, tk, DN_DV):
            a = of_ref[r0:r0 + rs, c0:c0 + DN_DV].astype(F32) + or_ref[r0:r0 + rs, c0:c0 + DN_DV].astype(F32)
            ms = jnp.mean(a * a, axis=-1, keepdims=True)
            z = z_ref[r0:r0 + rs, c0:c0 + DN_DV].astype(F32)
            parts.append((a * lax.rsqrt(ms + EPS) * g_ref[...] * _silu(z)).astype(BF16))
        strips.append(jnp.concatenate(parts, axis=-1))
    return jnp.concatenate(strips, axis=0)


def _ffn_prologue(i, k, ug, pg, ng, uu, pu, nu, cwg, cwu, *, tm, nc, t):
    gates = _conv3_strips(ug, pg, ng, cwg, i, tm, nc, t)
    ups = _conv3_strips(uu, pu, nu, cwu, i, tm, nc, t)
    return jnp.concatenate([(_silu(g) * u).astype(BF16) for g, u in zip(gates, ups)], axis=0)


def _final_norm_kernel(x_ref, g_ref, o_ref):
    x = x_ref[...]
    ms = jnp.mean(x * x, axis=-1, keepdims=True)
    o_ref[...] = x * lax.rsqrt(ms + EPS) * g_ref[...]


def final_norm(x, g, nc):
    t, d = x.shape
    tm = math.gcd(nc, 256)
    off = nc // tm
    return pl.pallas_call(
        _final_norm_kernel,
        grid=((t - nc) // tm,),
        in_specs=[pl.BlockSpec((tm, d), lambda i: (i + off, 0)), pl.BlockSpec((1, d), lambda i: (0, 0))],
        out_specs=pl.BlockSpec((tm, d), lambda i: (i, 0)),
        out_shape=jax.ShapeDtypeStruct((t - nc, d), F32),
        compiler_params=_cp(("arbitrary",)),
        name="final_norm",
    )(x, g.reshape(1, d))


def _rope_tables(n, nc):
    rows = n // GRID_W
    inv = ROPE_BASE ** (-jnp.arange(ROPE_AXIS_FREQS, dtype=F32) / ROPE_AXIS_FREQS)
    ar = jnp.arange(rows, dtype=F32)[:, None] * inv
    ac = jnp.arange(GRID_W, dtype=F32)[:, None] * inv

    def per_token(row_table, col_table):
        r = jnp.broadcast_to(row_table[:, None, :], (rows, GRID_W, ROPE_AXIS_FREQS)).reshape(n, ROPE_AXIS_FREQS)
        c = jnp.broadcast_to(col_table[None, :, :], (rows, GRID_W, ROPE_AXIS_FREQS)).reshape(n, ROPE_AXIS_FREQS)
        return r, c

    cr, cc = per_token(jnp.cos(ar), jnp.cos(ac))
    sr, sc = per_token(jnp.sin(ar), jnp.sin(ac))
    cos = jnp.concatenate([cr, cr, cc, cc], axis=-1)
    sin = jnp.concatenate([-sr, sr, -sc, sc], axis=-1)
    cos = jnp.concatenate([jnp.ones((nc, LANES), F32), cos], axis=0)
    sin = jnp.concatenate([jnp.zeros((nc, LANES), F32), sin], axis=0)
    return cos, sin


def kernel(x, c, ctx, c_ctx, w_mod, b_mod, norm1_g, norm2_g, attn_w_in, diff_lambda, diff_subln_g, swa_sink,
           attn_w_out, dn_w_in, dn_conv_w, dn_a_log, dn_dt_bias, dn_norm_g, dn_w_out, ffn_w_up, ffn_conv_w,
           ffn_w_down, final_norm_g):
    assert x.shape[0] == 1, "single-sequence kernel"
    n, d = x.shape[1], x.shape[2]
    nc = ctx.shape[1]
    t = nc + n
    depth = w_mod.shape[0]
    xs = jnp.concatenate([ctx[0], x[0]], axis=0)
    cc = jnp.zeros((8, d), F32).at[0].set(c[0]).at[1].set(c_ctx)
    mods = mod_vectors(cc, w_mod, b_mod)
    cos, sin = _rope_tables(n, nc)

    da, sb_, skv = DIFF_HEADS * 2 * DIFF_DH, SWA_HEADS * SWA_DH, SWA_KV_HEADS * SWA_DH
    col_qa, col_ka, col_qb = 0, da, 2 * da
    col_va = col_qb + sb_
    col_kb = col_va + da
    col_vb = col_kb + skv
    n_attn = col_vb + skv
    cols = jnp.arange(n_attn)
    cmask = ((cols < col_va) | ((cols >= col_kb) & (cols < col_vb))).astype(F32).reshape(1, n_attn)
    cscale = jnp.where(cols < col_ka, DIFF_DH ** -0.5 * LOG2E,
                       jnp.where((cols >= col_qb) & (cols < col_va), SWA_DH ** -0.5, 1.0))
    cscale = cscale.astype(F32).reshape(1, n_attn)

    tm_p = _pick(t, (640, 320, 256, 128))
    hidden = ffn_w_down.shape[1]
    tk_f = _pick(hidden, (1408, 512, 256, 128))
    qkv_cols = 2 * DN_QK_HEADS * DN_DK + DN_V_HEADS * DN_DV
    dn_out = DN_V_HEADS * DN_DV

    for layer in range(depth):
        m = mods[layer]
        sh1, sc1, g1, sh2, sc2, g2 = (m[:, j * d:(j + 1) * d] for j in range(6))
        i = layer // 2
        if layer % 2 == 0:
            lam_init = 0.8 - 0.6 * math.exp(-0.3 * layer)
            w = attn_w_in[i]
            w = jnp.concatenate([w[:, 0:2 * da], w[:, 3 * da:3 * da + sb_], w[:, 2 * da:3 * da],
                                 w[:, 3 * da + sb_:]], axis=1).astype(BF16)
            qkv = norm_mod_matmul(xs, norm1_g[layer], sh1, sc1, w, nc, rope_args=(cos, sin, cmask, cscale),
                                  tn_candidates=(512, 256, 128))
            oa = diff_attention(qkv, diff_lambda[i], diff_subln_g[i], nc, lam_init, col_qa, col_ka, col_va)
            ob = swa_attention(qkv, swa_sink[i], nc, col_qb, col_kb, col_vb)
            tk = da
            n_a = da // tk
            xs = proj_residual(
                functools.partial(_attn_out_prologue, n_a=n_a), [oa, ob],
                [pl.BlockSpec((tm_p, tk), lambda r, k: (r, jnp.minimum(k, n_a - 1))),
                 pl.BlockSpec((tm_p, tk), lambda r, k: (r, jnp.maximum(k - n_a, 0)))],
                attn_w_out[i].astype(BF16), xs, g1, nc, tm_p, tk)
        else:
            w = dn_w_in[i]
            u = norm_mod_matmul(xs, norm1_g[layer], sh1, sc1, w[:, :qkv_cols + dn_out].astype(BF16), nc)
            gates = norm_mod_matmul(xs, norm1_g[layer], sh1, sc1, w[:, qkv_cols + dn_out:].astype(BF16), nc, out_dtype=F32)
            qkvc = dn_conv(u, dn_conv_w[i], nc, qkv_cols)
            zero = jnp.zeros((2, DN_V_HEADS), F32)
            alog = jnp.stack([zero, dn_a_log[i].astype(F32)], axis=1).reshape(1, 4 * DN_V_HEADS)
            bias = jnp.stack([zero, dn_dt_bias[i].astype(F32)], axis=1).reshape(1, 4 * DN_V_HEADS)
            o_f, o_r = gated_delta(qkvc, gates, jnp.broadcast_to(alog, (8, LANES)),
                                   jnp.broadcast_to(bias, (8, LANES)), nc)
            tk = 1024
            zoff = qkv_cols // tk
            xs = proj_residual(
                functools.partial(_dn_out_prologue, tk=tk), [o_f, o_r, u, dn_norm_g[i].reshape(1, DN_DV)],
                [pl.BlockSpec((tm_p, tk), lambda r, k: (r, k)),
                 pl.BlockSpec((tm_p, tk), lambda r, k: (r, k)),
                 pl.BlockSpec((tm_p, tk), lambda r, k: (r, zoff + k)),
                 pl.BlockSpec((1, DN_DV), lambda r, k: (0, 0))],
                dn_w_out[i].astype(BF16), xs, g1, nc, tm_p, tk)
        uf = norm_mod_matmul(xs, norm2_g[layer], sh2, sc2, ffn_w_up[layer].astype(BF16), nc)
        cw = ffn_conv_w[layer]
        xs = proj_residual(
            functools.partial(_ffn_prologue, tm=tm_p, nc=nc, t=t), [uf, uf, uf, uf, uf, uf, cw, cw],
            _halo_specs(tm_p, tk_f, t, 0) + _halo_specs(tm_p, tk_f, t, hidden // tk_f)
            + [pl.BlockSpec((3, tk_f), lambda r, k: (0, k)),
               pl.BlockSpec((3, tk_f), lambda r, k: (0, hidden // tk_f + k))],
            ffn_w_down[layer].astype(BF16), xs, g2, nc, tm_p, tk_f)
    return final_norm(xs, final_norm_g, nc)[None]
```

```python
import functools
import math

import jax
import jax.numpy as jnp
from jax import lax
from jax.experimental import pallas as pl
from jax.experimental.pallas import tpu as pltpu

F32 = jnp.float32
BF16 = jnp.bfloat16

EPS = 1e-6
NEG_INF = -1e30
GRID_W = 64
ROPE_BASE = 10000.0
ROPE_AXIS_FREQS = 32
DIFF_HEADS = 4
DIFF_DH = 128
SWA_HEADS = 8
SWA_KV_HEADS = 2
SWA_GROUP = SWA_HEADS // SWA_KV_HEADS
SWA_DH = 128
WINDOW = 128
DN_QK_HEADS = 16
DN_V_HEADS = 32
DN_DK = 128
DN_DV = 128
DN_CHUNK = 64
LANES = 128
HALO = 16
VMEM_LIMIT = 56 * 1024 * 1024


def _cp(sem):
    return pltpu.CompilerParams(dimension_semantics=sem, vmem_limit_bytes=VMEM_LIMIT)


def _pick(n, candidates):
    for c in candidates:
        if n % c == 0:
            return c
    raise ValueError(f"no tile for {n} among {candidates}")


def _silu(x):
    return x * jax.nn.sigmoid(x)


def _rows(i, tm):
    return i * tm + lax.broadcasted_iota(jnp.int32, (tm, 1), 0)


def _mod_kernel(a_ref, w_ref, b_ref, o_ref):
    a = _silu(a_ref[...]).astype(BF16)
    w = w_ref[0].astype(BF16)
    o_ref[0] = jnp.dot(a, w, preferred_element_type=F32) + b_ref[0]


def mod_vectors(cc, w_mod, b_mod):
    depth, d, n = w_mod.shape
    tn = _pick(n, (1024, 512, 256, 128))
    return pl.pallas_call(
        _mod_kernel,
        grid=(depth, n // tn),
        in_specs=[pl.BlockSpec((8, d), lambda l, j: (0, 0)),
                  pl.BlockSpec((1, d, tn), lambda l, j: (l, 0, j)),
                  pl.BlockSpec((1, 1, tn), lambda l, j: (l, 0, j))],
        out_specs=pl.BlockSpec((1, 8, tn), lambda l, j: (l, 0, j)),
        out_shape=jax.ShapeDtypeStruct((depth, 8, n), F32),
        compiler_params=_cp(("arbitrary", "arbitrary")),
        name="mod_vectors",
    )(cc, w_mod, b_mod.reshape(depth, 1, n))


def _nmm_kernel(x_ref, g_ref, sh_ref, sc_ref, w_ref, *rest, tm, tn, nc, rope):
    if rope:
        cos_ref, sin_ref, cmask_ref, cscale_ref, o_ref, h_ref = rest
    else:
        o_ref, h_ref = rest
    i = pl.program_id(0)
    j = pl.program_id(1)

    @pl.when(j == 0)
    def _():
        sub = math.gcd(math.gcd(tm, 128), nc)
        gain = g_ref[...] * (1.0 + sc_ref[0:2, :])
        for r0 in range(0, tm, sub):
            x = x_ref[r0:r0 + sub, :]
            ms = jnp.mean(x * x, axis=-1, keepdims=True)
            is_ctx = i * tm + r0 < nc
            gs = jnp.where(is_ctx, gain[1:2, :], gain[0:1, :])
            sh = jnp.where(is_ctx, sh_ref[1:2, :], sh_ref[0:1, :])
            h_ref[r0:r0 + sub, :] = (x * lax.rsqrt(ms + EPS) * gs + sh).astype(BF16)

    acc = jnp.dot(h_ref[...], w_ref[...], preferred_element_type=F32)
    if not rope:
        o_ref[...] = acc.astype(o_ref.dtype)
        return
    rs = math.gcd(tm, 128)
    lane = lax.broadcasted_iota(jnp.int32, (rs, LANES), 1)
    first = (lane % (2 * ROPE_AXIS_FREQS)) < ROPE_AXIS_FREQS
    for r0 in range(0, tm, rs):
        cos = cos_ref[r0:r0 + rs, :]
        sin = sin_ref[r0:r0 + rs, :]
        for c0 in range(0, tn, LANES):
            a = acc[r0:r0 + rs, c0:c0 + LANES]
            partner = jnp.where(first, pltpu.roll(a, LANES - ROPE_AXIS_FREQS, 1), pltpu.roll(a, ROPE_AXIS_FREQS, 1))
            roped = a * cos + partner * sin
            a = jnp.where(cmask_ref[:, c0:c0 + LANES] > 0.5, roped, a) * cscale_ref[:, c0:c0 + LANES]
            o_ref[r0:r0 + rs, c0:c0 + LANES] = a.astype(o_ref.dtype)


def norm_mod_matmul(x, g, sh, sc, w, nc, rope_args=None, tn_candidates=(1024, 512, 256, 128), out_dtype=BF16):
    t, d = x.shape
    n = w.shape[1]
    tm = _pick(t, (1280, 640, 512, 320, 256, 128))
    tn = _pick(n, tn_candidates)
    rope = rope_args is not None
    in_specs = [pl.BlockSpec((tm, d), lambda i, j: (i, 0)),
                pl.BlockSpec((1, d), lambda i, j: (0, 0)),
                pl.BlockSpec((8, d), lambda i, j: (0, 0)),
                pl.BlockSpec((8, d), lambda i, j: (0, 0)),
                pl.BlockSpec((d, tn), lambda i, j: (0, j))]
    args = [x, g.reshape(1, d), sh, sc, w]
    if rope:
        cos, sin, cmask, cscale = rope_args
        in_specs += [pl.BlockSpec((tm, LANES), lambda i, j: (i, 0)),
                     pl.BlockSpec((tm, LANES), lambda i, j: (i, 0)),
                     pl.BlockSpec((1, tn), lambda i, j: (0, j)),
                     pl.BlockSpec((1, tn), lambda i, j: (0, j))]
        args += [cos, sin, cmask, cscale]
    return pl.pallas_call(
        functools.partial(_nmm_kernel, tm=tm, tn=tn, nc=nc, rope=rope),
        grid=(t // tm, n // tn),
        in_specs=in_specs,
        out_specs=pl.BlockSpec((tm, tn), lambda i, j: (i, j)),
        out_shape=jax.ShapeDtypeStruct((t, n), out_dtype),
        scratch_shapes=[pltpu.VMEM((tm, d), BF16)],
        compiler_params=_cp(("arbitrary", "arbitrary")),
        name="norm_mod_matmul",
    )(*args)


def _proj_res_kernel(*refs, prologue, n_lhs, nk, tm, nc):
    lhs = refs[:n_lhs]
    w_ref, x_ref, gate_ref, o_ref, acc_ref = refs[n_lhs:]
    i = pl.program_id(0)
    k = pl.program_id(1)
    a = prologue(i, k, *lhs)
    p = jnp.dot(a, w_ref[...], preferred_element_type=F32)

    @pl.when(k == 0)
    def _():
        acc_ref[...] = p

    @pl.when(k > 0)
    def _():
        acc_ref[...] += p

    @pl.when(k == nk - 1)
    def _():
        is_ctx = _rows(i, tm) < nc
        gate = jnp.where(is_ctx, gate_ref[1:2, :], gate_ref[0:1, :])
        o_ref[...] = x_ref[...] + gate * acc_ref[...]


def proj_residual(prologue, lhs_args, lhs_specs, w, x, gate, nc, tm, tk):
    t, d = x.shape
    kdim = w.shape[0]
    nk = kdim // tk
    in_specs = list(lhs_specs) + [pl.BlockSpec((tk, d), lambda i, k: (k, 0)),
                                  pl.BlockSpec((tm, d), lambda i, k: (i, 0)),
                                  pl.BlockSpec((8, d), lambda i, k: (0, 0))]
    return pl.pallas_call(
        functools.partial(_proj_res_kernel, prologue=prologue, n_lhs=len(lhs_args), nk=nk, tm=tm, nc=nc),
        grid=(t // tm, nk),
        in_specs=in_specs,
        out_specs=pl.BlockSpec((tm, d), lambda i, k: (i, 0)),
        out_shape=jax.ShapeDtypeStruct((t, d), F32),
        scratch_shapes=[pltpu.VMEM((tm, d), F32)],
        compiler_params=_cp(("arbitrary", "arbitrary")),
        name="proj_residual",
    )(*lhs_args, w, x, gate)


def _halo_specs(tm, tk, t, col_off):
    nh = t // HALO
    per = tm // HALO
    return [pl.BlockSpec((tm, tk), lambda i, k: (i, col_off + k)),
            pl.BlockSpec((HALO, tk), lambda i, k: (jnp.maximum(i * per - 1, 0), col_off + k)),
            pl.BlockSpec((HALO, tk), lambda i, k: (jnp.minimum((i + 1) * per, nh - 1), col_off + k))]


CONV_STRIP = 128
SUBLANES = 8


def _patch_rows(x, g0, patch):
    parts = [x[:g0], patch(x[g0:g0 + SUBLANES]), x[g0 + SUBLANES:]]
    return jnp.concatenate([p for p in parts if p.shape[0]], axis=0)


def _conv3_strips(u_ref, prev_ref, next_ref, cw_ref, i, tm, nc, t):
    s = CONV_STRIP
    sub = lax.broadcasted_iota(jnp.int32, (SUBLANES, 1), 0)
    w = cw_ref[...]
    w0, w1, w2 = w[0:1, :], w[1:2, :], w[2:3, :]
    halo_prev = jnp.where(i == 0, 0.0, prev_ref[...].astype(F32)[HALO - 1:HALO, :])
    halo_next = jnp.where(i == t // tm - 1, 0.0, next_ref[...].astype(F32)[0:1, :])
    n = tm // s
    strips = [u_ref[j * s:(j + 1) * s, :].astype(F32) for j in range(n)]
    first_blk, first_row = nc // tm, nc % tm
    last_blk, last_row = (nc - 1) // tm, (nc - 1) % tm
    for j in range(n):
        u = strips[j]
        before = halo_prev if j == 0 else strips[j - 1][s - 1:s, :]
        after = halo_next if j == n - 1 else strips[j + 1][0:1, :]
        up = _patch_rows(pltpu.roll(u, 1, 0), 0, lambda g: jnp.where(sub == 0, before, g))
        un = _patch_rows(pltpu.roll(u, s - 1, 0), s - SUBLANES, lambda g: jnp.where(sub == SUBLANES - 1, after, g))
        if 0 < nc < t and first_row // s == j:
            r = first_row % s
            up = _patch_rows(up, r // SUBLANES * SUBLANES,
                             lambda g: jnp.where((sub == r % SUBLANES) & (i == first_blk), 0.0, g))
        if 0 < nc < t and last_row // s == j:
            r = last_row % s
            un = _patch_rows(un, r // SUBLANES * SUBLANES,
                             lambda g: jnp.where((sub == r % SUBLANES) & (i == last_blk), 0.0, g))
        yield up * w0 + u * w1 + un * w2


ATTN_STRIP = 32
LOG2E = math.log2(math.e)


def _diff_attn_kernel(lam_ref, g_ref, q_ref, k_ref, v_ref, o_ref, m_ref, l_ref, acc_ref, p_ref, alpha_ref,
                      *, tq, tk, nc, nk, lam_init):
    i = pl.program_id(1)
    kk = pl.program_id(2)

    @pl.when(kk == 0)
    def _():
        m_ref[...] = jnp.full(m_ref.shape, NEG_INF, F32)
        l_ref[...] = jnp.zeros(l_ref.shape, F32)
        acc_ref[...] = jnp.zeros(acc_ref.shape, F32)

    def step(masked):
        nt = (((1,), (1,)), ((), ()))
        v = v_ref[...]
        for c in range(2):
            cols = slice(c * DIFF_DH, (c + 1) * DIFF_DH)
            s_all = lax.dot_general(q_ref[:, cols], k_ref[:, cols], nt, preferred_element_type=F32)
            for r0 in range(0, tq, ATTN_STRIP):
                rows = slice(r0, r0 + ATTN_STRIP)
                s = s_all[rows, :]
                if masked:
                    qrow = i * tq + r0 + lax.broadcasted_iota(jnp.int32, (ATTN_STRIP, 1), 0)
                    kcol = kk * tk + lax.broadcasted_iota(jnp.int32, (1, tk), 1)
                    s = jnp.where((qrow >= nc) | (kcol < nc), s, NEG_INF)
                m_prev = m_ref[c, rows, :]
                m_new = jnp.maximum(m_prev, jnp.max(s, axis=-1, keepdims=True))
                alpha = jnp.exp2(m_prev - m_new)
                p = jnp.exp2(s - m_new)
                l_ref[c, rows, :] = alpha * l_ref[c, rows, :] + jnp.sum(p, axis=-1, keepdims=True)
                m_ref[c, rows, :] = m_new
                alpha_ref[c, rows, :] = alpha
                p_ref[c, rows, :] = p.astype(BF16)
            acc_ref[c] = alpha_ref[c] * acc_ref[c] + jnp.dot(p_ref[c], v, preferred_element_type=F32)

    has_ctx_rows = i * tq < nc

    @pl.when(has_ctx_rows)
    def _():
        step(True)

    @pl.when(jnp.logical_not(has_ctx_rows))
    def _():
        step(False)

    @pl.when(kk == nk - 1)
    def _():
        lf = lam_ref[...]
        lam = (jnp.exp(jnp.sum(lf[0:1] * lf[1:2], axis=-1, keepdims=True))
               - jnp.exp(jnp.sum(lf[2:3] * lf[3:4], axis=-1, keepdims=True)) + lam_init)
        o = acc_ref[0] / l_ref[0] - lam * (acc_ref[1] / l_ref[1])
        ms = jnp.mean(o * o, axis=-1, keepdims=True)
        o_ref[...] = (o * lax.rsqrt(ms + EPS) * g_ref[...] * (1.0 - lam_init)).astype(o_ref.dtype)


def diff_attention(qkv, lam_vec, subln_g, nc, lam_init, col_q, col_k, col_v):
    t = qkv.shape[0]
    tq = _pick(t, (1280, 640, 256, 128))
    tk = _pick(t, (1280, 256, 128))
    nk = t // tk
    hw = 2 * DIFF_DH
    return pl.pallas_call(
        functools.partial(_diff_attn_kernel, tq=tq, tk=tk, nc=nc, nk=nk, lam_init=lam_init),
        grid=(DIFF_HEADS, t // tq, nk),
        in_specs=[pl.BlockSpec((4, DIFF_DH), lambda h, i, k: (0, 0)),
                  pl.BlockSpec((1, hw), lambda h, i, k: (0, 0)),
                  pl.BlockSpec((tq, hw), lambda h, i, k: (i, col_q // hw + h)),
                  pl.BlockSpec((tk, hw), lambda h, i, k: (k, col_k // hw + h)),
                  pl.BlockSpec((tk, hw), lambda h, i, k: (k, col_v // hw + h))],
        out_specs=pl.BlockSpec((tq, hw), lambda h, i, k: (i, h)),
        out_shape=jax.ShapeDtypeStruct((t, DIFF_HEADS * hw), BF16),
        scratch_shapes=[pltpu.VMEM((2, tq, 1), F32), pltpu.VMEM((2, tq, 1), F32), pltpu.VMEM((2, tq, hw), F32),
                        pltpu.VMEM((2, tq, tk), BF16), pltpu.VMEM((2, tq, 1), F32)],
        compiler_params=_cp(("arbitrary", "arbitrary", "arbitrary")),
        name="diff_attention",
    )(lam_vec, subln_g.reshape(1, hw), qkv, qkv, qkv)


def _diff_attn_t_kernel(lam_ref, g_ref, q_ref, k_ref, vt_ref, o_ref, m_ref, l_ref, acc_ref, p_ref, alpha_ref,
                        *, tq, tk, nc, nk, lam_init):
    i = pl.program_id(1)
    kk = pl.program_id(2)

    @pl.when(kk == 0)
    def _():
        m_ref[...] = jnp.full(m_ref.shape, NEG_INF, F32)
        l_ref[...] = jnp.zeros(l_ref.shape, F32)
        acc_ref[...] = jnp.zeros(acc_ref.shape, F32)

    def step(masked):
        nt = (((1,), (1,)), ((), ()))
        vt = vt_ref[...]
        for c in range(2):
            cols = slice(c * DIFF_DH, (c + 1) * DIFF_DH)
            s_all = lax.dot_general(k_ref[:, cols], q_ref[:, cols], nt, preferred_element_type=F32)
            for q0 in range(0, tq, LANES):
                qs = slice(q0, q0 + LANES)
                s = s_all[:, qs]
                if masked:
                    krow = kk * tk + lax.broadcasted_iota(jnp.int32, (tk, 1), 0)
                    qcol = i * tq + q0 + lax.broadcasted_iota(jnp.int32, (1, LANES), 1)
                    s = jnp.where((qcol >= nc) | (krow < nc), s, NEG_INF)
                m_prev = m_ref[c, :, qs]
                m_new = jnp.maximum(m_prev, jnp.max(s, axis=0, keepdims=True))
                alpha = jnp.exp2(m_prev - m_new)
                p = jnp.exp2(s - m_new)
                l_ref[c, :, qs] = alpha * l_ref[c, :, qs] + jnp.sum(p, axis=0, keepdims=True)
                m_ref[c, :, qs] = m_new
                alpha_ref[c, :, qs] = alpha
                p_ref[c, :, qs] = p.astype(BF16)
            acc_ref[c] = alpha_ref[c] * acc_ref[c] + jnp.dot(vt, p_ref[c], preferred_element_type=F32)

    has_ctx_rows = i * tq < nc

    @pl.when(has_ctx_rows)
    def _():
        step(True)

    @pl.when(jnp.logical_not(has_ctx_rows))
    def _():
        step(False)

    @pl.when(kk == nk - 1)
    def _():
        lf = lam_ref[...]
        lam = (jnp.exp(jnp.sum(lf[0:1] * lf[1:2], axis=-1, keepdims=True))
               - jnp.exp(jnp.sum(lf[2:3] * lf[3:4], axis=-1, keepdims=True)) + lam_init)
        o = acc_ref[0] / l_ref[0] - lam * (acc_ref[1] / l_ref[1])
        ms = jnp.mean(o * o, axis=0, keepdims=True)
        o_ref[...] = (o * lax.rsqrt(ms + EPS) * g_ref[...] * (1.0 - lam_init)).astype(o_ref.dtype)


def diff_attention_keymajor(qkv, lam_vec, subln_g, nc, lam_init, col_q, col_k, col_v):
    t = qkv.shape[0]
    tq = _pick(t, (1280, 640, 256, 128))
    tk = _pick(t, (1280, 256, 128))
    nk = t // tk
    hw = 2 * DIFF_DH
    vt = qkv[:, col_v:col_v + DIFF_HEADS * hw].T
    out_t = pl.pallas_call(
        functools.partial(_diff_attn_t_kernel, tq=tq, tk=tk, nc=nc, nk=nk, lam_init=lam_init),
        grid=(DIFF_HEADS, t // tq, nk),
        in_specs=[pl.BlockSpec((4, DIFF_DH), lambda h, i, k: (0, 0)),
                  pl.BlockSpec((hw, 1), lambda h, i, k: (0, 0)),
                  pl.BlockSpec((tq, hw), lambda h, i, k: (i, col_q // hw + h)),
                  pl.BlockSpec((tk, hw), lambda h, i, k: (k, col_k // hw + h)),
                  pl.BlockSpec((hw, tk), lambda h, i, k: (h, k))],
        out_specs=pl.BlockSpec((hw, tq), lambda h, i, k: (h, i)),
        out_shape=jax.ShapeDtypeStruct((DIFF_HEADS * hw, t), BF16),
        scratch_shapes=[pltpu.VMEM((2, 1, tq), F32), pltpu.VMEM((2, 1, tq), F32), pltpu.VMEM((2, hw, tq), F32),
                        pltpu.VMEM((2, tk, tq), BF16), pltpu.VMEM((2, 1, tq), F32)],
        compiler_params=_cp(("arbitrary", "arbitrary", "arbitrary")),
        name="diff_attention",
    )(lam_vec, subln_g.reshape(hw, 1), qkv, qkv, vt)
    return out_t.T


def _swa_kernel(sink_ref, q_ref, kc_ref, k0_ref, k1_ref, k2_ref, vc_ref, v0_ref, v1_ref, v2_ref, o_ref,
                *, nc, t):
    r = pl.program_id(1)
    qb = WINDOW
    q = jnp.concatenate([q_ref[:, g * SWA_DH:(g + 1) * SWA_DH] for g in range(SWA_GROUP)], axis=0)
    sk = sink_ref[0]
    sink = jnp.concatenate([jnp.broadcast_to(sk[g:g + 1, 0:1], (qb, 1)) for g in range(SWA_GROUP)], axis=0)
    nt = (((1,), (1,)), ((), ()))
    local_q = lax.broadcasted_iota(jnp.int32, (qb, 1), 0)
    qrow = jnp.concatenate([r * qb + local_q] * SWA_GROUP, axis=0)
    s_list = [lax.dot_general(q, kc_ref[...], nt, preferred_element_type=F32)]
    for w, k_ref in zip((-1, 0, 1), (k0_ref, k1_ref, k2_ref)):
        s = lax.dot_general(q, k_ref[...], nt, preferred_element_type=F32)
        krow = (r + w) * qb + lax.broadcasted_iota(jnp.int32, (1, qb), 1)
        valid = (jnp.abs(qrow - krow) <= WINDOW) & (krow >= nc) & (krow < t) & (qrow >= nc)
        s_list.append(jnp.where(valid, s, NEG_INF))
    m = sink
    for s in s_list:
        m = jnp.maximum(m, jnp.max(s, axis=-1, keepdims=True))
    denom = jnp.exp(sink - m)
    out = jnp.zeros((SWA_GROUP * qb, SWA_DH), F32)
    for s, v_ref in zip(s_list, (vc_ref, v0_ref, v1_ref, v2_ref)):
        e = jnp.exp(s - m)
        denom = denom + jnp.sum(e, axis=-1, keepdims=True)
        out = out + jnp.dot(e.astype(BF16), v_ref[...], preferred_element_type=F32)
    out = out / denom
    for g in range(SWA_GROUP):
        o_ref[:, g * SWA_DH:(g + 1) * SWA_DH] = out[g * qb:(g + 1) * qb, :].astype(o_ref.dtype)


def swa_attention(qkv, sink, nc, col_q, col_k, col_v):
    t = qkv.shape[0]
    qb = WINDOW
    nb = t // qb
    gw = SWA_GROUP * SWA_DH
    sink_b = jnp.broadcast_to(sink.astype(F32).reshape(SWA_KV_HEADS, SWA_GROUP, 1), (SWA_KV_HEADS, SWA_GROUP, LANES))
    kcb, vcb = col_k // SWA_DH, col_v // SWA_DH

    def win(cb, w):
        return pl.BlockSpec((qb, SWA_DH), lambda kv, r: (jnp.clip(r + w, 0, nb - 1), cb + kv))

    return pl.pallas_call(
        functools.partial(_swa_kernel, nc=nc, t=t),
        grid=(SWA_KV_HEADS, nb),
        in_specs=[pl.BlockSpec((1, SWA_GROUP, LANES), lambda kv, r: (kv, 0, 0)),
                  pl.BlockSpec((qb, gw), lambda kv, r: (r, col_q // gw + kv)),
                  pl.BlockSpec((nc, SWA_DH), lambda kv, r: (0, kcb + kv)),
                  win(kcb, -1), win(kcb, 0), win(kcb, 1),
                  pl.BlockSpec((nc, SWA_DH), lambda kv, r: (0, vcb + kv)),
                  win(vcb, -1), win(vcb, 0), win(vcb, 1)],
        out_specs=pl.BlockSpec((qb, gw), lambda kv, r: (r, kv)),
        out_shape=jax.ShapeDtypeStruct((t, SWA_HEADS * SWA_DH), BF16),
        compiler_params=_cp(("arbitrary", "arbitrary")),
        name="swa_attention",
    )(sink_b, qkv, qkv, qkv, qkv, qkv, qkv, qkv, qkv, qkv)


def _dn_conv_kernel(u_ref, prev_ref, next_ref, cw_ref, o_ref, *, tm, tc, nc, t, n_q, n_qk):
    i = pl.program_id(0)
    j = pl.program_id(1)
    s = CONV_STRIP

    @pl.when(j < n_qk)
    def _():
        scale = jnp.where(j < n_q, DN_DK ** -0.5, 1.0).astype(F32)
        for n, conv in enumerate(_conv3_strips(u_ref, prev_ref, next_ref, cw_ref, i, tm, nc, t)):
            y = _silu(conv)
            for c0 in range(0, tc, DN_DK):
                a = y[:, c0:c0 + DN_DK]
                ss = jnp.sum(a * a, axis=-1, keepdims=True)
                o_ref[n * s:(n + 1) * s, c0:c0 + DN_DK] = (a * lax.rsqrt(ss + EPS) * scale).astype(o_ref.dtype)

    @pl.when(j >= n_qk)
    def _():
        for n, conv in enumerate(_conv3_strips(u_ref, prev_ref, next_ref, cw_ref, i, tm, nc, t)):
            o_ref[n * s:(n + 1) * s, :] = _silu(conv).astype(o_ref.dtype)


def dn_conv(u, conv_w, nc, n_cols):
    t = u.shape[0]
    tm = _pick(t, (640, 320, 256, 128))
    tc = 1024
    qk_cols = DN_QK_HEADS * DN_DK
    return pl.pallas_call(
        functools.partial(_dn_conv_kernel, tm=tm, tc=tc, nc=nc, t=t, n_q=qk_cols // tc, n_qk=2 * qk_cols // tc),
        grid=(t // tm, n_cols // tc),
        in_specs=_halo_specs(tm, tc, t, 0) + [pl.BlockSpec((3, tc), lambda i, k: (0, k))],
        out_specs=pl.BlockSpec((tm, tc), lambda i, k: (i, k)),
        out_shape=jax.ShapeDtypeStruct((t, n_cols), BF16),
        compiler_params=_cp(("arbitrary", "arbitrary")),
        name="dn_conv",
    )(u, u, u, conv_w)


def _split3(x):
    hi = x.astype(BF16)
    r1 = x - hi.astype(F32)
    mid = r1.astype(BF16)
    lo = (r1 - mid.astype(F32)).astype(BF16)
    return hi, mid, lo


def _delta_kernel(gf_ref, gr_ref, alog_ref, bias_ref, qf_ref, kf_ref, vf_ref, qr_ref, kr_ref, vr_ref,
                       of_ref, or_ref, s_ref):
    s_idx = pl.program_id(0)
    c = DN_CHUNK
    nt = (((1,), (1,)), ((), ()))
    tn = (((0,), (0,)), ((), ()))

    @pl.when(s_idx == 0)
    def _():
        s_ref[...] = jnp.zeros(s_ref.shape, F32)

    def dot(x, y):
        return jnp.dot(x, y, preferred_element_type=F32)

    ri = lax.broadcasted_iota(jnp.int32, (c, c), 0)
    ci = lax.broadcasted_iota(jnp.int32, (c, c), 1)
    eye = ri == ci
    eye_f = eye.astype(F32)
    nh = DN_V_HEADS
    ctx = []
    for d, (g_ref, q_ref, k_ref, v_ref, o_ref) in enumerate(((gf_ref, qf_ref, kf_ref, vf_ref, of_ref),
                                                           (gr_ref, qr_ref, kr_ref, vr_ref, or_ref))):
        shift = (2 * LANES - d * 2 * nh) % LANES
        gt = pltpu.roll(g_ref[...], shift, 1) if shift else g_ref[...]
        alog = pltpu.roll(alog_ref[...], shift, 1) if shift else alog_ref[...]
        bias = pltpu.roll(bias_ref[...], shift, 1) if shift else bias_ref[...]
        z = gt + bias[0:1, :]
        g_all = -jnp.exp(alog[0:1, :]) * (jnp.maximum(z, 0.0) + jnp.log(1.0 + jnp.exp(-jnp.abs(z))))
        order = (ri - ci) * (1 if d == 0 else -1)
        incl = order >= 0
        csum = incl.astype(BF16)
        hi, mid, lo = _split3(g_all)
        ctx.append(dict(beta=jax.nn.sigmoid(gt), gam=dot(csum, hi) + dot(csum, mid) + dot(csum, lo),
                        tot=jnp.sum(g_all, axis=0, keepdims=True), incl=incl, strict=order > 0,
                        q=q_ref, k=k_ref, v=v_ref, o=o_ref))

    heads = [(d, hv) for d in range(2) for hv in range(nh)]
    pairs = [(d, hq) for d in range(2) for hq in range(DN_QK_HEADS)]
    kb = {p: ctx[p[0]]["k"][:, p[1] * DN_DK:(p[1] + 1) * DN_DK] for p in pairs}
    qb = {p: ctx[p[0]]["q"][:, p[1] * DN_DK:(p[1] + 1) * DN_DK] for p in pairs}
    kq = {p: lax.dot_general(jnp.concatenate([kb[p], qb[p]], axis=0), kb[p], nt, preferred_element_type=F32)
          for p in pairs}
    kf = {p: kb[p].astype(F32) for p in pairs}
    qf = {p: qb[p].astype(F32) for p in pairs}
    qk_of = {h: (h[0], h[1] // 2) for h in heads}
    beta = {h: ctx[h[0]]["beta"][:, h[1]:h[1] + 1] for h in heads}
    gam = {h: ctx[h[0]]["gam"][:, nh + h[1]:nh + h[1] + 1] for h in heads}
    tot = {h: ctx[h[0]]["tot"][:, nh + h[1]:nh + h[1] + 1] for h in heads}
    incl = {h: ctx[h[0]]["incl"] for h in heads}
    gam_row = {h: jnp.sum(jnp.where(eye, gam[h], 0.0), axis=0, keepdims=True) for h in heads}
    dec = {h: jnp.where(incl[h], jnp.exp(jnp.where(incl[h], gam[h] - gam_row[h], 0.0)), 0.0) for h in heads}
    a = {h: jnp.where(ctx[h[0]]["strict"], beta[h] * kq[qk_of[h]][:c] * dec[h], 0.0) for h in heads}
    pw = {h: -a[h] for h in heads}
    tinv = {h: eye_f + pw[h] for h in heads}
    for _ in range(int(math.log2(c)) - 2):
        pwb = {h: pw[h].astype(BF16) for h in heads}
        pw = {h: dot(pwb[h], pwb[h]) for h in heads}
        tinv = {h: tinv[h] + dot(tinv[h].astype(BF16), pw[h].astype(BF16)) for h in heads}
    tb = {h: tinv[h].astype(BF16) for h in heads}
    resid = {h: eye_f - tinv[h] - dot(a[h].astype(BF16), tb[h]) for h in heads}
    tinv = {h: tinv[h] + dot(tb[h], resid[h].astype(BF16)) for h in heads}
    tb = {h: tinv[h].astype(BF16) for h in heads}
    eg = {h: jnp.exp(gam[h]) for h in heads}
    vf = {h: ctx[h[0]]["v"][:, h[1] * DN_DV:(h[1] + 1) * DN_DV].astype(F32) for h in heads}
    rhs = {h: jnp.concatenate([kf[qk_of[h]] * (beta[h] * eg[h]), vf[h] * beta[h]], axis=1).astype(BF16) for h in heads}
    sol = {h: dot(tb[h], rhs[h]) for h in heads}
    aqk = {h: jnp.where(incl[h], kq[qk_of[h]][c:] * dec[h], 0.0).astype(BF16) for h in heads}
    wq = {h: jnp.concatenate([sol[h][:, :DN_DK], qf[qk_of[h]] * eg[h]], axis=0).astype(BF16) for h in heads}
    kd = {h: (kf[qk_of[h]] * jnp.exp(tot[h] - gam[h])).astype(BF16) for h in heads}
    state = {h: s_ref[h[0] * nh + h[1]] for h in heads}
    ws = {h: dot(wq[h], state[h].astype(BF16)) for h in heads}
    vnb = {h: (sol[h][:, DN_DK:] - ws[h][:c]).astype(BF16) for h in heads}
    o = {h: ws[h][c:] + dot(aqk[h], vnb[h]) for h in heads}
    upd = {h: lax.dot_general(kd[h], vnb[h], tn, preferred_element_type=F32) for h in heads}
    for h in heads:
        s_ref[h[0] * nh + h[1]] = jnp.exp(tot[h]) * state[h] + upd[h]
        ctx[h[0]]["o"][:, h[1] * DN_DV:(h[1] + 1) * DN_DV] = o[h].astype(of_ref.dtype)


def gated_delta(qkvc, gates, alog, bias, nc):
    t = qkvc.shape[0]
    c = DN_CHUNK
    nch = t // c
    ncc = nc // c
    qw = DN_QK_HEADS * DN_DK
    vw = DN_V_HEADS * DN_DV

    def rev(s):
        return jnp.where(s < ncc, ncc - 1 - s, nch - 1 - (s - ncc))

    def specs(chunk):
        return [pl.BlockSpec((c, qw), lambda s: (chunk(s), 0)),
                pl.BlockSpec((c, qw), lambda s: (chunk(s), 1)),
                pl.BlockSpec((c, vw), lambda s: (chunk(s), 2 * qw // vw))]

    fwd = lambda s: s
    out = jax.ShapeDtypeStruct((t, vw), BF16)
    return pl.pallas_call(
        _delta_kernel,
        grid=(nch,),
        in_specs=[pl.BlockSpec((c, LANES), lambda s: (s, 0)),
                  pl.BlockSpec((c, LANES), lambda s: (rev(s), 0)),
                  pl.BlockSpec((8, LANES), lambda s: (0, 0)),
                  pl.BlockSpec((8, LANES), lambda s: (0, 0))] + specs(fwd) + specs(rev),
        out_specs=[pl.BlockSpec((c, vw), lambda s: (s, 0)), pl.BlockSpec((c, vw), lambda s: (rev(s), 0))],
        out_shape=[out, out],
        scratch_shapes=[pltpu.VMEM((2 * DN_V_HEADS, DN_DK, DN_DV), F32)],
        compiler_params=_cp(("arbitrary",)),
        name="gated_delta",
    )(gates, gates, alog, bias, qkvc, qkvc, qkvc, qkvc, qkvc, qkvc)


def _attn_out_prologue(i, k, oa_ref, ob_ref, *, n_a):
    return jnp.where(k < n_a, oa_ref[...], ob_ref[...])


def _dn_out_prologue(i, k, of_ref, or_ref, z_ref, g_ref, *, tk):
    tm = z_ref.shape[0]
    rs = math.gcd(tm, 64)
    strips = []
    for r0 in range(0, tm, rs):
        parts = []
        for c0 in range(0, tk, DN_DV):
            a = of_ref[r0:r0 + rs, c0:c0 + DN_DV].astype(F32) + or_ref[r0:r0 + rs, c0:c0 + DN_DV].astype(F32)
            ms = jnp.mean(a * a, axis=-1, keepdims=True)
            z = z_ref[r0:r0 + rs, c0:c0 + DN_DV].astype(F32)
            parts.append((a * lax.rsqrt(ms + EPS) * g_ref[...] * _silu(z)).astype(BF16))
        strips.append(jnp.concatenate(parts, axis=-1))
    return jnp.concatenate(strips, axis=0)


def _ffn_prologue(i, k, ug, pg, ng, uu, pu, nu, cwg, cwu, *, tm, nc, t):
    gates = _conv3_strips(ug, pg, ng, cwg, i, tm, nc, t)
    ups = _conv3_strips(uu, pu, nu, cwu, i, tm, nc, t)
    return jnp.concatenate([(_silu(g) * u).astype(BF16) for g, u in zip(gates, ups)], axis=0)


def _final_norm_kernel(x_ref, g_ref, o_ref):
    x = x_ref[...]
    ms = jnp.mean(x * x, axis=-1, keepdims=True)
    o_ref[...] = x * lax.rsqrt(ms + EPS) * g_ref[...]


def final_norm(x, g, nc):
    t, d = x.shape
    tm = math.gcd(nc, 256)
    off = nc // tm
    return pl.pallas_call(
        _final_norm_kernel,
        grid=((t - nc) // tm,),
        in_specs=[pl.BlockSpec((tm, d), lambda i: (i + off, 0)), pl.BlockSpec((1, d), lambda i: (0, 0))],
        out_specs=pl.BlockSpec((tm, d), lambda i: (i, 0)),
        out_shape=jax.ShapeDtypeStruct((t - nc, d), F32),
        compiler_params=_cp(("arbitrary",)),
        name="final_norm",
    )(x, g.reshape(1, d))


def _rope_tables(n, nc):
    rows = n // GRID_W
    inv = ROPE_BASE ** (-jnp.arange(ROPE_AXIS_FREQS, dtype=F32) / ROPE_AXIS_FREQS)
    ar = jnp.arange(rows, dtype=F32)[:, None] * inv
    ac = jnp.arange(GRID_W, dtype=F32)[:, None] * inv

    def per_token(row_table, col_table):
        r = jnp.broadcast_to(row_table[:, None, :], (rows, GRID_W, ROPE_AXIS_FREQS)).reshape(n, ROPE_AXIS_FREQS)
        c = jnp.broadcast_to(col_table[None, :, :], (rows, GRID_W, ROPE_AXIS_FREQS)).reshape(n, ROPE_AXIS_FREQS)
        return r, c

    cr, cc = per_token(jnp.cos(ar), jnp.cos(ac))
    sr, sc = per_token(jnp.sin(ar), jnp.sin(ac))
    cos = jnp.concatenate([cr, cr, cc, cc], axis=-1)
    sin = jnp.concatenate([-sr, sr, -sc, sc], axis=-1)
    cos = jnp.concatenate([jnp.ones((nc, LANES), F32), cos], axis=0)
    sin = jnp.concatenate([jnp.zeros((nc, LANES), F32), sin], axis=0)
    return cos, sin


def kernel(x, c, ctx, c_ctx, w_mod, b_mod, norm1_g, norm2_g, attn_w_in, diff_lambda, diff_subln_g, swa_sink,
           attn_w_out, dn_w_in, dn_conv_w, dn_a_log, dn_dt_bias, dn_norm_g, dn_w_out, ffn_w_up, ffn_conv_w,
           ffn_w_down, final_norm_g):
    assert x.shape[0] == 1, "single-sequence kernel"
    n, d = x.shape[1], x.shape[2]
    nc = ctx.shape[1]
    t = nc + n
    depth = w_mod.shape[0]
    xs = jnp.concatenate([ctx[0], x[0]], axis=0)
    cc = jnp.zeros((8, d), F32).at[0].set(c[0]).at[1].set(c_ctx)
    mods = mod_vectors(cc, w_mod, b_mod)
    cos, sin = _rope_tables(n, nc)

    da, sb_, skv = DIFF_HEADS * 2 * DIFF_DH, SWA_HEADS * SWA_DH, SWA_KV_HEADS * SWA_DH
    col_qa, col_ka, col_qb = 0, da, 2 * da
    col_va = col_qb + sb_
    col_kb = col_va + da
    col_vb = col_kb + skv
    n_attn = col_vb + skv
    cols = jnp.arange(n_attn)
    cmask = ((cols < col_va) | ((cols >= col_kb) & (cols < col_vb))).astype(F32).reshape(1, n_attn)
    cscale = jnp.where(cols < col_ka, DIFF_DH ** -0.5 * LOG2E,
                       jnp.where((cols >= col_qb) & (cols < col_va), SWA_DH ** -0.5, 1.0))
    cscale = cscale.astype(F32).reshape(1, n_attn)

    tm_p = _pick(t, (640, 320, 256, 128))
    hidden = ffn_w_down.shape[1]
    tk_f = _pick(hidden, (1408, 512, 256, 128))
    qkv_cols = 2 * DN_QK_HEADS * DN_DK + DN_V_HEADS * DN_DV
    dn_out = DN_V_HEADS * DN_DV

    for layer in range(depth):
        m = mods[layer]
        sh1, sc1, g1, sh2, sc2, g2 = (m[:, j * d:(j + 1) * d] for j in range(6))
        i = layer // 2
        if layer % 2 == 0:
            lam_init = 0.8 - 0.6 * math.exp(-0.3 * layer)
            w = attn_w_in[i]
            w = jnp.concatenate([w[:, 0:2 * da], w[:, 3 * da:3 * da + sb_], w[:, 2 * da:3 * da],
                                 w[:, 3 * da + sb_:]], axis=1).astype(BF16)
            qkv = norm_mod_matmul(xs, norm1_g[layer], sh1, sc1, w, nc, rope_args=(cos, sin, cmask, cscale),
                                  tn_candidates=(512, 256, 128))
            oa = diff_attention_keymajor(qkv, diff_lambda[i], diff_subln_g[i], nc, lam_init, col_qa, col_ka, col_va)
            ob = swa_attention(qkv, swa_sink[i], nc, col_qb, col_kb, col_vb)
            tk = da
            n_a = da // tk
            xs = proj_residual(
                functools.partial(_attn_out_prologue, n_a=n_a), [oa, ob],
                [pl.BlockSpec((tm_p, tk), lambda r, k: (r, jnp.minimum(k, n_a - 1))),
                 pl.BlockSpec((tm_p, tk), lambda r, k: (r, jnp.maximum(k - n_a, 0)))],
                attn_w_out[i].astype(BF16), xs, g1, nc, tm_p, tk)
        else:
            w = dn_w_in[i]
            u = norm_mod_matmul(xs, norm1_g[layer], sh1, sc1, w[:, :qkv_cols + dn_out].astype(BF16), nc)
            gates = norm_mod_matmul(xs, norm1_g[layer], sh1, sc1, w[:, qkv_cols + dn_out:].astype(BF16), nc, out_dtype=F32)
            qkvc = dn_conv(u, dn_conv_w[i], nc, qkv_cols)
            zero = jnp.zeros((2, DN_V_HEADS), F32)
            alog = jnp.stack([zero, dn_a_log[i].astype(F32)], axis=1).reshape(1, 4 * DN_V_HEADS)
            bias = jnp.stack([zero, dn_dt_bias[i].astype(F32)], axis=1).reshape(1, 4 * DN_V_HEADS)
            o_f, o_r = gated_delta(qkvc, gates, jnp.broadcast_to(alog, (8, LANES)),
                                   jnp.broadcast_to(bias, (8, LANES)), nc)
            tk = 1024
            zoff = qkv_cols // tk
            xs = proj_residual(
                functools.partial(_dn_out_prologue, tk=tk), [o_f, o_r, u, dn_norm_g[i].reshape(1, DN_DV)],
                [pl.BlockSpec((tm_p, tk), lambda r, k: (r, k)),
                 pl.BlockSpec((tm_p, tk), lambda r, k: (r, k)),
                 pl.BlockSpec((tm_p, tk), lambda r, k: (r, zoff + k)),
                 pl.BlockSpec((1, DN_DV), lambda r, k: (0, 0))],
                dn_w_out[i].astype(BF16), xs, g1, nc, tm_p, tk)
        uf = norm_mod_matmul(xs, norm2_g[layer], sh2, sc2, ffn_w_up[layer].astype(BF16), nc)
        cw = ffn_conv_w[layer]
        xs = proj_residual(
            functools.partial(_ffn_prologue, tm=tm_p, nc=nc, t=t), [uf, uf, uf, uf, uf, uf, cw, cw],
            _halo_specs(tm_p, tk_f, t, 0) + _halo_specs(tm_p, tk_f, t, hidden // tk_f)
            + [pl.BlockSpec((3, tk_f), lambda r, k: (0, k)),
               pl.BlockSpec((3, tk_f), lambda r, k: (0, hidden // tk_f + k))],
            ffn_w_down[layer].astype(BF16), xs, g2, nc, tm_p, tk_f)
    return final_norm(xs, final_norm_g, nc)[None]
```

```python
import functools
import math

import jax
import jax.numpy as jnp
from jax import lax
from jax.experimental import pallas as pl
from jax.experimental.pallas import tpu as pltpu

F32 = jnp.float32
BF16 = jnp.bfloat16

EPS = 1e-6
NEG_INF = -1e30
GRID_W = 64
ROPE_BASE = 10000.0
ROPE_AXIS_FREQS = 32
DIFF_HEADS = 4
DIFF_DH = 128
SWA_HEADS = 8
SWA_KV_HEADS = 2
SWA_GROUP = SWA_HEADS // SWA_KV_HEADS
SWA_DH = 128
WINDOW = 128
DN_QK_HEADS = 16
DN_V_HEADS = 32
DN_DK = 128
DN_DV = 128
DN_CHUNK = 64
LANES = 128
HALO = 16
VMEM_LIMIT = 56 * 1024 * 1024


def _cp(sem):
    return pltpu.CompilerParams(dimension_semantics=sem, vmem_limit_bytes=VMEM_LIMIT)


def _pick(n, candidates):
    for c in candidates:
        if n % c == 0:
            return c
    raise ValueError(f"no tile for {n} among {candidates}")


def _silu(x):
    return x * jax.nn.sigmoid(x)


def _rows(i, tm):
    return i * tm + lax.broadcasted_iota(jnp.int32, (tm, 1), 0)


def _mod_kernel(a_ref, w_ref, b_ref, o_ref):
    a = _silu(a_ref[...]).astype(BF16)
    w = w_ref[0].astype(BF16)
    o_ref[0] = jnp.dot(a, w, preferred_element_type=F32) + b_ref[0]


def mod_vectors(cc, w_mod, b_mod):
    depth, d, n = w_mod.shape
    tn = _pick(n, (1024, 512, 256, 128))
    return pl.pallas_call(
        _mod_kernel,
        grid=(depth, n // tn),
        in_specs=[pl.BlockSpec((8, d), lambda l, j: (0, 0)),
                  pl.BlockSpec((1, d, tn), lambda l, j: (l, 0, j)),
                  pl.BlockSpec((1, 1, tn), lambda l, j: (l, 0, j))],
        out_specs=pl.BlockSpec((1, 8, tn), lambda l, j: (l, 0, j)),
        out_shape=jax.ShapeDtypeStruct((depth, 8, n), F32),
        compiler_params=_cp(("arbitrary", "arbitrary")),
        name="mod_vectors",
    )(cc, w_mod, b_mod.reshape(depth, 1, n))


def _nmm_kernel(x_ref, g_ref, sh_ref, sc_ref, w_ref, *rest, tm, tn, nc, rope):
    if rope:
        cos_ref, sin_ref, cmask_ref, cscale_ref, o_ref, h_ref = rest
    else:
        o_ref, h_ref = rest
    i = pl.program_id(0)
    j = pl.program_id(1)

    @pl.when(j == 0)
    def _():
        sub = math.gcd(math.gcd(tm, 128), nc)
        gain = g_ref[...] * (1.0 + sc_ref[0:2, :])
        for r0 in range(0, tm, sub):
            x = x_ref[r0:r0 + sub, :]
            ms = jnp.mean(x * x, axis=-1, keepdims=True)
            is_ctx = i * tm + r0 < nc
            gs = jnp.where(is_ctx, gain[1:2, :], gain[0:1, :])
            sh = jnp.where(is_ctx, sh_ref[1:2, :], sh_ref[0:1, :])
            h_ref[r0:r0 + sub, :] = (x * lax.rsqrt(ms + EPS) * gs + sh).astype(BF16)

    acc = jnp.dot(h_ref[...], w_ref[...], preferred_element_type=F32)
    if not rope:
        o_ref[...] = acc.astype(o_ref.dtype)
        return
    rs = math.gcd(tm, 128)
    lane = lax.broadcasted_iota(jnp.int32, (rs, LANES), 1)
    first = (lane % (2 * ROPE_AXIS_FREQS)) < ROPE_AXIS_FREQS
    for r0 in range(0, tm, rs):
        cos = cos_ref[r0:r0 + rs, :]
        sin = sin_ref[r0:r0 + rs, :]
        for c0 in range(0, tn, LANES):
            a = acc[r0:r0 + rs, c0:c0 + LANES]
            partner = jnp.where(first, pltpu.roll(a, LANES - ROPE_AXIS_FREQS, 1), pltpu.roll(a, ROPE_AXIS_FREQS, 1))
            roped = a * cos + partner * sin
            a = jnp.where(cmask_ref[:, c0:c0 + LANES] > 0.5, roped, a) * cscale_ref[:, c0:c0 + LANES]
            o_ref[r0:r0 + rs, c0:c0 + LANES] = a.astype(o_ref.dtype)


def norm_mod_matmul(x, g, sh, sc, w, nc, rope_args=None, tn_candidates=(1024, 512, 256, 128), out_dtype=BF16):
    t, d = x.shape
    n = w.shape[1]
    tm = _pick(t, (1280, 640, 512, 320, 256, 128))
    tn = _pick(n, tn_candidates)
    rope = rope_args is not None
    in_specs = [pl.BlockSpec((tm, d), lambda i, j: (i, 0)),
                pl.BlockSpec((1, d), lambda i, j: (0, 0)),
                pl.BlockSpec((8, d), lambda i, j: (0, 0)),
                pl.BlockSpec((8, d), lambda i, j: (0, 0)),
                pl.BlockSpec((d, tn), lambda i, j: (0, j))]
    args = [x, g.reshape(1, d), sh, sc, w]
    if rope:
        cos, sin, cmask, cscale = rope_args
        in_specs += [pl.BlockSpec((tm, LANES), lambda i, j: (i, 0)),
                     pl.BlockSpec((tm, LANES), lambda i, j: (i, 0)),
                     pl.BlockSpec((1, tn), lambda i, j: (0, j)),
                     pl.BlockSpec((1, tn), lambda i, j: (0, j))]
        args += [cos, sin, cmask, cscale]
    return pl.pallas_call(
        functools.partial(_nmm_kernel, tm=tm, tn=tn, nc=nc, rope=rope),
        grid=(t // tm, n // tn),
        in_specs=in_specs,
        out_specs=pl.BlockSpec((tm, tn), lambda i, j: (i, j)),
        out_shape=jax.ShapeDtypeStruct((t, n), out_dtype),
        scratch_shapes=[pltpu.VMEM((tm, d), BF16)],
        compiler_params=_cp(("arbitrary", "arbitrary")),
        name="norm_mod_matmul",
    )(*args)


def _proj_res_kernel(*refs, prologue, n_lhs, nk, tm, nc):
    lhs = refs[:n_lhs]
    w_ref, x_ref, gate_ref, o_ref, acc_ref = refs[n_lhs:]
    i = pl.program_id(0)
    k = pl.program_id(1)
    a = prologue(i, k, *lhs)
    p = jnp.dot(a, w_ref[...], preferred_element_type=F32)

    @pl.when(k == 0)
    def _():
        acc_ref[...] = p

    @pl.when(k > 0)
    def _():
        acc_ref[...] += p

    @pl.when(k == nk - 1)
    def _():
        is_ctx = _rows(i, tm) < nc
        gate = jnp.where(is_ctx, gate_ref[1:2, :], gate_ref[0:1, :])
        o_ref[...] = x_ref[...] + gate * acc_ref[...]


def proj_residual(prologue, lhs_args, lhs_specs, w, x, gate, nc, tm, tk):
    t, d = x.shape
    kdim = w.shape[0]
    nk = kdim // tk
    in_specs = list(lhs_specs) + [pl.BlockSpec((tk, d), lambda i, k: (k, 0)),
                                  pl.BlockSpec((tm, d), lambda i, k: (i, 0)),
                                  pl.BlockSpec((8, d), lambda i, k: (0, 0))]
    return pl.pallas_call(
        functools.partial(_proj_res_kernel, prologue=prologue, n_lhs=len(lhs_args), nk=nk, tm=tm, nc=nc),
        grid=(t // tm, nk),
        in_specs=in_specs,
        out_specs=pl.BlockSpec((tm, d), lambda i, k: (i, 0)),
        out_shape=jax.ShapeDtypeStruct((t, d), F32),
        scratch_shapes=[pltpu.VMEM((tm, d), F32)],
        compiler_params=_cp(("arbitrary", "arbitrary")),
        name="proj_residual",
    )(*lhs_args, w, x, gate)


def _halo_specs(tm, tk, t, col_off):
    nh = t // HALO
    per = tm // HALO
    return [pl.BlockSpec((tm, tk), lambda i, k: (i, col_off + k)),
            pl.BlockSpec((HALO, tk), lambda i, k: (jnp.maximum(i * per - 1, 0), col_off + k)),
            pl.BlockSpec((HALO, tk), lambda i, k: (jnp.minimum((i + 1) * per, nh - 1), col_off + k))]


CONV_STRIP = 128
SUBLANES = 8


def _patch_rows(x, g0, patch):
    parts = [x[:g0], patch(x[g0:g0 + SUBLANES]), x[g0 + SUBLANES:]]
    return jnp.concatenate([p for p in parts if p.shape[0]], axis=0)


def _conv3_strips(u_ref, prev_ref, next_ref, cw_ref, i, tm, nc, t):
    s = CONV_STRIP
    sub = lax.broadcasted_iota(jnp.int32, (SUBLANES, 1), 0)
    w = cw_ref[...]
    w0, w1, w2 = w[0:1, :], w[1:2, :], w[2:3, :]
    halo_prev = jnp.where(i == 0, 0.0, prev_ref[...].astype(F32)[HALO - 1:HALO, :])
    halo_next = jnp.where(i == t // tm - 1, 0.0, next_ref[...].astype(F32)[0:1, :])
    n = tm // s
    strips = [u_ref[j * s:(j + 1) * s, :].astype(F32) for j in range(n)]
    first_blk, first_row = nc // tm, nc % tm
    last_blk, last_row = (nc - 1) // tm, (nc - 1) % tm
    for j in range(n):
        u = strips[j]
        before = halo_prev if j == 0 else strips[j - 1][s - 1:s, :]
        after = halo_next if j == n - 1 else strips[j + 1][0:1, :]
        up = _patch_rows(pltpu.roll(u, 1, 0), 0, lambda g: jnp.where(sub == 0, before, g))
        un = _patch_rows(pltpu.roll(u, s - 1, 0), s - SUBLANES, lambda g: jnp.where(sub == SUBLANES - 1, after, g))
        if 0 < nc < t and first_row // s == j:
            r = first_row % s
            up = _patch_rows(up, r // SUBLANES * SUBLANES,
                             lambda g: jnp.where((sub == r % SUBLANES) & (i == first_blk), 0.0, g))
        if 0 < nc < t and last_row // s == j:
            r = last_row % s
            un = _patch_rows(un, r // SUBLANES * SUBLANES,
                             lambda g: jnp.where((sub == r % SUBLANES) & (i == last_blk), 0.0, g))
        yield up * w0 + u * w1 + un * w2


LOG2E = math.log2(math.e)


def _diff_attn_t_kernel(lam_ref, g_ref, q_ref, k_ref, vt_ref, o_ref, m_ref, l_ref, acc_ref, p_ref, alpha_ref,
                        *, tq, tk, nc, nk, lam_init):
    i = pl.program_id(1)
    kk = pl.program_id(2)

    @pl.when(kk == 0)
    def _():
        m_ref[...] = jnp.full(m_ref.shape, NEG_INF, F32)
        l_ref[...] = jnp.zeros(l_ref.shape, F32)
        acc_ref[...] = jnp.zeros(acc_ref.shape, F32)

    def step(masked):
        nt = (((1,), (1,)), ((), ()))
        vt = vt_ref[...]
        for c in range(2):
            cols = slice(c * DIFF_DH, (c + 1) * DIFF_DH)
            s_all = lax.dot_general(k_ref[:, cols], q_ref[:, cols], nt, preferred_element_type=F32)
            for q0 in range(0, tq, LANES):
                qs = slice(q0, q0 + LANES)
                s = s_all[:, qs]
                if masked:
                    krow = kk * tk + lax.broadcasted_iota(jnp.int32, (tk, 1), 0)
                    qcol = i * tq + q0 + lax.broadcasted_iota(jnp.int32, (1, LANES), 1)
                    s = jnp.where((qcol >= nc) | (krow < nc), s, NEG_INF)
                m_prev = m_ref[c, :, qs]
                m_new = jnp.maximum(m_prev, jnp.max(s, axis=0, keepdims=True))
                alpha = jnp.exp2(m_prev - m_new)
                p = jnp.exp2(s - m_new)
                l_ref[c, :, qs] = alpha * l_ref[c, :, qs] + jnp.sum(p, axis=0, keepdims=True)
                m_ref[c, :, qs] = m_new
                alpha_ref[c, :, qs] = alpha
                p_ref[c, :, qs] = p.astype(BF16)
            acc_ref[c] = alpha_ref[c] * acc_ref[c] + jnp.dot(vt, p_ref[c], preferred_element_type=F32)

    has_ctx_rows = i * tq < nc

    @pl.when(has_ctx_rows)
    def _():
        step(True)

    @pl.when(jnp.logical_not(has_ctx_rows))
    def _():
        step(False)

    @pl.when(kk == nk - 1)
    def _():
        lf = lam_ref[...]
        lam = (jnp.exp(jnp.sum(lf[0:1] * lf[1:2], axis=-1, keepdims=True))
               - jnp.exp(jnp.sum(lf[2:3] * lf[3:4], axis=-1, keepdims=True)) + lam_init)
        o = acc_ref[0] / l_ref[0] - lam * (acc_ref[1] / l_ref[1])
        ms = jnp.mean(o * o, axis=0, keepdims=True)
        o_ref[...] = (o * lax.rsqrt(ms + EPS) * g_ref[...] * (1.0 - lam_init)).astype(o_ref.dtype)


def diff_attention_keymajor(qkv, lam_vec, subln_g, nc, lam_init, col_q, col_k, col_v):
    t = qkv.shape[0]
    tq = _pick(t, (1280, 640, 256, 128))
    tk = _pick(t, (1280, 256, 128))
    nk = t // tk
    hw = 2 * DIFF_DH
    vt = qkv[:, col_v:col_v + DIFF_HEADS * hw].T
    out_t = pl.pallas_call(
        functools.partial(_diff_attn_t_kernel, tq=tq, tk=tk, nc=nc, nk=nk, lam_init=lam_init),
        grid=(DIFF_HEADS, t // tq, nk),
        in_specs=[pl.BlockSpec((4, DIFF_DH), lambda h, i, k: (0, 0)),
                  pl.BlockSpec((hw, 1), lambda h, i, k: (0, 0)),
                  pl.BlockSpec((tq, hw), lambda h, i, k: (i, col_q // hw + h)),
                  pl.BlockSpec((tk, hw), lambda h, i, k: (k, col_k // hw + h)),
                  pl.BlockSpec((hw, tk), lambda h, i, k: (h, k))],
        out_specs=pl.BlockSpec((hw, tq), lambda h, i, k: (h, i)),
        out_shape=jax.ShapeDtypeStruct((DIFF_HEADS * hw, t), BF16),
        scratch_shapes=[pltpu.VMEM((2, 1, tq), F32), pltpu.VMEM((2, 1, tq), F32), pltpu.VMEM((2, hw, tq), F32),
                        pltpu.VMEM((2, tk, tq), BF16), pltpu.VMEM((2, 1, tq), F32)],
        compiler_params=_cp(("arbitrary", "arbitrary", "arbitrary")),
        name="diff_attention",
    )(lam_vec, subln_g.reshape(hw, 1), qkv, qkv, vt)
    return out_t.T


def _swa_kernel(sink_ref, q_ref, kc_ref, k0_ref, k1_ref, k2_ref, vc_ref, v0_ref, v1_ref, v2_ref, o_ref,
                *, nc, t):
    r = pl.program_id(1)
    qb = WINDOW
    q = jnp.concatenate([q_ref[:, g * SWA_DH:(g + 1) * SWA_DH] for g in range(SWA_GROUP)], axis=0)
    sk = sink_ref[0]
    sink = jnp.concatenate([jnp.broadcast_to(sk[g:g + 1, 0:1], (qb, 1)) for g in range(SWA_GROUP)], axis=0)
    nt = (((1,), (1,)), ((), ()))
    local_q = lax.broadcasted_iota(jnp.int32, (qb, 1), 0)
    qrow = jnp.concatenate([r * qb + local_q] * SWA_GROUP, axis=0)
    s_list = [lax.dot_general(q, kc_ref[...], nt, preferred_element_type=F32)]
    for w, k_ref in zip((-1, 0, 1), (k0_ref, k1_ref, k2_ref)):
        s = lax.dot_general(q, k_ref[...], nt, preferred_element_type=F32)
        krow = (r + w) * qb + lax.broadcasted_iota(jnp.int32, (1, qb), 1)
        valid = (jnp.abs(qrow - krow) <= WINDOW) & (krow >= nc) & (krow < t) & (qrow >= nc)
        s_list.append(jnp.where(valid, s, NEG_INF))
    m = sink
    for s in s_list:
        m = jnp.maximum(m, jnp.max(s, axis=-1, keepdims=True))
    denom = jnp.exp(sink - m)
    out = jnp.zeros((SWA_GROUP * qb, SWA_DH), F32)
    for s, v_ref in zip(s_list, (vc_ref, v0_ref, v1_ref, v2_ref)):
        e = jnp.exp(s - m)
        denom = denom + jnp.sum(e, axis=-1, keepdims=True)
        out = out + jnp.dot(e.astype(BF16), v_ref[...], preferred_element_type=F32)
    out = out / denom
    for g in range(SWA_GROUP):
        o_ref[:, g * SWA_DH:(g + 1) * SWA_DH] = out[g * qb:(g + 1) * qb, :].astype(o_ref.dtype)


def swa_attention(qkv, sink, nc, col_q, col_k, col_v):
    t = qkv.shape[0]
    qb = WINDOW
    nb = t // qb
    gw = SWA_GROUP * SWA_DH
    sink_b = jnp.broadcast_to(sink.astype(F32).reshape(SWA_KV_HEADS, SWA_GROUP, 1), (SWA_KV_HEADS, SWA_GROUP, LANES))
    kcb, vcb = col_k // SWA_DH, col_v // SWA_DH

    def win(cb, w):
        return pl.BlockSpec((qb, SWA_DH), lambda kv, r: (jnp.clip(r + w, 0, nb - 1), cb + kv))

    return pl.pallas_call(
        functools.partial(_swa_kernel, nc=nc, t=t),
        grid=(SWA_KV_HEADS, nb),
        in_specs=[pl.BlockSpec((1, SWA_GROUP, LANES), lambda kv, r: (kv, 0, 0)),
                  pl.BlockSpec((qb, gw), lambda kv, r: (r, col_q // gw + kv)),
                  pl.BlockSpec((nc, SWA_DH), lambda kv, r: (0, kcb + kv)),
                  win(kcb, -1), win(kcb, 0), win(kcb, 1),
                  pl.BlockSpec((nc, SWA_DH), lambda kv, r: (0, vcb + kv)),
                  win(vcb, -1), win(vcb, 0), win(vcb, 1)],
        out_specs=pl.BlockSpec((qb, gw), lambda kv, r: (r, kv)),
        out_shape=jax.ShapeDtypeStruct((t, SWA_HEADS * SWA_DH), BF16),
        compiler_params=_cp(("arbitrary", "arbitrary")),
        name="swa_attention",
    )(sink_b, qkv, qkv, qkv, qkv, qkv, qkv, qkv, qkv, qkv)


def _dn_conv_kernel(u_ref, prev_ref, next_ref, cw_ref, o_ref, *, tm, tc, nc, t, n_q, n_qk):
    i = pl.program_id(0)
    j = pl.program_id(1)
    s = CONV_STRIP

    @pl.when(j < n_qk)
    def _():
        scale = jnp.where(j < n_q, DN_DK ** -0.5, 1.0).astype(F32)
        for n, conv in enumerate(_conv3_strips(u_ref, prev_ref, next_ref, cw_ref, i, tm, nc, t)):
            y = _silu(conv)
            for c0 in range(0, tc, DN_DK):
                a = y[:, c0:c0 + DN_DK]
                ss = jnp.sum(a * a, axis=-1, keepdims=True)
                o_ref[n * s:(n + 1) * s, c0:c0 + DN_DK] = (a * lax.rsqrt(ss + EPS) * scale).astype(o_ref.dtype)

    @pl.when(j >= n_qk)
    def _():
        for n, conv in enumerate(_conv3_strips(u_ref, prev_ref, next_ref, cw_ref, i, tm, nc, t)):
            o_ref[n * s:(n + 1) * s, :] = _silu(conv).astype(o_ref.dtype)


def dn_conv(u, conv_w, nc, n_cols):
    t = u.shape[0]
    tm = _pick(t, (1280, 640, 320, 256, 128))
    tc = 1024
    qk_cols = DN_QK_HEADS * DN_DK
    return pl.pallas_call(
        functools.partial(_dn_conv_kernel, tm=tm, tc=tc, nc=nc, t=t, n_q=qk_cols // tc, n_qk=2 * qk_cols // tc),
        grid=(t // tm, n_cols // tc),
        in_specs=_halo_specs(tm, tc, t, 0) + [pl.BlockSpec((3, tc), lambda i, k: (0, k))],
        out_specs=pl.BlockSpec((tm, tc), lambda i, k: (i, k)),
        out_shape=jax.ShapeDtypeStruct((t, n_cols), BF16),
        compiler_params=_cp(("arbitrary", "arbitrary")),
        name="dn_conv",
    )(u, u, u, conv_w)


def _split3(x):
    hi = x.astype(BF16)
    r1 = x - hi.astype(F32)
    mid = r1.astype(BF16)
    lo = (r1 - mid.astype(F32)).astype(BF16)
    return hi, mid, lo


def _delta_kernel(gf_ref, gr_ref, alog_ref, bias_ref, qf_ref, kf_ref, vf_ref, qr_ref, kr_ref, vr_ref,
                       of_ref, or_ref, s_ref):
    s_idx = pl.program_id(0)
    c = DN_CHUNK
    nt = (((1,), (1,)), ((), ()))
    tn = (((0,), (0,)), ((), ()))

    @pl.when(s_idx == 0)
    def _():
        s_ref[...] = jnp.zeros(s_ref.shape, F32)

    def dot(x, y):
        return jnp.dot(x, y, preferred_element_type=F32)

    ri = lax.broadcasted_iota(jnp.int32, (c, c), 0)
    ci = lax.broadcasted_iota(jnp.int32, (c, c), 1)
    eye = ri == ci
    eye_f = eye.astype(F32)
    nh = DN_V_HEADS
    ctx = []
    for d, (g_ref, q_ref, k_ref, v_ref, o_ref) in enumerate(((gf_ref, qf_ref, kf_ref, vf_ref, of_ref),
                                                           (gr_ref, qr_ref, kr_ref, vr_ref, or_ref))):
        shift = (2 * LANES - d * 2 * nh) % LANES
        gt = pltpu.roll(g_ref[...], shift, 1) if shift else g_ref[...]
        alog = pltpu.roll(alog_ref[...], shift, 1) if shift else alog_ref[...]
        bias = pltpu.roll(bias_ref[...], shift, 1) if shift else bias_ref[...]
        z = gt + bias[0:1, :]
        g_all = -jnp.exp(alog[0:1, :]) * (jnp.maximum(z, 0.0) + jnp.log(1.0 + jnp.exp(-jnp.abs(z))))
        order = (ri - ci) * (1 if d == 0 else -1)
        incl = order >= 0
        csum = incl.astype(BF16)
        hi, mid, lo = _split3(g_all)
        ctx.append(dict(beta=jax.nn.sigmoid(gt), gam=dot(csum, hi) + dot(csum, mid) + dot(csum, lo),
                        tot=jnp.sum(g_all, axis=0, keepdims=True), incl=incl, strict=order > 0,
                        q=q_ref, k=k_ref, v=v_ref, o=o_ref))

    heads = [(d, hv) for d in range(2) for hv in range(nh)]
    pairs = [(d, hq) for d in range(2) for hq in range(DN_QK_HEADS)]
    kb = {p: ctx[p[0]]["k"][:, p[1] * DN_DK:(p[1] + 1) * DN_DK] for p in pairs}
    qb = {p: ctx[p[0]]["q"][:, p[1] * DN_DK:(p[1] + 1) * DN_DK] for p in pairs}
    kq = {p: lax.dot_general(jnp.concatenate([kb[p], qb[p]], axis=0), kb[p], nt, preferred_element_type=F32)
          for p in pairs}
    kf = {p: kb[p].astype(F32) for p in pairs}
    qf = {p: qb[p].astype(F32) for p in pairs}
    qk_of = {h: (h[0], h[1] // 2) for h in heads}
    beta = {h: ctx[h[0]]["beta"][:, h[1]:h[1] + 1] for h in heads}
    gam = {h: ctx[h[0]]["gam"][:, nh + h[1]:nh + h[1] + 1] for h in heads}
    tot = {h: ctx[h[0]]["tot"][:, nh + h[1]:nh + h[1] + 1] for h in heads}
    incl = {h: ctx[h[0]]["incl"] for h in heads}
    gam_row = {h: jnp.sum(jnp.where(eye, gam[h], 0.0), axis=0, keepdims=True) for h in heads}
    dec = {h: jnp.where(incl[h], jnp.exp(jnp.where(incl[h], gam[h] - gam_row[h], 0.0)), 0.0) for h in heads}
    a = {h: jnp.where(ctx[h[0]]["strict"], beta[h] * kq[qk_of[h]][:c] * dec[h], 0.0) for h in heads}
    pw = {h: -a[h] for h in heads}
    tinv = {h: eye_f + pw[h] for h in heads}
    for _ in range(int(math.log2(c)) - 2):
        pwb = {h: pw[h].astype(BF16) for h in heads}
        pw = {h: dot(pwb[h], pwb[h]) for h in heads}
        tinv = {h: tinv[h] + dot(tinv[h].astype(BF16), pw[h].astype(BF16)) for h in heads}
    tb = {h: tinv[h].astype(BF16) for h in heads}
    resid = {h: eye_f - tinv[h] - dot(a[h].astype(BF16), tb[h]) for h in heads}
    tinv = {h: tinv[h] + dot(tb[h], resid[h].astype(BF16)) for h in heads}
    tb = {h: tinv[h].astype(BF16) for h in heads}
    eg = {h: jnp.exp(gam[h]) for h in heads}
    vf = {h: ctx[h[0]]["v"][:, h[1] * DN_DV:(h[1] + 1) * DN_DV].astype(F32) for h in heads}
    rhs = {h: jnp.concatenate([kf[qk_of[h]] * (beta[h] * eg[h]), vf[h] * beta[h]], axis=1).astype(BF16) for h in heads}
    sol = {h: dot(tb[h], rhs[h]) for h in heads}
    aqk = {h: jnp.where(incl[h], kq[qk_of[h]][c:] * dec[h], 0.0).astype(BF16) for h in heads}
    wq = {h: jnp.concatenate([sol[h][:, :DN_DK], qf[qk_of[h]] * eg[h]], axis=0).astype(BF16) for h in heads}
    kd = {h: (kf[qk_of[h]] * jnp.exp(tot[h] - gam[h])).astype(BF16) for h in heads}
    state = {h: s_ref[h[0] * nh + h[1]] for h in heads}
    ws = {h: dot(wq[h], state[h].astype(BF16)) for h in heads}
    vnb = {h: (sol[h][:, DN_DK:] - ws[h][:c]).astype(BF16) for h in heads}
    o = {h: ws[h][c:] + dot(aqk[h], vnb[h]) for h in heads}
    upd = {h: lax.dot_general(kd[h], vnb[h], tn, preferred_element_type=F32) for h in heads}
    for h in heads:
        s_ref[h[0] * nh + h[1]] = jnp.exp(tot[h]) * state[h] + upd[h]
        ctx[h[0]]["o"][:, h[1] * DN_DV:(h[1] + 1) * DN_DV] = o[h].astype(of_ref.dtype)


def gated_delta(qkvc, gates, alog, bias, nc):
    t = qkvc.shape[0]
    c = DN_CHUNK
    nch = t // c
    ncc = nc // c
    qw = DN_QK_HEADS * DN_DK
    vw = DN_V_HEADS * DN_DV

    def rev(s):
        return jnp.where(s < ncc, ncc - 1 - s, nch - 1 - (s - ncc))

    def specs(chunk):
        return [pl.BlockSpec((c, qw), lambda s: (chunk(s), 0)),
                pl.BlockSpec((c, qw), lambda s: (chunk(s), 1)),
                pl.BlockSpec((c, vw), lambda s: (chunk(s), 2 * qw // vw))]

    fwd = lambda s: s
    out = jax.ShapeDtypeStruct((t, vw), BF16)
    return pl.pallas_call(
        _delta_kernel,
        grid=(nch,),
        in_specs=[pl.BlockSpec((c, LANES), lambda s: (s, 0)),
                  pl.BlockSpec((c, LANES), lambda s: (rev(s), 0)),
                  pl.BlockSpec((8, LANES), lambda s: (0, 0)),
                  pl.BlockSpec((8, LANES), lambda s: (0, 0))] + specs(fwd) + specs(rev),
        out_specs=[pl.BlockSpec((c, vw), lambda s: (s, 0)), pl.BlockSpec((c, vw), lambda s: (rev(s), 0))],
        out_shape=[out, out],
        scratch_shapes=[pltpu.VMEM((2 * DN_V_HEADS, DN_DK, DN_DV), F32)],
        compiler_params=_cp(("arbitrary",)),
        name="gated_delta",
    )(gates, gates, alog, bias, qkvc, qkvc, qkvc, qkvc, qkvc, qkvc)


def _attn_out_prologue(i, k, oa_ref, ob_ref, *, n_a):
    return jnp.where(k < n_a, oa_ref[...], ob_ref[...])


def _dn_out_prologue(i, k, of_ref, or_ref, z_ref, g_ref, *, tk):
    tm = z_ref.shape[0]
    rs = math.gcd(tm, 64)
    strips = []
    for r0 in range(0, tm, rs):
        parts = []
        for c0 in range(0, tk, DN_DV):
            a = of_ref[r0:r0 + rs, c0:c0 + DN_DV].astype(F32) + or_ref[r0:r0 + rs, c0:c0 + DN_DV].astype(F32)
            ms = jnp.mean(a * a, axis=-1, keepdims=True)
            z = z_ref[r0:r0 + rs, c0:c0 + DN_DV].astype(F32)
            parts.append((a * lax.rsqrt(ms + EPS) * g_ref[...] * _silu(z)).astype(BF16))
        strips.append(jnp.concatenate(parts, axis=-1))
    return jnp.concatenate(strips, axis=0)


def _ffn_prologue(i, k, ug, pg, ng, uu, pu, nu, cwg, cwu, *, tm, nc, t):
    gates = _conv3_strips(ug, pg, ng, cwg, i, tm, nc, t)
    ups = _conv3_strips(uu, pu, nu, cwu, i, tm, nc, t)
    return jnp.concatenate([(_silu(g) * u).astype(BF16) for g, u in zip(gates, ups)], axis=0)


def _final_norm_kernel(x_ref, g_ref, o_ref):
    x = x_ref[...]
    ms = jnp.mean(x * x, axis=-1, keepdims=True)
    o_ref[...] = x * lax.rsqrt(ms + EPS) * g_ref[...]


def final_norm(x, g, nc):
    t, d = x.shape
    tm = math.gcd(nc, 256)
    off = nc // tm
    return pl.pallas_call(
        _final_norm_kernel,
        grid=((t - nc) // tm,),
        in_specs=[pl.BlockSpec((tm, d), lambda i: (i + off, 0)), pl.BlockSpec((1, d), lambda i: (0, 0))],
        out_specs=pl.BlockSpec((tm, d), lambda i: (i, 0)),
        out_shape=jax.ShapeDtypeStruct((t - nc, d), F32),
        compiler_params=_cp(("arbitrary",)),
        name="final_norm",
    )(x, g.reshape(1, d))


def _rope_tables(n, nc):
    rows = n // GRID_W
    inv = ROPE_BASE ** (-jnp.arange(ROPE_AXIS_FREQS, dtype=F32) / ROPE_AXIS_FREQS)
    ar = jnp.arange(rows, dtype=F32)[:, None] * inv
    ac = jnp.arange(GRID_W, dtype=F32)[:, None] * inv

    def per_token(row_table, col_table):
        r = jnp.broadcast_to(row_table[:, None, :], (rows, GRID_W, ROPE_AXIS_FREQS)).reshape(n, ROPE_AXIS_FREQS)
        c = jnp.broadcast_to(col_table[None, :, :], (rows, GRID_W, ROPE_AXIS_FREQS)).reshape(n, ROPE_AXIS_FREQS)
        return r, c

    cr, cc = per_token(jnp.cos(ar), jnp.cos(ac))
    sr, sc = per_token(jnp.sin(ar), jnp.sin(ac))
    cos = jnp.concatenate([cr, cr, cc, cc], axis=-1)
    sin = jnp.concatenate([-sr, sr, -sc, sc], axis=-1)
    cos = jnp.concatenate([jnp.ones((nc, LANES), F32), cos], axis=0)
    sin = jnp.concatenate([jnp.zeros((nc, LANES), F32), sin], axis=0)
    return cos, sin


def kernel(x, c, ctx, c_ctx, w_mod, b_mod, norm1_g, norm2_g, attn_w_in, diff_lambda, diff_subln_g, swa_sink,
           attn_w_out, dn_w_in, dn_conv_w, dn_a_log, dn_dt_bias, dn_norm_g, dn_w_out, ffn_w_up, ffn_conv_w,
           ffn_w_down, final_norm_g):
    assert x.shape[0] == 1, "single-sequence kernel"
    n, d = x.shape[1], x.shape[2]
    nc = ctx.shape[1]
    t = nc + n
    depth = w_mod.shape[0]
    xs = jnp.concatenate([ctx[0], x[0]], axis=0)
    cc = jnp.zeros((8, d), F32).at[0].set(c[0]).at[1].set(c_ctx)
    mods = mod_vectors(cc, w_mod, b_mod)
    cos, sin = _rope_tables(n, nc)

    da, sb_, skv = DIFF_HEADS * 2 * DIFF_DH, SWA_HEADS * SWA_DH, SWA_KV_HEADS * SWA_DH
    col_qa, col_ka, col_qb = 0, da, 2 * da
    col_va = col_qb + sb_
    col_kb = col_va + da
    col_vb = col_kb + skv
    n_attn = col_vb + skv
    cols = jnp.arange(n_attn)
    cmask = ((cols < col_va) | ((cols >= col_kb) & (cols < col_vb))).astype(F32).reshape(1, n_attn)
    cscale = jnp.where(cols < col_ka, DIFF_DH ** -0.5 * LOG2E,
                       jnp.where((cols >= col_qb) & (cols < col_va), SWA_DH ** -0.5, 1.0))
    cscale = cscale.astype(F32).reshape(1, n_attn)

    tm_p = _pick(t, (640, 320, 256, 128))
    hidden = ffn_w_down.shape[1]
    tk_f = _pick(hidden, (1408, 512, 256, 128))
    qkv_cols = 2 * DN_QK_HEADS * DN_DK + DN_V_HEADS * DN_DV
    dn_out = DN_V_HEADS * DN_DV

    for layer in range(depth):
        m = mods[layer]
        sh1, sc1, g1, sh2, sc2, g2 = (m[:, j * d:(j + 1) * d] for j in range(6))
        i = layer // 2
        if layer % 2 == 0:
            lam_init = 0.8 - 0.6 * math.exp(-0.3 * layer)
            w = attn_w_in[i]
            w = jnp.concatenate([w[:, 0:2 * da], w[:, 3 * da:3 * da + sb_], w[:, 2 * da:3 * da],
                                 w[:, 3 * da + sb_:]], axis=1).astype(BF16)
            qkv = norm_mod_matmul(xs, norm1_g[layer], sh1, sc1, w, nc, rope_args=(cos, sin, cmask, cscale),
                                  tn_candidates=(512, 256, 128))
            oa = diff_attention_keymajor(qkv, diff_lambda[i], diff_subln_g[i], nc, lam_init, col_qa, col_ka, col_va)
            ob = swa_attention(qkv, swa_sink[i], nc, col_qb, col_kb, col_vb)
            tk = da
            n_a = da // tk
            xs = proj_residual(
                functools.partial(_attn_out_prologue, n_a=n_a), [oa, ob],
                [pl.BlockSpec((tm_p, tk), lambda r, k: (r, jnp.minimum(k, n_a - 1))),
                 pl.BlockSpec((tm_p, tk), lambda r, k: (r, jnp.maximum(k - n_a, 0)))],
                attn_w_out[i].astype(BF16), xs, g1, nc, tm_p, tk)
        else:
            w = dn_w_in[i]
            u = norm_mod_matmul(xs, norm1_g[layer], sh1, sc1, w[:, :qkv_cols + dn_out].astype(BF16), nc)
            gates = norm_mod_matmul(xs, norm1_g[layer], sh1, sc1, w[:, qkv_cols + dn_out:].astype(BF16), nc, out_dtype=F32)
            qkvc = dn_conv(u, dn_conv_w[i], nc, qkv_cols)
            zero = jnp.zeros((2, DN_V_HEADS), F32)
            alog = jnp.stack([zero, dn_a_log[i].astype(F32)], axis=1).reshape(1, 4 * DN_V_HEADS)
            bias = jnp.stack([zero, dn_dt_bias[i].astype(F32)], axis=1).reshape(1, 4 * DN_V_HEADS)
            o_f, o_r = gated_delta(qkvc, gates, jnp.broadcast_to(alog, (8, LANES)),
                                   jnp.broadcast_to(bias, (8, LANES)), nc)
            tk = 1024
            zoff = qkv_cols // tk
            xs = proj_residual(
                functools.partial(_dn_out_prologue, tk=tk), [o_f, o_r, u, dn_norm_g[i].reshape(1, DN_DV)],
                [pl.BlockSpec((tm_p, tk), lambda r, k: (r, k)),
                 pl.BlockSpec((tm_p, tk), lambda r, k: (r, k)),
                 pl.BlockSpec((tm_p, tk), lambda r, k: (r, zoff + k)),
                 pl.BlockSpec((1, DN_DV), lambda r, k: (0, 0))],
                dn_w_out[i].astype(BF16), xs, g1, nc, tm_p, tk)
        uf = norm_mod_matmul(xs, norm2_g[layer], sh2, sc2, ffn_w_up[layer].astype(BF16), nc)
        cw = ffn_conv_w[layer]
        xs = proj_residual(
            functools.partial(_ffn_prologue, tm=tm_p, nc=nc, t=t), [uf, uf, uf, uf, uf, uf, cw, cw],
            _halo_specs(tm_p, tk_f, t, 0) + _halo_specs(tm_p, tk_f, t, hidden // tk_f)
            + [pl.BlockSpec((3, tk_f), lambda r, k: (0, k)),
               pl.BlockSpec((3, tk_f), lambda r, k: (0, hidden // tk_f + k))],
            ffn_w_down[layer].astype(BF16), xs, g2, nc, tm_p, tk_f)
    return final_norm(xs, final_norm_g, nc)[None]
```

```python
import functools
import math

import jax
import jax.numpy as jnp
from jax import lax
from jax.experimental import pallas as pl
from jax.experimental.pallas import tpu as pltpu

F32 = jnp.float32
BF16 = jnp.bfloat16

EPS = 1e-6
NEG_INF = -1e30
GRID_W = 64
ROPE_BASE = 10000.0
ROPE_AXIS_FREQS = 32
DIFF_HEADS = 4
DIFF_DH = 128
SWA_HEADS = 8
SWA_KV_HEADS = 2
SWA_GROUP = SWA_HEADS // SWA_KV_HEADS
SWA_DH = 128
WINDOW = 128
DN_QK_HEADS = 16
DN_V_HEADS = 32
DN_DK = 128
DN_DV = 128
DN_CHUNK = 64
LANES = 128
HALO = 16
VMEM_LIMIT = 56 * 1024 * 1024


def _cp(sem):
    return pltpu.CompilerParams(dimension_semantics=sem, vmem_limit_bytes=VMEM_LIMIT)


def _pick(n, candidates):
    for c in candidates:
        if n % c == 0:
            return c
    raise ValueError(f"no tile for {n} among {candidates}")


def _silu(x):
    return x * jax.nn.sigmoid(x)


def _rows(i, tm):
    return i * tm + lax.broadcasted_iota(jnp.int32, (tm, 1), 0)


def _mod_kernel(a_ref, w_ref, b_ref, o_ref):
    a = _silu(a_ref[...]).astype(BF16)
    w = w_ref[0].astype(BF16)
    o_ref[0] = jnp.dot(a, w, preferred_element_type=F32) + b_ref[0]


def mod_vectors(cc, w_mod, b_mod):
    depth, d, n = w_mod.shape
    tn = _pick(n, (1024, 512, 256, 128))
    return pl.pallas_call(
        _mod_kernel,
        grid=(depth, n // tn),
        in_specs=[pl.BlockSpec((8, d), lambda l, j: (0, 0)),
                  pl.BlockSpec((1, d, tn), lambda l, j: (l, 0, j)),
                  pl.BlockSpec((1, 1, tn), lambda l, j: (l, 0, j))],
        out_specs=pl.BlockSpec((1, 8, tn), lambda l, j: (l, 0, j)),
        out_shape=jax.ShapeDtypeStruct((depth, 8, n), F32),
        compiler_params=_cp(("arbitrary", "arbitrary")),
        name="mod_vectors",
    )(cc, w_mod, b_mod.reshape(depth, 1, n))


def _nmm_kernel(x_ref, g_ref, sh_ref, sc_ref, w_ref, *rest, tm, tn, nc, rope):
    if rope:
        cos_ref, sin_ref, cmask_ref, cscale_ref, o_ref, h_ref = rest
    else:
        o_ref, h_ref = rest
    i = pl.program_id(0)
    j = pl.program_id(1)

    @pl.when(j == 0)
    def _():
        sub = math.gcd(math.gcd(tm, 128), nc)
        gain = g_ref[...] * (1.0 + sc_ref[0:2, :])
        for r0 in range(0, tm, sub):
            x = x_ref[r0:r0 + sub, :]
            ms = jnp.mean(x * x, axis=-1, keepdims=True)
            is_ctx = i * tm + r0 < nc
            gs = jnp.where(is_ctx, gain[1:2, :], gain[0:1, :])
            sh = jnp.where(is_ctx, sh_ref[1:2, :], sh_ref[0:1, :])
            h_ref[r0:r0 + sub, :] = (x * lax.rsqrt(ms + EPS) * gs + sh).astype(BF16)

    acc = jnp.dot(h_ref[...], w_ref[...], preferred_element_type=F32)
    if not rope:
        o_ref[...] = acc.astype(o_ref.dtype)
        return
    rs = math.gcd(tm, 128)
    lane = lax.broadcasted_iota(jnp.int32, (rs, LANES), 1)
    first = (lane % (2 * ROPE_AXIS_FREQS)) < ROPE_AXIS_FREQS
    for r0 in range(0, tm, rs):
        cos = cos_ref[r0:r0 + rs, :]
        sin = sin_ref[r0:r0 + rs, :]
        for c0 in range(0, tn, LANES):
            a = acc[r0:r0 + rs, c0:c0 + LANES]
            partner = jnp.where(first, pltpu.roll(a, LANES - ROPE_AXIS_FREQS, 1), pltpu.roll(a, ROPE_AXIS_FREQS, 1))
            roped = a * cos + partner * sin
            a = jnp.where(cmask_ref[:, c0:c0 + LANES] > 0.5, roped, a) * cscale_ref[:, c0:c0 + LANES]
            o_ref[r0:r0 + rs, c0:c0 + LANES] = a.astype(o_ref.dtype)


def norm_mod_matmul(x, g, sh, sc, w, nc, rope_args=None, tn_candidates=(1024, 512, 256, 128), out_dtype=BF16):
    t, d = x.shape
    n = w.shape[1]
    tm = _pick(t, (1280, 640, 512, 320, 256, 128))
    tn = _pick(n, tn_candidates)
    rope = rope_args is not None
    in_specs = [pl.BlockSpec((tm, d), lambda i, j: (i, 0)),
                pl.BlockSpec((1, d), lambda i, j: (0, 0)),
                pl.BlockSpec((8, d), lambda i, j: (0, 0)),
                pl.BlockSpec((8, d), lambda i, j: (0, 0)),
                pl.BlockSpec((d, tn), lambda i, j: (0, j))]
    args = [x, g.reshape(1, d), sh, sc, w]
    if rope:
        cos, sin, cmask, cscale = rope_args
        in_specs += [pl.BlockSpec((tm, LANES), lambda i, j: (i, 0)),
                     pl.BlockSpec((tm, LANES), lambda i, j: (i, 0)),
                     pl.BlockSpec((1, tn), lambda i, j: (0, j)),
                     pl.BlockSpec((1, tn), lambda i, j: (0, j))]
        args += [cos, sin, cmask, cscale]
    return pl.pallas_call(
        functools.partial(_nmm_kernel, tm=tm, tn=tn, nc=nc, rope=rope),
        grid=(t // tm, n // tn),
        in_specs=in_specs,
        out_specs=pl.BlockSpec((tm, tn), lambda i, j: (i, j)),
        out_shape=jax.ShapeDtypeStruct((t, n), out_dtype),
        scratch_shapes=[pltpu.VMEM((tm, d), BF16)],
        compiler_params=_cp(("arbitrary", "arbitrary")),
        name="norm_mod_matmul",
    )(*args)


def _proj_res_kernel(*refs, prologue, n_lhs, nk, tm, nc):
    lhs = refs[:n_lhs]
    w_ref, x_ref, gate_ref, o_ref, acc_ref = refs[n_lhs:]
    i = pl.program_id(0)
    k = pl.program_id(1)
    a = prologue(i, k, *lhs)
    p = jnp.dot(a, w_ref[...], preferred_element_type=F32)

    @pl.when(k == 0)
    def _():
        acc_ref[...] = p

    @pl.when(k > 0)
    def _():
        acc_ref[...] += p

    @pl.when(k == nk - 1)
    def _():
        is_ctx = _rows(i, tm) < nc
        gate = jnp.where(is_ctx, gate_ref[1:2, :], gate_ref[0:1, :])
        o_ref[...] = x_ref[...] + gate * acc_ref[...]


def proj_residual(prologue, lhs_args, lhs_specs, w, x, gate, nc, tm, tk):
    t, d = x.shape
    kdim = w.shape[0]
    nk = kdim // tk
    in_specs = list(lhs_specs) + [pl.BlockSpec((tk, d), lambda i, k: (k, 0)),
                                  pl.BlockSpec((tm, d), lambda i, k: (i, 0)),
                                  pl.BlockSpec((8, d), lambda i, k: (0, 0))]
    return pl.pallas_call(
        functools.partial(_proj_res_kernel, prologue=prologue, n_lhs=len(lhs_args), nk=nk, tm=tm, nc=nc),
        grid=(t // tm, nk),
        in_specs=in_specs,
        out_specs=pl.BlockSpec((tm, d), lambda i, k: (i, 0)),
        out_shape=jax.ShapeDtypeStruct((t, d), F32),
        scratch_shapes=[pltpu.VMEM((tm, d), F32)],
        compiler_params=_cp(("arbitrary", "arbitrary")),
        name="proj_residual",
    )(*lhs_args, w, x, gate)


def _halo_specs(tm, tk, t, col_off):
    nh = t // HALO
    per = tm // HALO
    return [pl.BlockSpec((tm, tk), lambda i, k: (i, col_off + k)),
            pl.BlockSpec((HALO, tk), lambda i, k: (jnp.maximum(i * per - 1, 0), col_off + k)),
            pl.BlockSpec((HALO, tk), lambda i, k: (jnp.minimum((i + 1) * per, nh - 1), col_off + k))]


CONV_STRIP = 128
SUBLANES = 8


def _patch_rows(x, g0, patch):
    parts = [x[:g0], patch(x[g0:g0 + SUBLANES]), x[g0 + SUBLANES:]]
    return jnp.concatenate([p for p in parts if p.shape[0]], axis=0)


def _conv3_strips(u_ref, prev_ref, next_ref, cw_ref, i, tm, nc, t):
    s = CONV_STRIP
    sub = lax.broadcasted_iota(jnp.int32, (SUBLANES, 1), 0)
    w = cw_ref[...]
    w0, w1, w2 = w[0:1, :], w[1:2, :], w[2:3, :]
    halo_prev = jnp.where(i == 0, 0.0, prev_ref[...].astype(F32)[HALO - 1:HALO, :])
    halo_next = jnp.where(i == t // tm - 1, 0.0, next_ref[...].astype(F32)[0:1, :])
    n = tm // s
    strips = [u_ref[j * s:(j + 1) * s, :].astype(F32) for j in range(n)]
    first_blk, first_row = nc // tm, nc % tm
    last_blk, last_row = (nc - 1) // tm, (nc - 1) % tm
    for j in range(n):
        u = strips[j]
        before = halo_prev if j == 0 else strips[j - 1][s - 1:s, :]
        after = halo_next if j == n - 1 else strips[j + 1][0:1, :]
        up = _patch_rows(pltpu.roll(u, 1, 0), 0, lambda g: jnp.where(sub == 0, before, g))
        un = _patch_rows(pltpu.roll(u, s - 1, 0), s - SUBLANES, lambda g: jnp.where(sub == SUBLANES - 1, after, g))
        if 0 < nc < t and first_row // s == j:
            r = first_row % s
            up = _patch_rows(up, r // SUBLANES * SUBLANES,
                             lambda g: jnp.where((sub == r % SUBLANES) & (i == first_blk), 0.0, g))
        if 0 < nc < t and last_row // s == j:
            r = last_row % s
            un = _patch_rows(un, r // SUBLANES * SUBLANES,
                             lambda g: jnp.where((sub == r % SUBLANES) & (i == last_blk), 0.0, g))
        yield up * w0 + u * w1 + un * w2


LOG2E = math.log2(math.e)
PV_TILE = 256


def _diff_attn_t_kernel(lam_ref, g_ref, q_ref, k_ref, vt_ref, o_ref, m_ref, l_ref, acc_ref, p_ref, alpha_ref,
                        *, tq, tk, nc, nk, lam_init):
    i = pl.program_id(1)
    kk = pl.program_id(2)

    @pl.when(kk == 0)
    def _():
        m_ref[...] = jnp.full(m_ref.shape, NEG_INF, F32)
        l_ref[...] = jnp.zeros(l_ref.shape, F32)
        acc_ref[...] = jnp.zeros(acc_ref.shape, F32)

    def step(masked):
        nt = (((1,), (1,)), ((), ()))
        vt = vt_ref[...]
        for c in range(2):
            cols = slice(c * DIFF_DH, (c + 1) * DIFF_DH)
            s_all = lax.dot_general(k_ref[:, cols], q_ref[:, cols], nt, preferred_element_type=F32)
            for q0 in range(0, tq, LANES):
                qs = slice(q0, q0 + LANES)
                s = s_all[:, qs]
                if masked:
                    krow = kk * tk + lax.broadcasted_iota(jnp.int32, (tk, 1), 0)
                    qcol = i * tq + q0 + lax.broadcasted_iota(jnp.int32, (1, LANES), 1)
                    s = jnp.where((qcol >= nc) | (krow < nc), s, NEG_INF)
                m_prev = m_ref[c, :, qs]
                m_new = jnp.maximum(m_prev, jnp.max(s, axis=0, keepdims=True))
                alpha = jnp.exp2(m_prev - m_new)
                p = jnp.exp2(s - m_new)
                l_ref[c, :, qs] = alpha * l_ref[c, :, qs] + jnp.sum(p, axis=0, keepdims=True)
                m_ref[c, :, qs] = m_new
                alpha_ref[c, :, qs] = alpha
                p_ref[c, :, qs] = p.astype(BF16)
                if (q0 + LANES) % PV_TILE == 0:
                    qt = slice(q0 + LANES - PV_TILE, q0 + LANES)
                    acc_ref[c, :, qt] = (alpha_ref[c, :, qt] * acc_ref[c, :, qt]
                                         + jnp.dot(vt, p_ref[c, :, qt], preferred_element_type=F32))

    has_ctx_rows = i * tq < nc

    @pl.when(has_ctx_rows)
    def _():
        step(True)

    @pl.when(jnp.logical_not(has_ctx_rows))
    def _():
        step(False)

    @pl.when(kk == nk - 1)
    def _():
        lf = lam_ref[...]
        lam = (jnp.exp(jnp.sum(lf[0:1] * lf[1:2], axis=-1, keepdims=True))
               - jnp.exp(jnp.sum(lf[2:3] * lf[3:4], axis=-1, keepdims=True)) + lam_init)
        o = acc_ref[0] / l_ref[0] - lam * (acc_ref[1] / l_ref[1])
        ms = jnp.mean(o * o, axis=0, keepdims=True)
        o_ref[...] = (o * lax.rsqrt(ms + EPS) * g_ref[...] * (1.0 - lam_init)).astype(o_ref.dtype)


def diff_attention_keymajor(qkv, lam_vec, subln_g, nc, lam_init, col_q, col_k, col_v):
    t = qkv.shape[0]
    tq = _pick(t, (1280, 640, 256, 128))
    tk = _pick(t, (1280, 256, 128))
    nk = t // tk
    hw = 2 * DIFF_DH
    vt = qkv[:, col_v:col_v + DIFF_HEADS * hw].T
    out_t = pl.pallas_call(
        functools.partial(_diff_attn_t_kernel, tq=tq, tk=tk, nc=nc, nk=nk, lam_init=lam_init),
        grid=(DIFF_HEADS, t // tq, nk),
        in_specs=[pl.BlockSpec((4, DIFF_DH), lambda h, i, k: (0, 0)),
                  pl.BlockSpec((hw, 1), lambda h, i, k: (0, 0)),
                  pl.BlockSpec((tq, hw), lambda h, i, k: (i, col_q // hw + h)),
                  pl.BlockSpec((tk, hw), lambda h, i, k: (k, col_k // hw + h)),
                  pl.BlockSpec((hw, tk), lambda h, i, k: (h, k))],
        out_specs=pl.BlockSpec((hw, tq), lambda h, i, k: (h, i)),
        out_shape=jax.ShapeDtypeStruct((DIFF_HEADS * hw, t), BF16),
        scratch_shapes=[pltpu.VMEM((2, 1, tq), F32), pltpu.VMEM((2, 1, tq), F32), pltpu.VMEM((2, hw, tq), F32),
                        pltpu.VMEM((2, tk, tq), BF16), pltpu.VMEM((2, 1, tq), F32)],
        compiler_params=_cp(("arbitrary", "arbitrary", "arbitrary")),
        name="diff_attention",
    )(lam_vec, subln_g.reshape(hw, 1), qkv, qkv, vt)
    return out_t.T


def _swa_kernel(sink_ref, q_ref, kc_ref, k0_ref, k1_ref, k2_ref, vc_ref, v0_ref, v1_ref, v2_ref, o_ref,
                *, nc, t):
    r = pl.program_id(1)
    qb = WINDOW
    q = jnp.concatenate([q_ref[:, g * SWA_DH:(g + 1) * SWA_DH] for g in range(SWA_GROUP)], axis=0)
    sk = sink_ref[0]
    sink = jnp.concatenate([jnp.broadcast_to(sk[g:g + 1, 0:1], (qb, 1)) for g in range(SWA_GROUP)], axis=0)
    nt = (((1,), (1,)), ((), ()))
    local_q = lax.broadcasted_iota(jnp.int32, (qb, 1), 0)
    qrow = jnp.concatenate([r * qb + local_q] * SWA_GROUP, axis=0)
    s_list = [lax.dot_general(q, kc_ref[...], nt, preferred_element_type=F32)]
    for w, k_ref in zip((-1, 0, 1), (k0_ref, k1_ref, k2_ref)):
        s = lax.dot_general(q, k_ref[...], nt, preferred_element_type=F32)
        krow = (r + w) * qb + lax.broadcasted_iota(jnp.int32, (1, qb), 1)
        valid = (jnp.abs(qrow - krow) <= WINDOW) & (krow >= nc) & (krow < t) & (qrow >= nc)
        s_list.append(jnp.where(valid, s, NEG_INF))
    m = sink
    for s in s_list:
        m = jnp.maximum(m, jnp.max(s, axis=-1, keepdims=True))
    denom = jnp.exp(sink - m)
    out = jnp.zeros((SWA_GROUP * qb, SWA_DH), F32)
    for s, v_ref in zip(s_list, (vc_ref, v0_ref, v1_ref, v2_ref)):
        e = jnp.exp(s - m)
        denom = denom + jnp.sum(e, axis=-1, keepdims=True)
        out = out + jnp.dot(e.astype(BF16), v_ref[...], preferred_element_type=F32)
    out = out / denom
    for g in range(SWA_GROUP):
        o_ref[:, g * SWA_DH:(g + 1) * SWA_DH] = out[g * qb:(g + 1) * qb, :].astype(o_ref.dtype)


def swa_attention(qkv, sink, nc, col_q, col_k, col_v):
    t = qkv.shape[0]
    qb = WINDOW
    nb = t // qb
    gw = SWA_GROUP * SWA_DH
    sink_b = jnp.broadcast_to(sink.astype(F32).reshape(SWA_KV_HEADS, SWA_GROUP, 1), (SWA_KV_HEADS, SWA_GROUP, LANES))
    kcb, vcb = col_k // SWA_DH, col_v // SWA_DH

    def win(cb, w):
        return pl.BlockSpec((qb, SWA_DH), lambda kv, r: (jnp.clip(r + w, 0, nb - 1), cb + kv))

    return pl.pallas_call(
        functools.partial(_swa_kernel, nc=nc, t=t),
        grid=(SWA_KV_HEADS, nb),
        in_specs=[pl.BlockSpec((1, SWA_GROUP, LANES), lambda kv, r: (kv, 0, 0)),
                  pl.BlockSpec((qb, gw), lambda kv, r: (r, col_q // gw + kv)),
                  pl.BlockSpec((nc, SWA_DH), lambda kv, r: (0, kcb + kv)),
                  win(kcb, -1), win(kcb, 0), win(kcb, 1),
                  pl.BlockSpec((nc, SWA_DH), lambda kv, r: (0, vcb + kv)),
                  win(vcb, -1), win(vcb, 0), win(vcb, 1)],
        out_specs=pl.BlockSpec((qb, gw), lambda kv, r: (r, kv)),
        out_shape=jax.ShapeDtypeStruct((t, SWA_HEADS * SWA_DH), BF16),
        compiler_params=_cp(("arbitrary", "arbitrary")),
        name="swa_attention",
    )(sink_b, qkv, qkv, qkv, qkv, qkv, qkv, qkv, qkv, qkv)


def _dn_conv_kernel(u_ref, prev_ref, next_ref, cw_ref, o_ref, *, tm, tc, nc, t, n_q, n_qk):
    i = pl.program_id(0)
    j = pl.program_id(1)
    s = CONV_STRIP

    @pl.when(j < n_qk)
    def _():
        scale = jnp.where(j < n_q, DN_DK ** -0.5, 1.0).astype(F32)
        for n, conv in enumerate(_conv3_strips(u_ref, prev_ref, next_ref, cw_ref, i, tm, nc, t)):
            y = _silu(conv)
            for c0 in range(0, tc, DN_DK):
                a = y[:, c0:c0 + DN_DK]
                ss = jnp.sum(a * a, axis=-1, keepdims=True)
                o_ref[n * s:(n + 1) * s, c0:c0 + DN_DK] = (a * lax.rsqrt(ss + EPS) * scale).astype(o_ref.dtype)

    @pl.when(j >= n_qk)
    def _():
        for n, conv in enumerate(_conv3_strips(u_ref, prev_ref, next_ref, cw_ref, i, tm, nc, t)):
            o_ref[n * s:(n + 1) * s, :] = _silu(conv).astype(o_ref.dtype)


def dn_conv(u, conv_w, nc, n_cols):
    t = u.shape[0]
    tm = _pick(t, (1280, 640, 320, 256, 128))
    tc = 1024
    qk_cols = DN_QK_HEADS * DN_DK
    return pl.pallas_call(
        functools.partial(_dn_conv_kernel, tm=tm, tc=tc, nc=nc, t=t, n_q=qk_cols // tc, n_qk=2 * qk_cols // tc),
        grid=(t // tm, n_cols // tc),
        in_specs=_halo_specs(tm, tc, t, 0) + [pl.BlockSpec((3, tc), lambda i, k: (0, k))],
        out_specs=pl.BlockSpec((tm, tc), lambda i, k: (i, k)),
        out_shape=jax.ShapeDtypeStruct((t, n_cols), BF16),
        compiler_params=_cp(("arbitrary", "arbitrary")),
        name="dn_conv",
    )(u, u, u, conv_w)


def _split3(x):
    hi = x.astype(BF16)
    r1 = x - hi.astype(F32)
    mid = r1.astype(BF16)
    lo = (r1 - mid.astype(F32)).astype(BF16)
    return hi, mid, lo


def _delta_kernel(gf_ref, gr_ref, alog_ref, bias_ref, qf_ref, kf_ref, vf_ref, qr_ref, kr_ref, vr_ref,
                       of_ref, or_ref, s_ref):
    s_idx = pl.program_id(0)
    c = DN_CHUNK
    nt = (((1,), (1,)), ((), ()))
    tn = (((0,), (0,)), ((), ()))

    @pl.when(s_idx == 0)
    def _():
        s_ref[...] = jnp.zeros(s_ref.shape, F32)

    def dot(x, y):
        return jnp.dot(x, y, preferred_element_type=F32)

    ri = lax.broadcasted_iota(jnp.int32, (c, c), 0)
    ci = lax.broadcasted_iota(jnp.int32, (c, c), 1)
    eye = ri == ci
    eye_f = eye.astype(F32)
    nh = DN_V_HEADS
    ctx = []
    for d, (g_ref, q_ref, k_ref, v_ref, o_ref) in enumerate(((gf_ref, qf_ref, kf_ref, vf_ref, of_ref),
                                                           (gr_ref, qr_ref, kr_ref, vr_ref, or_ref))):
        shift = (2 * LANES - d * 2 * nh) % LANES
        gt = pltpu.roll(g_ref[...], shift, 1) if shift else g_ref[...]
        alog = pltpu.roll(alog_ref[...], shift, 1) if shift else alog_ref[...]
        bias = pltpu.roll(bias_ref[...], shift, 1) if shift else bias_ref[...]
        z = gt + bias[0:1, :]
        g_all = -jnp.exp(alog[0:1, :]) * (jnp.maximum(z, 0.0) + jnp.log(1.0 + jnp.exp(-jnp.abs(z))))
        order = (ri - ci) * (1 if d == 0 else -1)
        incl = order >= 0
        csum = incl.astype(BF16)
        hi, mid, lo = _split3(g_all)
        ctx.append(dict(beta=jax.nn.sigmoid(gt), gam=dot(csum, hi) + dot(csum, mid) + dot(csum, lo),
                        tot=jnp.sum(g_all, axis=0, keepdims=True), incl=incl, strict=order > 0,
                        q=q_ref, k=k_ref, v=v_ref, o=o_ref))

    heads = [(d, hv) for d in range(2) for hv in range(nh)]
    pairs = [(d, hq) for d in range(2) for hq in range(DN_QK_HEADS)]
    kb = {p: ctx[p[0]]["k"][:, p[1] * DN_DK:(p[1] + 1) * DN_DK] for p in pairs}
    qb = {p: ctx[p[0]]["q"][:, p[1] * DN_DK:(p[1] + 1) * DN_DK] for p in pairs}
    kq = {p: lax.dot_general(jnp.concatenate([kb[p], qb[p]], axis=0), kb[p], nt, preferred_element_type=F32)
          for p in pairs}
    kf = {p: kb[p].astype(F32) for p in pairs}
    qf = {p: qb[p].astype(F32) for p in pairs}
    qk_of = {h: (h[0], h[1] // 2) for h in heads}
    beta = {h: ctx[h[0]]["beta"][:, h[1]:h[1] + 1] for h in heads}
    gam = {h: ctx[h[0]]["gam"][:, nh + h[1]:nh + h[1] + 1] for h in heads}
    tot = {h: ctx[h[0]]["tot"][:, nh + h[1]:nh + h[1] + 1] for h in heads}
    incl = {h: ctx[h[0]]["incl"] for h in heads}
    gam_row = {h: jnp.sum(jnp.where(eye, gam[h], 0.0), axis=0, keepdims=True) for h in heads}
    dec = {h: jnp.where(incl[h], jnp.exp(jnp.where(incl[h], gam[h] - gam_row[h], 0.0)), 0.0) for h in heads}
    a = {h: jnp.where(ctx[h[0]]["strict"], beta[h] * kq[qk_of[h]][:c] * dec[h], 0.0) for h in heads}
    pw = {h: -a[h] for h in heads}
    tinv = {h: eye_f + pw[h] for h in heads}
    for _ in range(int(math.log2(c)) - 2):
        pwb = {h: pw[h].astype(BF16) for h in heads}
        pw = {h: dot(pwb[h], pwb[h]) for h in heads}
        tinv = {h: tinv[h] + dot(tinv[h].astype(BF16), pw[h].astype(BF16)) for h in heads}
    tb = {h: tinv[h].astype(BF16) for h in heads}
    resid = {h: eye_f - tinv[h] - dot(a[h].astype(BF16), tb[h]) for h in heads}
    tinv = {h: tinv[h] + dot(tb[h], resid[h].astype(BF16)) for h in heads}
    tb = {h: tinv[h].astype(BF16) for h in heads}
    eg = {h: jnp.exp(gam[h]) for h in heads}
    vf = {h: ctx[h[0]]["v"][:, h[1] * DN_DV:(h[1] + 1) * DN_DV].astype(F32) for h in heads}
    rhs = {h: jnp.concatenate([kf[qk_of[h]] * (beta[h] * eg[h]), vf[h] * beta[h]], axis=1).astype(BF16) for h in heads}
    sol = {h: dot(tb[h], rhs[h]) for h in heads}
    aqk = {h: jnp.where(incl[h], kq[qk_of[h]][c:] * dec[h], 0.0).astype(BF16) for h in heads}
    wq = {h: jnp.concatenate([sol[h][:, :DN_DK], qf[qk_of[h]] * eg[h]], axis=0).astype(BF16) for h in heads}
    kd = {h: (kf[qk_of[h]] * jnp.exp(tot[h] - gam[h])).astype(BF16) for h in heads}
    state = {h: s_ref[h[0] * nh + h[1]] for h in heads}
    ws = {h: dot(wq[h], state[h].astype(BF16)) for h in heads}
    vnb = {h: (sol[h][:, DN_DK:] - ws[h][:c]).astype(BF16) for h in heads}
    o = {h: ws[h][c:] + dot(aqk[h], vnb[h]) for h in heads}
    upd = {h: lax.dot_general(kd[h], vnb[h], tn, preferred_element_type=F32) for h in heads}
    for h in heads:
        s_ref[h[0] * nh + h[1]] = jnp.exp(tot[h]) * state[h] + upd[h]
        ctx[h[0]]["o"][:, h[1] * DN_DV:(h[1] + 1) * DN_DV] = o[h].astype(of_ref.dtype)


def gated_delta(qkvc, gates, alog, bias, nc):
    t = qkvc.shape[0]
    c = DN_CHUNK
    nch = t // c
    ncc = nc // c
    qw = DN_QK_HEADS * DN_DK
    vw = DN_V_HEADS * DN_DV

    def rev(s):
        return jnp.where(s < ncc, ncc - 1 - s, nch - 1 - (s - ncc))

    def specs(chunk):
        return [pl.BlockSpec((c, qw), lambda s: (chunk(s), 0)),
                pl.BlockSpec((c, qw), lambda s: (chunk(s), 1)),
                pl.BlockSpec((c, vw), lambda s: (chunk(s), 2 * qw // vw))]

    fwd = lambda s: s
    out = jax.ShapeDtypeStruct((t, vw), BF16)
    return pl.pallas_call(
        _delta_kernel,
        grid=(nch,),
        in_specs=[pl.BlockSpec((c, LANES), lambda s: (s, 0)),
                  pl.BlockSpec((c, LANES), lambda s: (rev(s), 0)),
                  pl.BlockSpec((8, LANES), lambda s: (0, 0)),
                  pl.BlockSpec((8, LANES), lambda s: (0, 0))] + specs(fwd) + specs(rev),
        out_specs=[pl.BlockSpec((c, vw), lambda s: (s, 0)), pl.BlockSpec((c, vw), lambda s: (rev(s), 0))],
        out_shape=[out, out],
        scratch_shapes=[pltpu.VMEM((2 * DN_V_HEADS, DN_DK, DN_DV), F32)],
        compiler_params=_cp(("arbitrary",)),
        name="gated_delta",
    )(gates, gates, alog, bias, qkvc, qkvc, qkvc, qkvc, qkvc, qkvc)


def _attn_out_prologue(i, k, oa_ref, ob_ref, *, n_a):
    return jnp.where(k < n_a, oa_ref[...], ob_ref[...])


def _dn_out_prologue(i, k, of_ref, or_ref, z_ref, g_ref, *, tk):
    tm = z_ref.shape[0]
    rs = math.gcd(tm, 64)
    strips = []
    for r0 in range(0, tm, rs):
        parts = []
        for c0 in range(0, tk, DN_DV):
            a = of_ref[r0:r0 + rs, c0:c0 + DN_DV].astype(F32) + or_ref[r0:r0 + rs, c0:c0 + DN_DV].astype(F32)
            ms = jnp.mean(a * a, axis=-1, keepdims=True)
            z = z_ref[r0:r0 + rs, c0:c0 + DN_DV].astype(F32)
            parts.append((a * lax.rsqrt(ms + EPS) * g_ref[...] * _silu(z)).astype(BF16))
        strips.append(jnp.concatenate(parts, axis=-1))
    return jnp.concatenate(strips, axis=0)


def _ffn_prologue(i, k, ug, pg, ng, uu, pu, nu, cwg, cwu, *, tm, nc, t):
    gates = _conv3_strips(ug, pg, ng, cwg, i, tm, nc, t)
    ups = _conv3_strips(uu, pu, nu, cwu, i, tm, nc, t)
    return jnp.concatenate([(_silu(g) * u).astype(BF16) for g, u in zip(gates, ups)], axis=0)


def _final_norm_kernel(x_ref, g_ref, o_ref):
    x = x_ref[...]
    ms = jnp.mean(x * x, axis=-1, keepdims=True)
    o_ref[...] = x * lax.rsqrt(ms + EPS) * g_ref[...]


def final_norm(x, g, nc):
    t, d = x.shape
    tm = math.gcd(nc, 256)
    off = nc // tm
    return pl.pallas_call(
        _final_norm_kernel,
        grid=((t - nc) // tm,),
        in_specs=[pl.BlockSpec((tm, d), lambda i: (i + off, 0)), pl.BlockSpec((1, d), lambda i: (0, 0))],
        out_specs=pl.BlockSpec((tm, d), lambda i: (i, 0)),
        out_shape=jax.ShapeDtypeStruct((t - nc, d), F32),
        compiler_params=_cp(("arbitrary",)),
        name="final_norm",
    )(x, g.reshape(1, d))


def _rope_tables(n, nc):
    rows = n // GRID_W
    inv = ROPE_BASE ** (-jnp.arange(ROPE_AXIS_FREQS, dtype=F32) / ROPE_AXIS_FREQS)
    ar = jnp.arange(rows, dtype=F32)[:, None] * inv
    ac = jnp.arange(GRID_W, dtype=F32)[:, None] * inv

    def per_token(row_table, col_table):
        r = jnp.broadcast_to(row_table[:, None, :], (rows, GRID_W, ROPE_AXIS_FREQS)).reshape(n, ROPE_AXIS_FREQS)
        c = jnp.broadcast_to(col_table[None, :, :], (rows, GRID_W, ROPE_AXIS_FREQS)).reshape(n, ROPE_AXIS_FREQS)
        return r, c

    cr, cc = per_token(jnp.cos(ar), jnp.cos(ac))
    sr, sc = per_token(jnp.sin(ar), jnp.sin(ac))
    cos = jnp.concatenate([cr, cr, cc, cc], axis=-1)
    sin = jnp.concatenate([-sr, sr, -sc, sc], axis=-1)
    cos = jnp.concatenate([jnp.ones((nc, LANES), F32), cos], axis=0)
    sin = jnp.concatenate([jnp.zeros((nc, LANES), F32), sin], axis=0)
    return cos, sin


def kernel(x, c, ctx, c_ctx, w_mod, b_mod, norm1_g, norm2_g, attn_w_in, diff_lambda, diff_subln_g, swa_sink,
           attn_w_out, dn_w_in, dn_conv_w, dn_a_log, dn_dt_bias, dn_norm_g, dn_w_out, ffn_w_up, ffn_conv_w,
           ffn_w_down, final_norm_g):
    assert x.shape[0] == 1, "single-sequence kernel"
    n, d = x.shape[1], x.shape[2]
    nc = ctx.shape[1]
    t = nc + n
    depth = w_mod.shape[0]
    xs = jnp.concatenate([ctx[0], x[0]], axis=0)
    cc = jnp.zeros((8, d), F32).at[0].set(c[0]).at[1].set(c_ctx)
    mods = mod_vectors(cc, w_mod, b_mod)
    cos, sin = _rope_tables(n, nc)

    da, sb_, skv = DIFF_HEADS * 2 * DIFF_DH, SWA_HEADS * SWA_DH, SWA_KV_HEADS * SWA_DH
    col_qa, col_ka, col_qb = 0, da, 2 * da
    col_va = col_qb + sb_
    col_kb = col_va + da
    col_vb = col_kb + skv
    n_attn = col_vb + skv
    cols = jnp.arange(n_attn)
    cmask = ((cols < col_va) | ((cols >= col_kb) & (cols < col_vb))).astype(F32).reshape(1, n_attn)
    cscale = jnp.where(cols < col_ka, DIFF_DH ** -0.5 * LOG2E,
                       jnp.where((cols >= col_qb) & (cols < col_va), SWA_DH ** -0.5, 1.0))
    cscale = cscale.astype(F32).reshape(1, n_attn)

    tm_p = _pick(t, (640, 320, 256, 128))
    hidden = ffn_w_down.shape[1]
    tk_f = _pick(hidden, (1408, 512, 256, 128))
    qkv_cols = 2 * DN_QK_HEADS * DN_DK + DN_V_HEADS * DN_DV
    dn_out = DN_V_HEADS * DN_DV

    for layer in range(depth):
        m = mods[layer]
        sh1, sc1, g1, sh2, sc2, g2 = (m[:, j * d:(j + 1) * d] for j in range(6))
        i = layer // 2
        if layer % 2 == 0:
            lam_init = 0.8 - 0.6 * math.exp(-0.3 * layer)
            w = attn_w_in[i]
            w = jnp.concatenate([w[:, 0:2 * da], w[:, 3 * da:3 * da + sb_], w[:, 2 * da:3 * da],
                                 w[:, 3 * da + sb_:]], axis=1).astype(BF16)
            qkv = norm_mod_matmul(xs, norm1_g[layer], sh1, sc1, w, nc, rope_args=(cos, sin, cmask, cscale),
                                  tn_candidates=(512, 256, 128))
            oa = diff_attention_keymajor(qkv, diff_lambda[i], diff_subln_g[i], nc, lam_init, col_qa, col_ka, col_va)
            ob = swa_attention(qkv, swa_sink[i], nc, col_qb, col_kb, col_vb)
            tk = da
            n_a = da // tk
            xs = proj_residual(
                functools.partial(_attn_out_prologue, n_a=n_a), [oa, ob],
                [pl.BlockSpec((tm_p, tk), lambda r, k: (r, jnp.minimum(k, n_a - 1))),
                 pl.BlockSpec((tm_p, tk), lambda r, k: (r, jnp.maximum(k - n_a, 0)))],
                attn_w_out[i].astype(BF16), xs, g1, nc, tm_p, tk)
        else:
            w = dn_w_in[i]
            u = norm_mod_matmul(xs, norm1_g[layer], sh1, sc1, w[:, :qkv_cols + dn_out].astype(BF16), nc)
            gates = norm_mod_matmul(xs, norm1_g[layer], sh1, sc1, w[:, qkv_cols + dn_out:].astype(BF16), nc, out_dtype=F32)
            qkvc = dn_conv(u, dn_conv_w[i], nc, qkv_cols)
            zero = jnp.zeros((2, DN_V_HEADS), F32)
            alog = jnp.stack([zero, dn_a_log[i].astype(F32)], axis=1).reshape(1, 4 * DN_V_HEADS)
            bias = jnp.stack([zero, dn_dt_bias[i].astype(F32)], axis=1).reshape(1, 4 * DN_V_HEADS)
            o_f, o_r = gated_delta(qkvc, gates, jnp.broadcast_to(alog, (8, LANES)),
                                   jnp.broadcast_to(bias, (8, LANES)), nc)
            tk = 1024
            zoff = qkv_cols // tk
            xs = proj_residual(
                functools.partial(_dn_out_prologue, tk=tk), [o_f, o_r, u, dn_norm_g[i].reshape(1, DN_DV)],
                [pl.BlockSpec((tm_p, tk), lambda r, k: (r, k)),
                 pl.BlockSpec((tm_p, tk), lambda r, k: (r, k)),
                 pl.BlockSpec((tm_p, tk), lambda r, k: (r, zoff + k)),
                 pl.BlockSpec((1, DN_DV), lambda r, k: (0, 0))],
                dn_w_out[i].astype(BF16), xs, g1, nc, tm_p, tk)
        uf = norm_mod_matmul(xs, norm2_g[layer], sh2, sc2, ffn_w_up[layer].astype(BF16), nc)
        cw = ffn_conv_w[layer]
        xs = proj_residual(
            functools.partial(_ffn_prologue, tm=tm_p, nc=nc, t=t), [uf, uf, uf, uf, uf, uf, cw, cw],
            _halo_specs(tm_p, tk_f, t, 0) + _halo_specs(tm_p, tk_f, t, hidden // tk_f)
            + [pl.BlockSpec((3, tk_f), lambda r, k: (0, k)),
               pl.BlockSpec((3, tk_f), lambda r, k: (0, hidden // tk_f + k))],
            ffn_w_down[layer].astype(BF16), xs, g2, nc, tm_p, tk_f)
    return final_norm(xs, final_norm_g, nc)[None]
```
